```python
import math
import jax, jax.numpy as jnp
from jax import lax
import numpy as np

D_MODEL = 1024
BATCH = 8
SEQ = 4096
DEPTH = 4

N_HEADS = 16
HEAD_DIM = D_MODEL // N_HEADS
N_A_LAYERS = DEPTH // 2
N_B_LAYERS = DEPTH - N_A_LAYERS
MOBA_BLOCK = 256
MOBA_TOPK = 3
Q_BLOCK = 128
N_BUCKETS = 32
MAX_DISTANCE = 1024
D_FF = 2816
N_EXPERTS = 8
TOP_K_EXPERTS = 2
D_FF_EXPERT = 3584
N_DENSE = (DEPTH + 1) // 2
N_MOE = DEPTH // 2
EPS = 1e-6
NEG = -1e30

kernel_name = "yoco_moba_stickbreaking_moe_trunk"


def rmsnorm(x, g):
    xf = x.astype(jnp.float32)
    y = xf * lax.rsqrt(jnp.mean(xf * xf, axis=-1, keepdims=True) + EPS)
    return (y * g.astype(jnp.float32)).astype(x.dtype)


def modulate(h, shift, scale):
    return h * (1.0 + scale[:, None, :]) + shift[:, None, :]


def rel_bucket(dist):
    dist = jnp.maximum(dist, 0)
    max_exact = N_BUCKETS // 2
    d = jnp.maximum(dist, 1).astype(jnp.float32)
    large = max_exact + (jnp.log(d / max_exact) / math.log(MAX_DISTANCE / max_exact)
                         * (N_BUCKETS - max_exact)).astype(jnp.int32)
    large = jnp.minimum(large, N_BUCKETS - 1)
    return jnp.where(dist < max_exact, dist, large)


def moba_attend(q, k, v, rel_bias):
    H, S, dh = q.shape
    nb = -(-S // MOBA_BLOCK)
    pad = nb * MOBA_BLOCK - S
    kp = jnp.pad(k, ((0, 0), (0, pad), (0, 0)))
    vp = jnp.pad(v, ((0, 0), (0, pad), (0, 0)))
    kb = kp.reshape(H, nb, MOBA_BLOCK, dh)
    vb = vp.reshape(H, nb, MOBA_BLOCK, dh)
    kmean = jnp.mean(kb, axis=2)
    bias_t = rel_bias.T
    scale = HEAD_DIM ** -0.5
    topk = min(MOBA_TOPK, nb)
    h_idx = jnp.arange(H)[:, None, None]
    blk_ids = jnp.arange(nb)
    offs = jnp.arange(MOBA_BLOCK)

    def per_block(i):
        q0 = i * Q_BLOCK
        qi = lax.dynamic_slice_in_dim(q, q0, Q_BLOCK, axis=1)
        qpos = q0 + jnp.arange(Q_BLOCK)
        cur = q0 // MOBA_BLOCK
        gate = jnp.einsum('hqd,hnd->hqn', qi, kmean).astype(jnp.float32)
        gate = jnp.where(blk_ids[None, None, :] < cur, gate, NEG)
        _, sel = lax.top_k(gate, topk)
        valid = sel < cur
        kg = kb[h_idx, sel]
        vg = vb[h_idx, sel]
        kpos_sel = sel[..., None] * MOBA_BLOCK + offs
        bucket_sel = rel_bucket(qpos[None, :, None, None] - kpos_sel)
        b_sel = jnp.take_along_axis(bias_t, bucket_sel.reshape(H, -1), axis=1).reshape(kpos_sel.shape)
        s_sel = jnp.einsum('hqd,hqjkd->hqjk', qi, kg).astype(jnp.float32) * scale + b_sel
        s_sel = jnp.where(valid[..., None], s_sel, NEG).reshape(H, Q_BLOCK, topk * MOBA_BLOCK)
        k0 = cur * MOBA_BLOCK
        ko = lax.dynamic_slice_in_dim(kp, k0, MOBA_BLOCK, axis=1)
        vo = lax.dynamic_slice_in_dim(vp, k0, MOBA_BLOCK, axis=1)
        d_own = qpos[:, None] - (k0 + offs)[None, :]
        b_own = jnp.take(bias_t, rel_bucket(d_own), axis=1)
        s_own = jnp.einsum('hqd,hkd->hqk', qi, ko).astype(jnp.float32) * scale + b_own
        s_own = jnp.where(d_own[None] >= 0, s_own, NEG)
        p = jax.nn.softmax(jnp.concatenate([s_sel, s_own], axis=-1), axis=-1)
        p_sel = p[..., :topk * MOBA_BLOCK].reshape(H, Q_BLOCK, topk, MOBA_BLOCK).astype(v.dtype)
        p_own = p[..., topk * MOBA_BLOCK:].astype(v.dtype)
        return (jnp.einsum('hqjk,hqjkd->hqd', p_sel, vg)
                + jnp.einsum('hqk,hkd->hqd', p_own, vo))

    outs = lax.map(per_block, jnp.arange(S // Q_BLOCK))
    return outs.transpose(1, 0, 2, 3).reshape(H, S, dh)


def stick_breaking(q, k, v):
    H, S, dh = q.shape
    scale = HEAD_DIM ** -0.5
    kpos = jnp.arange(S)

    def per_block(i):
        q0 = i * Q_BLOCK
        qi = lax.dynamic_slice_in_dim(q, q0, Q_BLOCK, axis=1)
        qpos = q0 + jnp.arange(Q_BLOCK)
        z = jnp.einsum('hqd,hkd->hqk', qi, k).astype(jnp.float32) * scale
        causal = (kpos[None, :] < qpos[:, None])[None]
        log_beta = jax.nn.log_sigmoid(z)
        log_1mb = jnp.where(causal, jax.nn.log_sigmoid(-z), 0.0)
        tail = lax.cumsum(log_1mb, axis=2, reverse=True) - log_1mb
        a = jnp.where(causal, jnp.exp(log_beta + tail), 0.0)
        return jnp.einsum('hqk,hkd->hqd', a.astype(v.dtype), v)

    outs = lax.map(per_block, jnp.arange(S // Q_BLOCK))
    return outs.transpose(1, 0, 2, 3).reshape(H, S, dh)


def split_heads(t):
    B, S, _ = t.shape
    return t.reshape(B, S, N_HEADS, HEAD_DIM).transpose(0, 2, 1, 3)


def merge_heads(t):
    B, H, S, dh = t.shape
    return t.transpose(0, 2, 1, 3).reshape(B, S, H * dh)


def swiglu(t, w13, w2):
    g, u = jnp.split(t @ w13, 2, axis=-1)
    return (jax.nn.silu(g) * u) @ w2


def moe_ffn(h, router_w, w13, w2):
    B, S, D = h.shape
    t = h.reshape(B * S, D)
    logits = (t @ router_w).astype(jnp.float32)
    top_v, top_i = lax.top_k(logits, TOP_K_EXPERTS)
    w = jax.nn.softmax(top_v, axis=-1)
    combine = jnp.sum(jax.nn.one_hot(top_i, N_EXPERTS, dtype=jnp.float32) * w[..., None], axis=1)
    out = jnp.zeros_like(t)
    for e in range(N_EXPERTS):
        out = out + combine[:, e:e + 1].astype(t.dtype) * swiglu(t, w13[e], w2[e])
    return out.reshape(B, S, D)


def setup_inputs(seed: int = 0) -> dict:
    key = jax.random.key(seed)
    ks = jax.random.split(key, 24)
    D = D_MODEL

    def nrm(k, shape, s):
        return jax.random.normal(k, shape, jnp.float32) * s

    return {
        "x": nrm(ks[0], (BATCH, SEQ, D), 1.0),
        "c": nrm(ks[1], (BATCH, D), 1.0),
        "rel_bias": nrm(ks[2], (N_BUCKETS, N_HEADS), 0.5),
        "mod_w": nrm(ks[3], (DEPTH, D, 6 * D), 0.5 * D ** -0.5),
        "mod_b": nrm(ks[4], (DEPTH, 6 * D), 0.02),
        "norm_mix_g": 1.0 + nrm(ks[5], (DEPTH, D), 0.02),
        "norm_ffn_g": 1.0 + nrm(ks[6], (DEPTH, D), 0.02),
        "a_wqkv": nrm(ks[7], (N_A_LAYERS, D, 3 * D), D ** -0.5),
        "a_wo": nrm(ks[8], (N_A_LAYERS, D, D), D ** -0.5),
        "kv_norm_g": 1.0 + nrm(ks[9], (D,), 0.02),
        "kv_mod_w": nrm(ks[10], (D, 2 * D), 0.5 * D ** -0.5),
        "kv_mod_b": nrm(ks[11], (2 * D,), 0.02),
        "b_wkv": nrm(ks[12], (D, 2 * D), D ** -0.5),
        "b_wq": nrm(ks[13], (N_B_LAYERS, D, D), D ** -0.5),
        "b_wo": nrm(ks[14], (N_B_LAYERS, D, D), D ** -0.5),
        "ffn_w13": nrm(ks[15], (N_DENSE, D, 2 * D_FF), D ** -0.5),
        "ffn_w2": nrm(ks[16], (N_DENSE, D_FF, D), D_FF ** -0.5),
        "router_w": nrm(ks[17], (N_MOE, D, N_EXPERTS), D ** -0.5),
        "moe_w13": nrm(ks[18], (N_MOE, N_EXPERTS, D, 2 * D_FF_EXPERT), D ** -0.5),
        "moe_w2": nrm(ks[19], (N_MOE, N_EXPERTS, D_FF_EXPERT, D), D_FF_EXPERT ** -0.5),
        "final_norm_g": 1.0 + nrm(ks[20], (D,), 0.02),
    }


def reference(x, c, rel_bias, mod_w, mod_b, norm_mix_g, norm_ffn_g, a_wqkv, a_wo,
              kv_norm_g, kv_mod_w, kv_mod_b, b_wkv, b_wq, b_wo, ffn_w13, ffn_w2,
              router_w, moe_w13, moe_w2, final_norm_g):
    B, S, D = x.shape
    silu_c = jax.nn.silu(c)
    k_sh = None
    v_sh = None
    for l in range(DEPTH):
        mod = silu_c @ mod_w[l] + mod_b[l]
        sh_a, sc_a, g_a, sh_f, sc_f, g_f = jnp.split(mod, 6, axis=-1)
        h = modulate(rmsnorm(x, norm_mix_g[l]), sh_a, sc_a)
        if l < N_A_LAYERS:
            q, k, v = jnp.split(h @ a_wqkv[l], 3, axis=-1)
            attn = lax.map(lambda a: moba_attend(a[0], a[1], a[2], rel_bias),
                           (split_heads(q), split_heads(k), split_heads(v)))
            y = merge_heads(attn) @ a_wo[l]
        else:
            j = l - N_A_LAYERS
            if j == 0:
                kv_mod = silu_c @ kv_mod_w + kv_mod_b
                kv_sh, kv_sc = jnp.split(kv_mod, 2, axis=-1)
                hk = modulate(rmsnorm(x, kv_norm_g), kv_sh, kv_sc)
                k_s, v_s = jnp.split(hk @ b_wkv, 2, axis=-1)
                k_sh = split_heads(k_s)
                v_sh = split_heads(v_s)
            q = split_heads(h @ b_wq[j])
            attn = lax.map(lambda a: stick_breaking(a[0], a[1], a[2]), (q, k_sh, v_sh))
            y = merge_heads(attn) @ b_wo[j]
        x = x + g_a[:, None, :] * y
        h = modulate(rmsnorm(x, norm_ffn_g[l]), sh_f, sc_f)
        if l % 2 == 0:
            y = swiglu(h, ffn_w13[l // 2], ffn_w2[l // 2])
        else:
            y = moe_ffn(h, router_w[l // 2], moe_w13[l // 2], moe_w2[l // 2])
        x = x + g_f[:, None, :] * y
    return rmsnorm(x, final_norm_g)
```

```python
import functools
import math

import numpy as np
import jax
import jax.numpy as jnp
from jax import lax
from jax.experimental import pallas as pl
from jax.experimental.pallas import tpu as pltpu

D_MODEL = 1024
DEPTH = 4
N_HEADS = 16
HEAD_DIM = D_MODEL // N_HEADS
N_A_LAYERS = DEPTH // 2
MOBA_BLOCK = 256
MOBA_TOPK = 3
N_BUCKETS = 32
MAX_DISTANCE = 1024
N_EXPERTS = 8
TOP_K_EXPERTS = 2
EPS = 1e-6
NEG = -1e30

LANES = 128
HEADS_PER_TILE = LANES // HEAD_DIM
ATT_BLOCK = 256
N_BIAS_TILES = 6
SB_DEAD_LOG = -105.0
VMEM_LIMIT = 56 * 1024 * 1024

F32 = jnp.float32
BF16 = jnp.bfloat16
HIGHEST = lax.Precision.HIGHEST
_NT = (((1,), (1,)), ((), ()))


def _params(n_axes):
    return pltpu.CompilerParams(dimension_semantics=("arbitrary",) * n_axes,
                                vmem_limit_bytes=VMEM_LIMIT)


def _norm_mod(x, g, sh, sc):
    ms = jnp.mean(x * x, axis=-1, keepdims=True)
    y = (x * lax.rsqrt(ms + EPS)) * g
    return y * (1.0 + sc) + sh


def _mod_body(c_ref, w_ref, b_ref, o_ref):
    c = c_ref[...]
    s = c * jax.nn.sigmoid(c)
    o_ref[0] = jnp.dot(s, w_ref[0], preferred_element_type=F32, precision=HIGHEST) + b_ref[0]


def _modulation(c, w, b):
    L, D, M = w.shape
    B = c.shape[0]
    tn = 1024
    return pl.pallas_call(
        _mod_body,
        grid=(L, M // tn),
        in_specs=[pl.BlockSpec((B, D), lambda l, j: (0, 0)),
                  pl.BlockSpec((1, D, tn), lambda l, j: (l, 0, j)),
                  pl.BlockSpec((1, 1, tn), lambda l, j: (l, 0, j))],
        out_specs=pl.BlockSpec((1, B, tn), lambda l, j: (l, 0, j)),
        out_shape=jax.ShapeDtypeStruct((L, B, M), F32),
        compiler_params=_params(2),
        name="adaln_mod",
    )(c, w, b.reshape(L, 1, M))


def _nm_mm_body(x_ref, g_ref, sh_ref, sc_ref, w_ref, o_ref, h_sc):
    @pl.when(pl.program_id(2) == 0)
    def _():
        h_sc[...] = _norm_mod(x_ref[0], g_ref[...], sh_ref[0], sc_ref[0]).astype(BF16)

    o_ref[0] = jnp.dot(h_sc[...], w_ref[...], preferred_element_type=F32).astype(o_ref.dtype)


def _norm_mod_matmul(x, g, sh, sc, w):
    B, S, D = x.shape
    N = w.shape[1]
    tm, tn = 1024, 1024
    return pl.pallas_call(
        _nm_mm_body,
        grid=(B, S // tm, N // tn),
        in_specs=[pl.BlockSpec((1, tm, D), lambda b, i, j: (b, i, 0)),
                  pl.BlockSpec((1, D), lambda b, i, j: (0, 0)),
                  pl.BlockSpec((1, 1, D), lambda b, i, j: (b, 0, 0)),
                  pl.BlockSpec((1, 1, D), lambda b, i, j: (b, 0, 0)),
                  pl.BlockSpec((D, tn), lambda b, i, j: (0, j))],
        out_specs=pl.BlockSpec((1, tm, tn), lambda b, i, j: (b, i, j)),
        out_shape=jax.ShapeDtypeStruct((B, S, N), BF16),
        scratch_shapes=[pltpu.VMEM((tm, D), BF16)],
        compiler_params=_params(3),
        name="norm_mod_proj",
    )(x, g.reshape(1, D), sh, sc, w)


def _mm_res_body(a_ref, w_ref, r_ref, gate_ref, o_ref):
    acc = jnp.dot(a_ref[0], w_ref[...], preferred_element_type=F32)
    o_ref[0] = r_ref[0] + gate_ref[0] * acc


def _matmul_gated_residual(a, w, res, gate):
    B, S, K = a.shape
    N = w.shape[1]
    tm, tn = 1024, 1024
    return pl.pallas_call(
        _mm_res_body,
        grid=(B, S // tm, N // tn),
        in_specs=[pl.BlockSpec((1, tm, K), lambda b, i, j: (b, i, 0)),
                  pl.BlockSpec((K, tn), lambda b, i, j: (0, j)),
                  pl.BlockSpec((1, tm, tn), lambda b, i, j: (b, i, j)),
                  pl.BlockSpec((1, 1, tn), lambda b, i, j: (b, 0, j))],
        out_specs=pl.BlockSpec((1, tm, tn), lambda b, i, j: (b, i, j)),
        out_shape=jax.ShapeDtypeStruct((B, S, N), F32),
        compiler_params=_params(3),
        name="out_proj_residual",
    )(a, w, res, gate)


def _rel_bucket_table():
    i = np.arange(ATT_BLOCK)[:, None]
    j = np.arange(ATT_BLOCK)[None, :]
    tiles = []
    max_exact = N_BUCKETS // 2
    for delta in range(N_BIAS_TILES):
        dist = np.maximum(delta * ATT_BLOCK + i - j, 0)
        d = np.maximum(dist, 1).astype(np.float32)
        ratio = np.log(d / np.float32(max_exact)) / np.float32(math.log(MAX_DISTANCE / max_exact))
        large = max_exact + (ratio * np.float32(N_BUCKETS - max_exact)).astype(np.int32)
        large = np.minimum(large, N_BUCKETS - 1)
        tiles.append(np.where(dist < max_exact, dist, large))
    tbl = np.stack(tiles).astype(np.int32)
    assert (tbl[N_BIAS_TILES - 1] == N_BUCKETS - 1).all()
    return tbl


def _moba_body(q_ref, k_ref, v_ref, bias_ref, o_ref, kmean_sc, *, nb):
    c = pl.program_id(2)
    blk_sz = ATT_BLOCK

    @pl.when(c == 0)
    def _():
        for n in range(nb):
            kb = k_ref[0, n * blk_sz:(n + 1) * blk_sz, :].astype(F32)
            kmean_sc[n:n + 1, :] = jnp.sum(kb, axis=0, keepdims=True) * (1.0 / blk_sz)

    q = q_ref[0]
    lane = lax.broadcasted_iota(jnp.int32, (blk_sz, LANES), 1)
    blk = lax.broadcasted_iota(jnp.int32, (blk_sz, nb), 1)
    rows = lax.broadcasted_iota(jnp.int32, (blk_sz, blk_sz), 0)
    cols = lax.broadcasted_iota(jnp.int32, (blk_sz, blk_sz), 1)
    causal = cols <= rows
    kmean = kmean_sc[...]
    scale = HEAD_DIM ** -0.5

    qs, sels = [], []
    for hh in range(HEADS_PER_TILE):
        in_head = (lane >= hh * HEAD_DIM) & (lane < (hh + 1) * HEAD_DIM)
        qh = jnp.where(in_head, q, jnp.zeros_like(q))
        gate = lax.dot_general(qh.astype(F32), kmean, _NT,
                               preferred_element_type=F32, precision=HIGHEST)
        gate = jnp.where(blk < c, gate, NEG)
        rank = jnp.zeros((blk_sz, nb), jnp.int32)
        for m in range(nb):
            gm = gate[:, m:m + 1]
            beats = (gm > gate) | ((gm == gate) & (blk > m))
            rank = rank + beats.astype(jnp.int32)
        sels.append(((rank < MOBA_TOPK) & (blk < c)).astype(F32))
        qs.append((qh.astype(F32) * scale).astype(BF16))

    c0 = pl.multiple_of(c * blk_sz, blk_sz)
    kb0 = k_ref[0, pl.ds(c0, blk_sz), :]
    vb0 = v_ref[0, pl.ds(c0, blk_sz), :]
    carry0 = []
    for hh in range(HEADS_PER_TILE):
        s = lax.dot_general(qs[hh], kb0, _NT, preferred_element_type=F32) + bias_ref[hh, 0]
        s = jnp.where(causal, s, NEG)
        m = jnp.max(s, axis=1, keepdims=True)
        p = jnp.exp(s - m)
        l = jnp.sum(p, axis=1, keepdims=True)
        acc = jnp.dot(p.astype(BF16), vb0, preferred_element_type=F32)
        carry0 += [m, l, acc]

    def body(n, carry):
        n0 = pl.multiple_of(n * blk_sz, blk_sz)
        kb = k_ref[0, pl.ds(n0, blk_sz), :]
        vb = v_ref[0, pl.ds(n0, blk_sz), :]
        dl = jnp.minimum(c - n, N_BIAS_TILES - 1)
        out = []
        for hh in range(HEADS_PER_TILE):
            m, l, acc = carry[3 * hh:3 * hh + 3]
            picked = jnp.sum(jnp.where(blk == n, sels[hh], 0.0), axis=1, keepdims=True) > 0.5
            s = lax.dot_general(qs[hh], kb, _NT, preferred_element_type=F32)
            s = s + bias_ref[hh, pl.ds(dl, 1)][0]
            s = jnp.where(picked, s, NEG)
            m_new = jnp.maximum(m, jnp.max(s, axis=1, keepdims=True))
            alpha = jnp.exp(m - m_new)
            p = jnp.exp(s - m_new)
            l = alpha * l + jnp.sum(p, axis=1, keepdims=True)
            acc = alpha * acc + jnp.dot(p.astype(BF16), vb, preferred_element_type=F32)
            out += [m_new, l, acc]
        return tuple(out)

    fin = lax.fori_loop(0, c, body, tuple(carry0))
    o = jnp.where(lane < HEAD_DIM, fin[2] / fin[1], fin[5] / fin[4])
    o_ref[0] = o.astype(o_ref.dtype)


def _moba_attention(qkv, bias_tiles):
    B, S, D3 = qkv.shape
    D = D3 // 3
    T = ATT_BLOCK
    nb = S // T
    n_hp = D // LANES
    return pl.pallas_call(
        functools.partial(_moba_body, nb=nb),
        grid=(B, n_hp, S // T),
        in_specs=[pl.BlockSpec((1, T, LANES), lambda b, h, i: (b, i, h)),
                  pl.BlockSpec((1, S, LANES), lambda b, h, i: (b, 0, n_hp + h)),
                  pl.BlockSpec((1, S, LANES), lambda b, h, i: (b, 0, 2 * n_hp + h)),
                  pl.BlockSpec((HEADS_PER_TILE, N_BIAS_TILES, T, T), lambda b, h, i: (h, 0, 0, 0))],
        out_specs=pl.BlockSpec((1, T, LANES), lambda b, h, i: (b, i, h)),
        out_shape=jax.ShapeDtypeStruct((B, S, D), BF16),
        scratch_shapes=[pltpu.VMEM((nb, LANES), F32)],
        compiler_params=_params(3),
        name="moba_attention",
    )(qkv, qkv, qkv, bias_tiles)


def _sb_body(q_ref, k_ref, v_ref, o_ref):
    c = pl.program_id(2)
    T = ATT_BLOCK
    q = q_ref[0]
    lane = lax.broadcasted_iota(jnp.int32, (T, LANES), 1)
    rows = lax.broadcasted_iota(jnp.int32, (T, T), 0)
    cols = lax.broadcasted_iota(jnp.int32, (T, T), 1)
    strict = cols < rows
    upper = (rows > cols).astype(BF16)
    scale = HEAD_DIM ** -0.5
    qs = []
    for hh in range(HEADS_PER_TILE):
        in_head = (lane >= hh * HEAD_DIM) & (lane < (hh + 1) * HEAD_DIM)
        qs.append((jnp.where(in_head, q, jnp.zeros_like(q)).astype(F32) * scale).astype(BF16))

    def block(n, state, diag):
        n0 = pl.multiple_of(n * T, T)
        kb = k_ref[0, pl.ds(n0, T), :]
        vb = v_ref[0, pl.ds(n0, T), :]
        out = []
        for hh in range(HEADS_PER_TILE):
            r, acc = state[2 * hh:2 * hh + 2]
            z = lax.dot_general(qs[hh], kb, _NT, preferred_element_type=F32)
            sp = jnp.log(1.0 + jnp.exp(-jnp.abs(z)))
            lb = jnp.minimum(z, 0.0) - sp
            l1 = lb - z
            if diag:
                l1 = jnp.where(strict, l1, 0.0)
            hi = l1.astype(BF16)
            lo = (l1 - hi.astype(F32)).astype(BF16)
            tail = (jnp.dot(hi, upper, preferred_element_type=F32)
                    + jnp.dot(lo, upper, preferred_element_type=F32))
            a = jnp.exp(lb + tail + r)
            if diag:
                a = jnp.where(strict, a, 0.0)
            acc = acc + jnp.dot(a.astype(BF16), vb, preferred_element_type=F32)
            r = r + tail[:, 0:1] + l1[:, 0:1]
            out += [r, acc]
        return tuple(out)

    zero_r = jnp.zeros((T, 1), F32)
    zero_acc = jnp.zeros((T, LANES), F32)
    state = block(c, (zero_r, zero_acc) * HEADS_PER_TILE, True)

    def cond(carry):
        i = carry[0]
        live = jnp.max(jnp.maximum(carry[1], carry[3])) > SB_DEAD_LOG
        return jnp.logical_and(i < c, live)

    def body(carry):
        i = carry[0]
        return (i + 1,) + block(c - 1 - i, carry[1:], False)

    fin = lax.while_loop(cond, body, (jnp.int32(0),) + state)
    o = jnp.where(lane < HEAD_DIM, fin[2], fin[4])
    o_ref[0] = o.astype(o_ref.dtype)


def _sb_attention(q, kv):
    B, S, D = q.shape
    T = ATT_BLOCK
    n_hp = D // LANES
    return pl.pallas_call(
        _sb_body,
        grid=(B, n_hp, S // T),
        in_specs=[pl.BlockSpec((1, T, LANES), lambda b, h, i: (b, i, h)),
                  pl.BlockSpec((1, S, LANES), lambda b, h, i: (b, 0, h)),
                  pl.BlockSpec((1, S, LANES), lambda b, h, i: (b, 0, n_hp + h))],
        out_specs=pl.BlockSpec((1, T, LANES), lambda b, h, i: (b, i, h)),
        out_shape=jax.ShapeDtypeStruct((B, S, D), BF16),
        compiler_params=_params(3),
        name="stick_breaking_attention",
    )(q, kv, kv)


def _swiglu_chunk(xb, wg, wu, w2):
    g = jnp.dot(xb, wg, preferred_element_type=F32)
    u = jnp.dot(xb, wu, preferred_element_type=F32)
    a = (g * jax.nn.sigmoid(g) * u).astype(BF16)
    return jnp.dot(a, w2, preferred_element_type=F32)


def _dense_ffn_body(x_ref, g_ref, sh_ref, sc_ref, gate_ref, wg_ref, wu_ref, w2_ref, o_ref,
                    xb_sc, acc_sc, *, n_chunks):
    j = pl.program_id(1)

    @pl.when(j == 0)
    def _():
        xb_sc[...] = _norm_mod(x_ref[...], g_ref[...], sh_ref[0], sc_ref[0]).astype(BF16)

    y = _swiglu_chunk(xb_sc[...], wg_ref[0], wu_ref[0], w2_ref[0])

    @pl.when(j == 0)
    def _():
        acc_sc[...] = y

    @pl.when(j > 0)
    def _():
        acc_sc[...] += y

    @pl.when(j == n_chunks - 1)
    def _():
        o_ref[...] = x_ref[...] + gate_ref[0] * acc_sc[...]


def _dense_ffn(x, g, sh, sc, gate, w13, w2):
    B, S, D = x.shape
    Fh = w2.shape[1]
    tm, tf = 512, 1408
    n_chunks = Fh // tf
    per_b = S // tm
    xf = x.reshape(B * S, D)
    out = pl.pallas_call(
        functools.partial(_dense_ffn_body, n_chunks=n_chunks),
        grid=(B * S // tm, n_chunks),
        in_specs=[pl.BlockSpec((tm, D), lambda i, j: (i, 0)),
                  pl.BlockSpec((1, D), lambda i, j: (0, 0)),
                  pl.BlockSpec((1, 1, D), lambda i, j: (i // per_b, 0, 0)),
                  pl.BlockSpec((1, 1, D), lambda i, j: (i // per_b, 0, 0)),
                  pl.BlockSpec((1, 1, D), lambda i, j: (i // per_b, 0, 0)),
                  pl.BlockSpec((1, D, tf), lambda i, j: (0, 0, j)),
                  pl.BlockSpec((1, D, tf), lambda i, j: (0, 0, j + n_chunks)),
                  pl.BlockSpec((1, tf, D), lambda i, j: (0, j, 0))],
        out_specs=pl.BlockSpec((tm, D), lambda i, j: (i, 0)),
        out_shape=jax.ShapeDtypeStruct((B * S, D), F32),
        scratch_shapes=[pltpu.VMEM((tm, D), BF16), pltpu.VMEM((tm, D), F32)],
        compiler_params=_params(2),
        name="dense_swiglu",
    )(xf, g.reshape(1, D), sh, sc, gate, w13, w13, w2)
    return out.reshape(B, S, D)


def _router_body(x_ref, g_ref, sh_ref, sc_ref, rwt_ref, h_ref, ei_ref, w_ref):
    h = _norm_mod(x_ref[0], g_ref[...], sh_ref[0], sc_ref[0])
    h_ref[...] = h
    logits = lax.dot_general(rwt_ref[...], h, _NT, preferred_element_type=F32,
                             precision=HIGHEST)
    n_e = logits.shape[0]
    eio = lax.broadcasted_iota(jnp.int32, logits.shape, 0)
    m1 = jnp.max(logits, axis=0, keepdims=True)
    i1 = jnp.min(jnp.where(logits == m1, eio, n_e), axis=0, keepdims=True)
    rest = jnp.where(eio == i1, -jnp.inf, logits)
    m2 = jnp.max(rest, axis=0, keepdims=True)
    i2 = jnp.min(jnp.where(rest == m2, eio, n_e), axis=0, keepdims=True)
    t = jnp.exp(m2 - m1)
    den = 1.0 + t
    ei_ref[...] = jnp.concatenate([i1, i2], axis=0)
    w_ref[...] = jnp.concatenate([1.0 / den, t / den], axis=0)


def _router(x, g, sh, sc, router_w):
    B, S, D = x.shape
    E = router_w.shape[1]
    tm = 512
    per_b = S // tm
    return pl.pallas_call(
        _router_body,
        grid=(B, per_b),
        in_specs=[pl.BlockSpec((1, tm, D), lambda b, i: (b, i, 0)),
                  pl.BlockSpec((1, D), lambda b, i: (0, 0)),
                  pl.BlockSpec((1, 1, D), lambda b, i: (b, 0, 0)),
                  pl.BlockSpec((1, 1, D), lambda b, i: (b, 0, 0)),
                  pl.BlockSpec((E, D), lambda b, i: (0, 0))],
        out_specs=[pl.BlockSpec((tm, D), lambda b, i: (b * per_b + i, 0)),
                   pl.BlockSpec((TOP_K_EXPERTS, tm), lambda b, i: (0, b * per_b + i)),
                   pl.BlockSpec((TOP_K_EXPERTS, tm), lambda b, i: (0, b * per_b + i))],
        out_shape=[jax.ShapeDtypeStruct((B * S, D), F32),
                   jax.ShapeDtypeStruct((TOP_K_EXPERTS, B * S), jnp.int32),
                   jax.ShapeDtypeStruct((TOP_K_EXPERTS, B * S), F32)],
        compiler_params=_params(2),
        name="moe_router",
    )(x, g.reshape(1, D), sh, sc, router_w.T)


def _route_plan(ei, tm, n_tiles):
    n_tok = ei.shape[1]
    e_flat = ei.reshape(-1)
    onehot = (e_flat[:, None] == jnp.arange(N_EXPERTS, dtype=jnp.int32)[None, :]).astype(jnp.int32)
    csum = jnp.cumsum(onehot, axis=0)
    rank = jnp.take_along_axis(csum, e_flat[:, None], axis=1)[:, 0] - 1
    counts = csum[-1]
    padded = ((counts + tm - 1) // tm) * tm
    ends = jnp.cumsum(padded)
    offs = ends - padded
    pos = (offs[e_flat] + rank).astype(jnp.int32)
    tok = jnp.tile(jnp.arange(n_tok, dtype=jnp.int32), TOP_K_EXPERTS)
    src = jnp.zeros((n_tiles * tm,), jnp.int32).at[pos].set(tok)
    starts = jnp.arange(n_tiles, dtype=jnp.int32) * tm
    t_exp = jnp.minimum(jnp.searchsorted(ends, starts, side="right"), N_EXPERTS - 1).astype(jnp.int32)
    t_valid = (starts < ends[-1]).astype(jnp.int32)
    return pos, src, t_exp, t_valid


def _gather_body(idx_ref, src_hbm, o_ref, sem, *, rows):
    base = pl.program_id(0) * rows

    def start(r, _):
        t = idx_ref[base + r]
        pltpu.make_async_copy(src_hbm.at[pl.ds(t, 1)], o_ref.at[pl.ds(r, 1)], sem).start()
        return 0

    lax.fori_loop(0, rows, start, 0)

    def wait(r, _):
        pltpu.make_async_copy(src_hbm.at[pl.ds(0, 1)], o_ref.at[pl.ds(r, 1)], sem).wait()
        return 0

    lax.fori_loop(0, rows, wait, 0)


def _gather_rows(src, idx):
    P = idx.shape[0]
    D = src.shape[1]
    rows = 512
    return pl.pallas_call(
        functools.partial(_gather_body, rows=rows),
        grid_spec=pltpu.PrefetchScalarGridSpec(
            num_scalar_prefetch=1,
            grid=(P // rows,),
            in_specs=[pl.BlockSpec(memory_space=pl.ANY)],
            out_specs=pl.BlockSpec((rows, D), lambda i, idx: (i, 0)),
            scratch_shapes=[pltpu.SemaphoreType.DMA]),
        out_shape=jax.ShapeDtypeStruct((P, D), src.dtype),
        compiler_params=_params(1),
        name="moe_gather",
    )(idx, src)


def _moe_ffn_body(te_ref, tv_ref, x_ref, wg_ref, wu_ref, w2_ref, o_ref, xb_sc, acc_sc, *, n_chunks):
    i = pl.program_id(0)
    j = pl.program_id(1)
    valid = tv_ref[i] > 0

    @pl.when(jnp.logical_and(valid, j == 0))
    def _():
        xb_sc[...] = x_ref[...].astype(BF16)

    @pl.when(valid)
    def _():
        y = _swiglu_chunk(xb_sc[...], wg_ref[0], wu_ref[0], w2_ref[0])

        @pl.when(j == 0)
        def _():
            acc_sc[...] = y

        @pl.when(j > 0)
        def _():
            acc_sc[...] += y

    @pl.when(j == n_chunks - 1)
    def _():
        @pl.when(valid)
        def _():
            o_ref[...] = acc_sc[...]

        @pl.when(jnp.logical_not(valid))
        def _():
            o_ref[...] = jnp.zeros_like(o_ref)


def _moe_ffn(xs, t_exp, t_valid, w13, w2, tm):
    P, D = xs.shape
    Fh = w2.shape[1]
    tf = 512
    n_chunks = Fh // tf
    return pl.pallas_call(
        functools.partial(_moe_ffn_body, n_chunks=n_chunks),
        grid_spec=pltpu.PrefetchScalarGridSpec(
            num_scalar_prefetch=2,
            grid=(P // tm, n_chunks),
            in_specs=[pl.BlockSpec((tm, D), lambda i, j, te, tv: (i, 0)),
                      pl.BlockSpec((1, D, tf), lambda i, j, te, tv: (te[i], 0, j * tv[i])),
                      pl.BlockSpec((1, D, tf), lambda i, j, te, tv: (te[i], 0, j * tv[i] + n_chunks)),
                      pl.BlockSpec((1, tf, D), lambda i, j, te, tv: (te[i], j * tv[i], 0))],
            out_specs=pl.BlockSpec((tm, D), lambda i, j, te, tv: (i, 0)),
            scratch_shapes=[pltpu.VMEM((tm, D), BF16), pltpu.VMEM((tm, D), F32)]),
        out_shape=jax.ShapeDtypeStruct((P, D), F32),
        compiler_params=_params(2),
        name="moe_expert_swiglu",
    )(t_exp, t_valid, xs, w13, w13, w2)


def _combine_body(pos_ref, y_hbm, x_ref, w_ref, gate_ref, fg_ref, o_ref, ybuf, sem, *,
                  rows, n_tok, final_norm):
    base = pl.program_id(0) * rows

    def start(r, _):
        for k in range(TOP_K_EXPERTS):
            p = pos_ref[k * n_tok + base + r]
            pltpu.make_async_copy(y_hbm.at[pl.ds(p, 1)], ybuf.at[k, pl.ds(r, 1)], sem).start()
        return 0

    lax.fori_loop(0, rows, start, 0)

    def wait(r, _):
        for k in range(TOP_K_EXPERTS):
            pltpu.make_async_copy(y_hbm.at[pl.ds(0, 1)], ybuf.at[k, pl.ds(r, 1)], sem).wait()
        return 0

    lax.fori_loop(0, rows, wait, 0)

    w = w_ref[...]
    y = w[:, 0:1] * ybuf[0] + w[:, 1:2] * ybuf[1]
    out = x_ref[...] + gate_ref[0] * y
    if final_norm:
        ms = jnp.mean(out * out, axis=-1, keepdims=True)
        out = (out * lax.rsqrt(ms + EPS)) * fg_ref[...]
    o_ref[...] = out


def _moe_combine(x, ys, pos, wts, gate, final_g, final_norm):
    B, S, D = x.shape
    n_tok = B * S
    rows = 256
    per_b = S // rows
    out = pl.pallas_call(
        functools.partial(_combine_body, rows=rows, n_tok=n_tok, final_norm=final_norm),
        grid_spec=pltpu.PrefetchScalarGridSpec(
            num_scalar_prefetch=1,
            grid=(n_tok // rows,),
            in_specs=[pl.BlockSpec(memory_space=pl.ANY),
                      pl.BlockSpec((rows, D), lambda i, pos: (i, 0)),
                      pl.BlockSpec((rows, TOP_K_EXPERTS), lambda i, pos: (i, 0)),
                      pl.BlockSpec((1, 1, D), lambda i, pos: (i // per_b, 0, 0)),
                      pl.BlockSpec((1, D), lambda i, pos: (0, 0))],
            out_specs=pl.BlockSpec((rows, D), lambda i, pos: (i, 0)),
            scratch_shapes=[pltpu.VMEM((TOP_K_EXPERTS, rows, D), F32), pltpu.SemaphoreType.DMA]),
        out_shape=jax.ShapeDtypeStruct((n_tok, D), F32),
        compiler_params=_params(1),
        name="moe_combine",
    )(pos, ys, x.reshape(n_tok, D), wts, gate, final_g.reshape(1, D))
    return out.reshape(B, S, D)


def _moe_layer(x, g, sh, sc, gate, router_w, w13, w2, final_g, final_norm):
    B, S, D = x.shape
    n_tok = B * S
    tm = 1024
    n_tiles = TOP_K_EXPERTS * n_tok // tm + N_EXPERTS
    h, ei, wts = _router(x, g, sh, sc, router_w)
    pos, src, t_exp, t_valid = _route_plan(ei, tm, n_tiles)
    xs = _gather_rows(h, src)
    ys = _moe_ffn(xs, t_exp, t_valid, w13, w2, tm)
    return _moe_combine(x, ys, pos, wts.T, gate, final_g, final_norm)


def kernel(x, c, rel_bias, mod_w, mod_b, norm_mix_g, norm_ffn_g, a_wqkv, a_wo, kv_norm_g, kv_mod_w,
           kv_mod_b, b_wkv, b_wq, b_wo, ffn_w13, ffn_w2, router_w, moe_w13, moe_w2, final_norm_g):
    D = D_MODEL
    mods = _modulation(c, mod_w, mod_b)
    kv_mod = _modulation(c, kv_mod_w[None], kv_mod_b[None])[0]
    bias_tiles = jnp.transpose(jnp.take(rel_bias, jnp.asarray(_rel_bucket_table()), axis=0),
                               (3, 0, 1, 2))
    kv = None
    for l in range(DEPTH):
        sh_a, sc_a, g_a, sh_f, sc_f, g_f = [mods[l][:, None, i * D:(i + 1) * D] for i in range(6)]
        if l < N_A_LAYERS:
            qkv = _norm_mod_matmul(x, norm_mix_g[l], sh_a, sc_a, a_wqkv[l].astype(BF16))
            attn = _moba_attention(qkv, bias_tiles)
            x = _matmul_gated_residual(attn, a_wo[l].astype(BF16), x, g_a)
        else:
            j = l - N_A_LAYERS
            if j == 0:
                kv = _norm_mod_matmul(x, kv_norm_g, kv_mod[:, None, :D], kv_mod[:, None, D:],
                                      b_wkv.astype(BF16))
            q = _norm_mod_matmul(x, norm_mix_g[l], sh_a, sc_a, b_wq[j].astype(BF16))
            attn = _sb_attention(q, kv)
            x = _matmul_gated_residual(attn, b_wo[j].astype(BF16), x, g_a)
        if l % 2 == 0:
            x = _dense_ffn(x, norm_ffn_g[l], sh_f, sc_f, g_f,
                           ffn_w13[l // 2][None].astype(BF16), ffn_w2[l // 2][None].astype(BF16))
        else:
            x = _moe_layer(x, norm_ffn_g[l], sh_f, sc_f, g_f, router_w[l // 2],
                           moe_w13[l // 2].astype(BF16), moe_w2[l // 2].astype(BF16),
                           final_norm_g, l == DEPTH - 1)
    return x
```

```python
import functools
import math

import numpy as np
import jax
import jax.numpy as jnp
from jax import lax
from jax.experimental import pallas as pl
from jax.experimental.pallas import tpu as pltpu

D_MODEL = 1024
DEPTH = 4
N_HEADS = 16
HEAD_DIM = D_MODEL // N_HEADS
N_A_LAYERS = DEPTH // 2
MOBA_BLOCK = 256
MOBA_TOPK = 3
N_BUCKETS = 32
MAX_DISTANCE = 1024
N_EXPERTS = 8
TOP_K_EXPERTS = 2
EPS = 1e-6
NEG = -1e30

LANES = 128
HEADS_PER_TILE = LANES // HEAD_DIM
ATT_BLOCK = 256
N_BIAS_TILES = 6
SB_DEAD_LOG = -105.0
VMEM_LIMIT = 56 * 1024 * 1024

F32 = jnp.float32
BF16 = jnp.bfloat16
HIGHEST = lax.Precision.HIGHEST
_NT = (((1,), (1,)), ((), ()))


def _params(n_axes):
    return pltpu.CompilerParams(dimension_semantics=("arbitrary",) * n_axes,
                                vmem_limit_bytes=VMEM_LIMIT)


def _norm_mod(x, g, sh, sc):
    ms = jnp.mean(x * x, axis=-1, keepdims=True)
    y = (x * lax.rsqrt(ms + EPS)) * g
    return y * (1.0 + sc) + sh


def _mod_body(c_ref, w_ref, b_ref, o_ref):
    c = c_ref[...]
    s = c * jax.nn.sigmoid(c)
    o_ref[0] = jnp.dot(s, w_ref[0], preferred_element_type=F32, precision=HIGHEST) + b_ref[0]


def _modulation(c, w, b):
    L, D, M = w.shape
    B = c.shape[0]
    tn = 1024
    return pl.pallas_call(
        _mod_body,
        grid=(L, M // tn),
        in_specs=[pl.BlockSpec((B, D), lambda l, j: (0, 0)),
                  pl.BlockSpec((1, D, tn), lambda l, j: (l, 0, j)),
                  pl.BlockSpec((1, 1, tn), lambda l, j: (l, 0, j))],
        out_specs=pl.BlockSpec((1, B, tn), lambda l, j: (l, 0, j)),
        out_shape=jax.ShapeDtypeStruct((L, B, M), F32),
        compiler_params=_params(2),
        name="adaln_mod",
    )(c, w, b.reshape(L, 1, M))


def _nm_mm_body(x_ref, g_ref, sh_ref, sc_ref, w_ref, o_ref, h_sc):
    @pl.when(pl.program_id(2) == 0)
    def _():
        h_sc[...] = _norm_mod(x_ref[0], g_ref[...], sh_ref[0], sc_ref[0]).astype(BF16)

    o_ref[0] = jnp.dot(h_sc[...], w_ref[...], preferred_element_type=F32).astype(o_ref.dtype)


def _norm_mod_matmul(x, g, sh, sc, w):
    B, S, D = x.shape
    N = w.shape[1]
    tm, tn = 1024, 1024
    return pl.pallas_call(
        _nm_mm_body,
        grid=(B, S // tm, N // tn),
        in_specs=[pl.BlockSpec((1, tm, D), lambda b, i, j: (b, i, 0)),
                  pl.BlockSpec((1, D), lambda b, i, j: (0, 0)),
                  pl.BlockSpec((1, 1, D), lambda b, i, j: (b, 0, 0)),
                  pl.BlockSpec((1, 1, D), lambda b, i, j: (b, 0, 0)),
                  pl.BlockSpec((D, tn), lambda b, i, j: (0, j))],
        out_specs=pl.BlockSpec((1, tm, tn), lambda b, i, j: (b, i, j)),
        out_shape=jax.ShapeDtypeStruct((B, S, N), BF16),
        scratch_shapes=[pltpu.VMEM((tm, D), BF16)],
        compiler_params=_params(3),
        name="norm_mod_proj",
    )(x, g.reshape(1, D), sh, sc, w)


def _nm_mm_t_body(x_ref, g_ref, sh_ref, sc_ref, wt_ref, o_ref, h_sc):
    @pl.when(pl.program_id(2) == 0)
    def _():
        h_sc[...] = _norm_mod(x_ref[0], g_ref[...], sh_ref[0], sc_ref[0]).astype(BF16)

    res = lax.dot_general(wt_ref[...], h_sc[...], _NT, preferred_element_type=F32)
    for r in range(o_ref.shape[1]):
        o_ref[0, r] = res[:, r * ATT_BLOCK:(r + 1) * ATT_BLOCK].astype(o_ref.dtype)


def _norm_mod_matmul_t(x, g, sh, sc, wt):
    B, S, D = x.shape
    N = wt.shape[0]
    tm, tn = 1024, 1024
    T = ATT_BLOCK
    return pl.pallas_call(
        _nm_mm_t_body,
        grid=(B, S // tm, N // tn),
        in_specs=[pl.BlockSpec((1, tm, D), lambda b, i, j: (b, i, 0)),
                  pl.BlockSpec((1, D), lambda b, i, j: (0, 0)),
                  pl.BlockSpec((1, 1, D), lambda b, i, j: (b, 0, 0)),
                  pl.BlockSpec((1, 1, D), lambda b, i, j: (b, 0, 0)),
                  pl.BlockSpec((tn, D), lambda b, i, j: (j, 0))],
        out_specs=pl.BlockSpec((1, tm // T, tn, T), lambda b, i, j: (b, i, j, 0)),
        out_shape=jax.ShapeDtypeStruct((B, S // T, N, T), BF16),
        scratch_shapes=[pltpu.VMEM((tm, D), BF16)],
        compiler_params=_params(3),
        name="norm_mod_proj_t",
    )(x, g.reshape(1, D), sh, sc, wt)


def _mm_res_body(a_ref, w_ref, r_ref, gate_ref, o_ref):
    acc = jnp.dot(a_ref[0], w_ref[...], preferred_element_type=F32)
    o_ref[0] = r_ref[0] + gate_ref[0] * acc


def _matmul_gated_residual(a, w, res, gate):
    B, S, K = a.shape
    N = w.shape[1]
    tm, tn = 1024, 1024
    return pl.pallas_call(
        _mm_res_body,
        grid=(B, S // tm, N // tn),
        in_specs=[pl.BlockSpec((1, tm, K), lambda b, i, j: (b, i, 0)),
                  pl.BlockSpec((K, tn), lambda b, i, j: (0, j)),
                  pl.BlockSpec((1, tm, tn), lambda b, i, j: (b, i, j)),
                  pl.BlockSpec((1, 1, tn), lambda b, i, j: (b, 0, j))],
        out_specs=pl.BlockSpec((1, tm, tn), lambda b, i, j: (b, i, j)),
        out_shape=jax.ShapeDtypeStruct((B, S, N), F32),
        compiler_params=_params(3),
        name="out_proj_residual",
    )(a, w, res, gate)


def _pv_t(vt, ps, head_row):
    zeros = jnp.zeros_like(vt)
    vcat = jnp.concatenate(
        [jnp.where((head_row >= hh * HEAD_DIM) & (head_row < (hh + 1) * HEAD_DIM), vt, zeros)
         for hh in range(HEADS_PER_TILE)], axis=1)
    return jnp.dot(vcat, jnp.concatenate(ps, axis=0), preferred_element_type=F32)


def _per_head_rows(head_row, vals):
    assert HEADS_PER_TILE == 2
    return jnp.where(head_row < HEAD_DIM, vals[0], vals[1])


def _head_queries(q, lane):
    out = []
    for hh in range(HEADS_PER_TILE):
        in_head = (lane >= hh * HEAD_DIM) & (lane < (hh + 1) * HEAD_DIM)
        out.append(jnp.where(in_head, q, jnp.zeros_like(q)))
    return out


def _rel_bucket(dist):
    max_exact = N_BUCKETS // 2
    d = np.maximum(dist, 1).astype(np.float32)
    ratio = np.log(d / np.float32(max_exact)) / np.float32(math.log(MAX_DISTANCE / max_exact))
    large = max_exact + (ratio * np.float32(N_BUCKETS - max_exact)).astype(np.int32)
    large = np.minimum(large, N_BUCKETS - 1)
    return np.where(dist < max_exact, dist, large)


def _bias_tiles(rel_bias):
    T = ATT_BLOCK
    W = 2 * T
    H = rel_bias.shape[1]
    k = np.arange(W)
    lag = np.where(k <= T, k, k - W)
    dist = np.maximum(np.arange(N_BIAS_TILES)[:, None] * T + lag[None, :], 0)
    bucket = _rel_bucket(dist)
    assert (bucket[N_BIAS_TILES - 1] == N_BUCKETS - 1).all()
    onehot = np.eye(N_BUCKETS, dtype=np.float32)[bucket.reshape(-1)]
    vec = jnp.dot(jnp.asarray(onehot), rel_bias, precision=HIGHEST)
    a = vec.T.reshape(H, N_BIAS_TILES, W)
    tiled = jnp.tile(a, (1, 1, T))
    return tiled[:, :, :T * (W - 1)].reshape(H, N_BIAS_TILES, T, W - 1)[:, :, :, :T]


def _moba_body(q_ref, k_ref, vt_ref, bias_ref, o_ref, kmean_sc, sel_sc, *, nb):
    c = pl.program_id(2)
    T = ATT_BLOCK

    @pl.when(c == 0)
    def _():
        for n in range(nb):
            kb = k_ref[0, n * T:(n + 1) * T, :].astype(F32)
            kmean_sc[n:n + 1, :] = jnp.sum(kb, axis=0, keepdims=True) * (1.0 / T)

    lane = lax.broadcasted_iota(jnp.int32, (T, LANES), 1)
    blk = lax.broadcasted_iota(jnp.int32, (nb, T), 0)
    key_i = lax.broadcasted_iota(jnp.int32, (T, T), 0)
    qry_i = lax.broadcasted_iota(jnp.int32, (T, T), 1)
    causal = key_i <= qry_i
    head_row = lax.broadcasted_iota(jnp.int32, (LANES, T), 0)
    kmean = kmean_sc[...]
    scale = HEAD_DIM ** -0.5

    qs = []
    for hh, qh in enumerate(_head_queries(q_ref[0], lane)):
        gate = lax.dot_general(kmean, qh.astype(F32), _NT,
                               preferred_element_type=F32, precision=HIGHEST)
        gate = jnp.where(blk < c, gate, NEG)
        rank = jnp.zeros((nb, T), jnp.int32)
        for m in range(nb):
            gm = gate[m:m + 1, :]
            beats = (gm > gate) | ((gm == gate) & (blk > m))
            rank = rank + beats.astype(jnp.int32)
        sel_sc[hh] = ((rank < MOBA_TOPK) & (blk < c)).astype(F32)
        qs.append((qh.astype(F32) * scale).astype(BF16))

    c0 = pl.multiple_of(c * T, T)
    kb0 = k_ref[0, pl.ds(c0, T), :]
    stats, ps = [], []
    for hh in range(HEADS_PER_TILE):
        s = lax.dot_general(kb0, qs[hh], _NT, preferred_element_type=F32) + bias_ref[hh, 0]
        s = jnp.where(causal, s, NEG)
        m = jnp.max(s, axis=0, keepdims=True)
        p = jnp.exp(s - m)
        stats += [m, jnp.sum(p, axis=0, keepdims=True)]
        ps.append(p.astype(BF16))
    acc0 = _pv_t(vt_ref[0, c], ps, head_row)

    def body(n, carry):
        n0 = pl.multiple_of(n * T, T)
        kb = k_ref[0, pl.ds(n0, T), :]
        dl = jnp.minimum(c - n, N_BIAS_TILES - 1)
        new_stats, alphas, ps = [], [], []
        for hh in range(HEADS_PER_TILE):
            m, l = carry[2 * hh:2 * hh + 2]
            picked = sel_sc[hh, pl.ds(n, 1), :] > 0.5
            s = lax.dot_general(kb, qs[hh], _NT, preferred_element_type=F32)
            s = s + bias_ref[hh, pl.ds(dl, 1)][0]
            s = jnp.where(picked, s, NEG)
            m_new = jnp.maximum(m, jnp.max(s, axis=0, keepdims=True))
            alpha = jnp.exp(m - m_new)
            p = jnp.exp(s - m_new)
            new_stats += [m_new, alpha * l + jnp.sum(p, axis=0, keepdims=True)]
            alphas.append(alpha)
            ps.append(p.astype(BF16))
        acc = _per_head_rows(head_row, alphas) * carry[-1] + _pv_t(vt_ref[0, n], ps, head_row)
        return tuple(new_stats) + (acc,)

    fin = lax.fori_loop(0, c, body, tuple(stats) + (acc0,))
    o = fin[-1] / _per_head_rows(head_row, [fin[1], fin[3]])
    o_ref[0] = o.T.astype(o_ref.dtype)


def _moba_attention(qk, vt, bias_tiles):
    B, S, D2 = qk.shape
    D = D2 // 2
    T = ATT_BLOCK
    nb = S // T
    n_hp = D // LANES
    return pl.pallas_call(
        functools.partial(_moba_body, nb=nb),
        grid=(B, n_hp, nb),
        in_specs=[pl.BlockSpec((1, T, LANES), lambda b, h, i: (b, i, h)),
                  pl.BlockSpec((1, S, LANES), lambda b, h, i: (b, 0, n_hp + h)),
                  pl.BlockSpec((1, nb, LANES, T), lambda b, h, i: (b, 0, h, 0)),
                  pl.BlockSpec((HEADS_PER_TILE, N_BIAS_TILES, T, T), lambda b, h, i: (h, 0, 0, 0))],
        out_specs=pl.BlockSpec((1, T, LANES), lambda b, h, i: (b, i, h)),
        out_shape=jax.ShapeDtypeStruct((B, S, D), BF16),
        scratch_shapes=[pltpu.VMEM((nb, LANES), F32), pltpu.VMEM((HEADS_PER_TILE, nb, T), F32)],
        compiler_params=_params(3),
        name="moba_attention",
    )(qk, qk, vt, bias_tiles)


def _sb_body(q_ref, k_ref, vt_ref, o_ref):
    c = pl.program_id(2)
    T = ATT_BLOCK
    lane = lax.broadcasted_iota(jnp.int32, (T, LANES), 1)
    key_i = lax.broadcasted_iota(jnp.int32, (T, T), 0)
    qry_i = lax.broadcasted_iota(jnp.int32, (T, T), 1)
    strict = key_i < qry_i
    later = (qry_i > key_i).astype(BF16)
    later2 = jnp.concatenate([later, later], axis=1)
    head_row = lax.broadcasted_iota(jnp.int32, (LANES, T), 0)
    scale = HEAD_DIM ** -0.5
    qs = [(qh.astype(F32) * scale).astype(BF16) for qh in _head_queries(q_ref[0], lane)]

    def block(n, state, diag):
        n0 = pl.multiple_of(n * T, T)
        kb = k_ref[0, pl.ds(n0, T), :]
        rs, ps = [], []
        for hh in range(HEADS_PER_TILE):
            r = state[hh]
            z = lax.dot_general(kb, qs[hh], _NT, preferred_element_type=F32)
            sp = jnp.log(1.0 + jnp.exp(-jnp.abs(z)))
            lb = jnp.minimum(z, 0.0) - sp
            l1 = lb - z
            if diag:
                l1 = jnp.where(strict, l1, 0.0)
            hi = l1.astype(BF16)
            lo = (l1 - hi.astype(F32)).astype(BF16)
            tail = jnp.dot(later2, jnp.concatenate([hi, lo], axis=0), preferred_element_type=F32)
            a = jnp.exp(lb + tail + r)
            if diag:
                a = jnp.where(strict, a, 0.0)
            ps.append(a.astype(BF16))
            rs.append(r + tail[0:1, :] + l1[0:1, :])
        return tuple(rs) + (state[-1] + _pv_t(vt_ref[0, n], ps, head_row),)

    zero_r = jnp.zeros((1, T), F32)
    state = block(c, (zero_r,) * HEADS_PER_TILE + (jnp.zeros((LANES, T), F32),), True)

    def cond(carry):
        live = jnp.max(jnp.maximum(carry[1], carry[2])) > SB_DEAD_LOG
        return jnp.logical_and(carry[0] < c, live)

    def body(carry):
        i = carry[0]
        return (i + 1,) + block(c - 1 - i, carry[1:], False)

    fin = lax.while_loop(cond, body, (jnp.int32(0),) + state)
    o_ref[0] = fin[-1].T.astype(o_ref.dtype)


def _sb_attention(q, k, vt):
    B, S, D = q.shape
    T = ATT_BLOCK
    nb = S // T
    n_hp = D // LANES
    return pl.pallas_call(
        _sb_body,
        grid=(B, n_hp, nb),
        in_specs=[pl.BlockSpec((1, T, LANES), lambda b, h, i: (b, i, h)),
                  pl.BlockSpec((1, S, LANES), lambda b, h, i: (b, 0, h)),
                  pl.BlockSpec((1, nb, LANES, T), lambda b, h, i: (b, 0, h, 0))],
        out_specs=pl.BlockSpec((1, T, LANES), lambda b, h, i: (b, i, h)),
        out_shape=jax.ShapeDtypeStruct((B, S, D), BF16),
        compiler_params=_params(3),
        name="stick_breaking_attention",
    )(q, k, vt)


def _swiglu_chunk(xb, wg, wu, w2):
    g = jnp.dot(xb, wg, preferred_element_type=F32)
    u = jnp.dot(xb, wu, preferred_element_type=F32)
    a = (g * jax.nn.sigmoid(g) * u).astype(BF16)
    return jnp.dot(a, w2, preferred_element_type=F32)


def _dense_ffn_body(x_ref, g_ref, sh_ref, sc_ref, gate_ref, wg_ref, wu_ref, w2_ref, o_ref,
                    xb_sc, acc_sc, *, n_chunks):
    j = pl.program_id(1)

    @pl.when(j == 0)
    def _():
        xb_sc[...] = _norm_mod(x_ref[...], g_ref[...], sh_ref[0], sc_ref[0]).astype(BF16)

    y = _swiglu_chunk(xb_sc[...], wg_ref[0], wu_ref[0], w2_ref[0])

    @pl.when(j == 0)
    def _():
        acc_sc[...] = y

    @pl.when(j > 0)
    def _():
        acc_sc[...] += y

    @pl.when(j == n_chunks - 1)
    def _():
        o_ref[...] = x_ref[...] + gate_ref[0] * acc_sc[...]


def _dense_ffn(x, g, sh, sc, gate, w13, w2):
    B, S, D = x.shape
    Fh = w2.shape[1]
    tm, tf = 512, 1408
    n_chunks = Fh // tf
    per_b = S // tm
    xf = x.reshape(B * S, D)
    out = pl.pallas_call(
        functools.partial(_dense_ffn_body, n_chunks=n_chunks),
        grid=(B * S // tm, n_chunks),
        in_specs=[pl.BlockSpec((tm, D), lambda i, j: (i, 0)),
                  pl.BlockSpec((1, D), lambda i, j: (0, 0)),
                  pl.BlockSpec((1, 1, D), lambda i, j: (i // per_b, 0, 0)),
                  pl.BlockSpec((1, 1, D), lambda i, j: (i // per_b, 0, 0)),
                  pl.BlockSpec((1, 1, D), lambda i, j: (i // per_b, 0, 0)),
                  pl.BlockSpec((1, D, tf), lambda i, j: (0, 0, j)),
                  pl.BlockSpec((1, D, tf), lambda i, j: (0, 0, j + n_chunks)),
                  pl.BlockSpec((1, tf, D), lambda i, j: (0, j, 0))],
        out_specs=pl.BlockSpec((tm, D), lambda i, j: (i, 0)),
        out_shape=jax.ShapeDtypeStruct((B * S, D), F32),
        scratch_shapes=[pltpu.VMEM((tm, D), BF16), pltpu.VMEM((tm, D), F32)],
        compiler_params=_params(2),
        name="dense_swiglu",
    )(xf, g.reshape(1, D), sh, sc, gate, w13, w13, w2)
    return out.reshape(B, S, D)


def _router_body(x_ref, g_ref, sh_ref, sc_ref, rwt_ref, h_ref, ei_ref, w_ref):
    h = _norm_mod(x_ref[0], g_ref[...], sh_ref[0], sc_ref[0])
    h_ref[...] = h
    logits = lax.dot_general(rwt_ref[...], h, _NT, preferred_element_type=F32,
                             precision=HIGHEST)
    n_e = logits.shape[0]
    eio = lax.broadcasted_iota(jnp.int32, logits.shape, 0)
    m1 = jnp.max(logits, axis=0, keepdims=True)
    i1 = jnp.min(jnp.where(logits == m1, eio, n_e), axis=0, keepdims=True)
    rest = jnp.where(eio == i1, -jnp.inf, logits)
    m2 = jnp.max(rest, axis=0, keepdims=True)
    i2 = jnp.min(jnp.where(rest == m2, eio, n_e), axis=0, keepdims=True)
    t = jnp.exp(m2 - m1)
    den = 1.0 + t
    ei_ref[...] = jnp.concatenate([i1, i2], axis=0)
    w_ref[...] = jnp.concatenate([1.0 / den, t / den], axis=0)


def _router(x, g, sh, sc, router_w):
    B, S, D = x.shape
    E = router_w.shape[1]
    tm = 512
    per_b = S // tm
    return pl.pallas_call(
        _router_body,
        grid=(B, per_b),
        in_specs=[pl.BlockSpec((1, tm, D), lambda b, i: (b, i, 0)),
                  pl.BlockSpec((1, D), lambda b, i: (0, 0)),
                  pl.BlockSpec((1, 1, D), lambda b, i: (b, 0, 0)),
                  pl.BlockSpec((1, 1, D), lambda b, i: (b, 0, 0)),
                  pl.BlockSpec((E, D), lambda b, i: (0, 0))],
        out_specs=[pl.BlockSpec((tm, D), lambda b, i: (b * per_b + i, 0)),
                   pl.BlockSpec((TOP_K_EXPERTS, tm), lambda b, i: (0, b * per_b + i)),
                   pl.BlockSpec((TOP_K_EXPERTS, tm), lambda b, i: (0, b * per_b + i))],
        out_shape=[jax.ShapeDtypeStruct((B * S, D), F32),
                   jax.ShapeDtypeStruct((TOP_K_EXPERTS, B * S), jnp.int32),
                   jax.ShapeDtypeStruct((TOP_K_EXPERTS, B * S), F32)],
        compiler_params=_params(2),
        name="moe_router",
    )(x, g.reshape(1, D), sh, sc, router_w.T)


def _route_plan(ei, tm, n_tiles):
    n_tok = ei.shape[1]
    e_flat = ei.reshape(-1)
    onehot = (e_flat[:, None] == jnp.arange(N_EXPERTS, dtype=jnp.int32)[None, :]).astype(jnp.int32)
    csum = jnp.cumsum(onehot, axis=0)
    rank = jnp.take_along_axis(csum, e_flat[:, None], axis=1)[:, 0] - 1
    counts = csum[-1]
    padded = ((counts + tm - 1) // tm) * tm
    ends = jnp.cumsum(padded)
    offs = ends - padded
    pos = (offs[e_flat] + rank).astype(jnp.int32)
    tok = jnp.tile(jnp.arange(n_tok, dtype=jnp.int32), TOP_K_EXPERTS)
    src = jnp.zeros((n_tiles * tm,), jnp.int32).at[pos].set(tok)
    starts = jnp.arange(n_tiles, dtype=jnp.int32) * tm
    t_exp = jnp.minimum(jnp.searchsorted(ends, starts, side="right"), N_EXPERTS - 1).astype(jnp.int32)
    t_valid = (starts < ends[-1]).astype(jnp.int32)
    return pos, src, t_exp, t_valid


def _gather_body(idx_ref, src_hbm, o_ref, sem, *, rows):
    base = pl.program_id(0) * rows

    def start(r, _):
        t = idx_ref[base + r]
        pltpu.make_async_copy(src_hbm.at[pl.ds(t, 1)], o_ref.at[pl.ds(r, 1)], sem).start()
        return 0

    lax.fori_loop(0, rows, start, 0)

    def wait(r, _):
        pltpu.make_async_copy(src_hbm.at[pl.ds(0, 1)], o_ref.at[pl.ds(r, 1)], sem).wait()
        return 0

    lax.fori_loop(0, rows, wait, 0)


def _gather_rows(src, idx):
    P = idx.shape[0]
    D = src.shape[1]
    rows = 512
    return pl.pallas_call(
        functools.partial(_gather_body, rows=rows),
        grid_spec=pltpu.PrefetchScalarGridSpec(
            num_scalar_prefetch=1,
            grid=(P // rows,),
            in_specs=[pl.BlockSpec(memory_space=pl.ANY)],
            out_specs=pl.BlockSpec((rows, D), lambda i, idx: (i, 0)),
            scratch_shapes=[pltpu.SemaphoreType.DMA]),
        out_shape=jax.ShapeDtypeStruct((P, D), src.dtype),
        compiler_params=_params(1),
        name="moe_gather",
    )(idx, src)


def _moe_ffn_body(te_ref, tv_ref, x_ref, wg_ref, wu_ref, w2_ref, o_ref, xb_sc, acc_sc, *, n_chunks):
    i = pl.program_id(0)
    j = pl.program_id(1)
    valid = tv_ref[i] > 0

    @pl.when(jnp.logical_and(valid, j == 0))
    def _():
        xb_sc[...] = x_ref[...].astype(BF16)

    @pl.when(valid)
    def _():
        y = _swiglu_chunk(xb_sc[...], wg_ref[0], wu_ref[0], w2_ref[0])

        @pl.when(j == 0)
        def _():
            acc_sc[...] = y

        @pl.when(j > 0)
        def _():
            acc_sc[...] += y

    @pl.when(j == n_chunks - 1)
    def _():
        @pl.when(valid)
        def _():
            o_ref[...] = acc_sc[...]

        @pl.when(jnp.logical_not(valid))
        def _():
            o_ref[...] = jnp.zeros_like(o_ref)


def _moe_ffn(xs, t_exp, t_valid, w13, w2, tm):
    P, D = xs.shape
    Fh = w2.shape[1]
    tf = 512
    n_chunks = Fh // tf
    return pl.pallas_call(
        functools.partial(_moe_ffn_body, n_chunks=n_chunks),
        grid_spec=pltpu.PrefetchScalarGridSpec(
            num_scalar_prefetch=2,
            grid=(P // tm, n_chunks),
            in_specs=[pl.BlockSpec((tm, D), lambda i, j, te, tv: (i, 0)),
                      pl.BlockSpec((1, D, tf), lambda i, j, te, tv: (te[i], 0, j * tv[i])),
                      pl.BlockSpec((1, D, tf), lambda i, j, te, tv: (te[i], 0, j * tv[i] + n_chunks)),
                      pl.BlockSpec((1, tf, D), lambda i, j, te, tv: (te[i], j * tv[i], 0))],
            out_specs=pl.BlockSpec((tm, D), lambda i, j, te, tv: (i, 0)),
            scratch_shapes=[pltpu.VMEM((tm, D), BF16), pltpu.VMEM((tm, D), F32)]),
        out_shape=jax.ShapeDtypeStruct((P, D), F32),
        compiler_params=_params(2),
        name="moe_expert_swiglu",
    )(t_exp, t_valid, xs, w13, w13, w2)


def _combine_body(pos_ref, y_hbm, x_ref, w_ref, gate_ref, fg_ref, o_ref, ybuf, sem, *,
                  rows, n_tok, final_norm):
    base = pl.program_id(0) * rows

    def start(r, _):
        for k in range(TOP_K_EXPERTS):
            p = pos_ref[k * n_tok + base + r]
            pltpu.make_async_copy(y_hbm.at[pl.ds(p, 1)], ybuf.at[k, pl.ds(r, 1)], sem).start()
        return 0

    lax.fori_loop(0, rows, start, 0)

    def wait(r, _):
        for k in range(TOP_K_EXPERTS):
            pltpu.make_async_copy(y_hbm.at[pl.ds(0, 1)], ybuf.at[k, pl.ds(r, 1)], sem).wait()
        return 0

    lax.fori_loop(0, rows, wait, 0)

    w = w_ref[...]
    y = w[:, 0:1] * ybuf[0] + w[:, 1:2] * ybuf[1]
    out = x_ref[...] + gate_ref[0] * y
    if final_norm:
        ms = jnp.mean(out * out, axis=-1, keepdims=True)
        out = (out * lax.rsqrt(ms + EPS)) * fg_ref[...]
    o_ref[...] = out


def _moe_combine(x, ys, pos, wts, gate, final_g, final_norm):
    B, S, D = x.shape
    n_tok = B * S
    rows = 256
    per_b = S // rows
    out = pl.pallas_call(
        functools.partial(_combine_body, rows=rows, n_tok=n_tok, final_norm=final_norm),
        grid_spec=pltpu.PrefetchScalarGridSpec(
            num_scalar_prefetch=1,
            grid=(n_tok // rows,),
            in_specs=[pl.BlockSpec(memory_space=pl.ANY),
                      pl.BlockSpec((rows, D), lambda i, pos: (i, 0)),
                      pl.BlockSpec((rows, TOP_K_EXPERTS), lambda i, pos: (i, 0)),
                      pl.BlockSpec((1, 1, D), lambda i, pos: (i // per_b, 0, 0)),
                      pl.BlockSpec((1, D), lambda i, pos: (0, 0))],
            out_specs=pl.BlockSpec((rows, D), lambda i, pos: (i, 0)),
            scratch_shapes=[pltpu.VMEM((TOP_K_EXPERTS, rows, D), F32), pltpu.SemaphoreType.DMA]),
        out_shape=jax.ShapeDtypeStruct((n_tok, D), F32),
        compiler_params=_params(1),
        name="moe_combine",
    )(pos, ys, x.reshape(n_tok, D), wts, gate, final_g.reshape(1, D))
    return out.reshape(B, S, D)


def _moe_layer(x, g, sh, sc, gate, router_w, w13, w2, final_g, final_norm):
    B, S, D = x.shape
    n_tok = B * S
    tm = 1024
    n_tiles = TOP_K_EXPERTS * n_tok // tm + N_EXPERTS
    h, ei, wts = _router(x, g, sh, sc, router_w)
    pos, src, t_exp, t_valid = _route_plan(ei, tm, n_tiles)
    xs = _gather_rows(h, src)
    ys = _moe_ffn(xs, t_exp, t_valid, w13, w2, tm)
    return _moe_combine(x, ys, pos, wts.T, gate, final_g, final_norm)


def kernel(x, c, rel_bias, mod_w, mod_b, norm_mix_g, norm_ffn_g, a_wqkv, a_wo, kv_norm_g, kv_mod_w,
           kv_mod_b, b_wkv, b_wq, b_wo, ffn_w13, ffn_w2, router_w, moe_w13, moe_w2, final_norm_g):
    D = D_MODEL
    mods = _modulation(c, mod_w, mod_b)
    kv_mod = _modulation(c, kv_mod_w[None], kv_mod_b[None])[0]
    bias_tiles = _bias_tiles(rel_bias)
    k_sh = vt_sh = None
    for l in range(DEPTH):
        sh_a, sc_a, g_a, sh_f, sc_f, g_f = [mods[l][:, None, i * D:(i + 1) * D] for i in range(6)]
        if l < N_A_LAYERS:
            w = a_wqkv[l].astype(BF16)
            qk = _norm_mod_matmul(x, norm_mix_g[l], sh_a, sc_a, w[:, :2 * D])
            vt = _norm_mod_matmul_t(x, norm_mix_g[l], sh_a, sc_a, w[:, 2 * D:].T)
            attn = _moba_attention(qk, vt, bias_tiles)
            x = _matmul_gated_residual(attn, a_wo[l].astype(BF16), x, g_a)
        else:
            j = l - N_A_LAYERS
            if j == 0:
                w = b_wkv.astype(BF16)
                kv_sh, kv_sc = kv_mod[:, None, :D], kv_mod[:, None, D:]
                k_sh = _norm_mod_matmul(x, kv_norm_g, kv_sh, kv_sc, w[:, :D])
                vt_sh = _norm_mod_matmul_t(x, kv_norm_g, kv_sh, kv_sc, w[:, D:].T)
            q = _norm_mod_matmul(x, norm_mix_g[l], sh_a, sc_a, b_wq[j].astype(BF16))
            attn = _sb_attention(q, k_sh, vt_sh)
            x = _matmul_gated_residual(attn, b_wo[j].astype(BF16), x, g_a)
        if l % 2 == 0:
            x = _dense_ffn(x, norm_ffn_g[l], sh_f, sc_f, g_f,
                           ffn_w13[l // 2][None].astype(BF16), ffn_w2[l // 2][None].astype(BF16))
        else:
            x = _moe_layer(x, norm_ffn_g[l], sh_f, sc_f, g_f, router_w[l // 2],
                           moe_w13[l // 2].astype(BF16), moe_w2[l // 2].astype(BF16),
                           final_norm_g, l == DEPTH - 1)
    return x
```

```python
import functools
import math

import numpy as np
import jax
import jax.numpy as jnp
from jax import lax
from jax.experimental import pallas as pl
from jax.experimental.pallas import tpu as pltpu

D_MODEL = 1024
DEPTH = 4
N_HEADS = 16
HEAD_DIM = D_MODEL // N_HEADS
N_A_LAYERS = DEPTH // 2
MOBA_BLOCK = 256
MOBA_TOPK = 3
N_BUCKETS = 32
MAX_DISTANCE = 1024
N_EXPERTS = 8
TOP_K_EXPERTS = 2
EPS = 1e-6
NEG = -1e30

LANES = 128
HEADS_PER_TILE = LANES // HEAD_DIM
ATT_BLOCK = 256
N_BIAS_TILES = 6
SB_DEAD_LOG = -105.0
VMEM_LIMIT = 56 * 1024 * 1024

F32 = jnp.float32
BF16 = jnp.bfloat16
HIGHEST = lax.Precision.HIGHEST
_NT = (((1,), (1,)), ((), ()))


def _params(n_axes):
    return pltpu.CompilerParams(dimension_semantics=("arbitrary",) * n_axes,
                                vmem_limit_bytes=VMEM_LIMIT)


def _norm_mod(x, g, sh, sc):
    ms = jnp.mean(x * x, axis=-1, keepdims=True)
    y = (x * lax.rsqrt(ms + EPS)) * g
    return y * (1.0 + sc) + sh


def _mod_body(c_ref, w_ref, b_ref, o_ref):
    c = c_ref[...]
    s = c * jax.nn.sigmoid(c)
    o_ref[0] = jnp.dot(s, w_ref[0], preferred_element_type=F32, precision=HIGHEST) + b_ref[0]


def _modulation(c, w, b):
    L, D, M = w.shape
    B = c.shape[0]
    tn = 1024
    return pl.pallas_call(
        _mod_body,
        grid=(L, M // tn),
        in_specs=[pl.BlockSpec((B, D), lambda l, j: (0, 0)),
                  pl.BlockSpec((1, D, tn), lambda l, j: (l, 0, j)),
                  pl.BlockSpec((1, 1, tn), lambda l, j: (l, 0, j))],
        out_specs=pl.BlockSpec((1, B, tn), lambda l, j: (l, 0, j)),
        out_shape=jax.ShapeDtypeStruct((L, B, M), F32),
        compiler_params=_params(2),
        name="adaln_mod",
    )(c, w, b.reshape(L, 1, M))


def _nm_mm_body(x_ref, g_ref, sh_ref, sc_ref, w_ref, o_ref, h_sc):
    @pl.when(pl.program_id(2) == 0)
    def _():
        h_sc[...] = _norm_mod(x_ref[0], g_ref[...], sh_ref[0], sc_ref[0]).astype(BF16)

    o_ref[0] = jnp.dot(h_sc[...], w_ref[...], preferred_element_type=F32).astype(o_ref.dtype)


def _norm_mod_matmul(x, g, sh, sc, w):
    B, S, D = x.shape
    N = w.shape[1]
    tm, tn = 1024, 1024
    return pl.pallas_call(
        _nm_mm_body,
        grid=(B, S // tm, N // tn),
        in_specs=[pl.BlockSpec((1, tm, D), lambda b, i, j: (b, i, 0)),
                  pl.BlockSpec((1, D), lambda b, i, j: (0, 0)),
                  pl.BlockSpec((1, 1, D), lambda b, i, j: (b, 0, 0)),
                  pl.BlockSpec((1, 1, D), lambda b, i, j: (b, 0, 0)),
                  pl.BlockSpec((D, tn), lambda b, i, j: (0, j))],
        out_specs=pl.BlockSpec((1, tm, tn), lambda b, i, j: (b, i, j)),
        out_shape=jax.ShapeDtypeStruct((B, S, N), BF16),
        scratch_shapes=[pltpu.VMEM((tm, D), BF16)],
        compiler_params=_params(3),
        name="norm_mod_proj",
    )(x, g.reshape(1, D), sh, sc, w)


def _nm_mm_t_body(x_ref, g_ref, sh_ref, sc_ref, wt_ref, o_ref, h_sc):
    @pl.when(pl.program_id(2) == 0)
    def _():
        h_sc[...] = _norm_mod(x_ref[0], g_ref[...], sh_ref[0], sc_ref[0]).astype(BF16)

    res = lax.dot_general(wt_ref[...], h_sc[...], _NT, preferred_element_type=F32)
    for r in range(o_ref.shape[1]):
        o_ref[0, r] = res[:, r * ATT_BLOCK:(r + 1) * ATT_BLOCK].astype(o_ref.dtype)


def _norm_mod_matmul_t(x, g, sh, sc, wt):
    B, S, D = x.shape
    N = wt.shape[0]
    tm, tn = 1024, 1024
    T = ATT_BLOCK
    return pl.pallas_call(
        _nm_mm_t_body,
        grid=(B, S // tm, N // tn),
        in_specs=[pl.BlockSpec((1, tm, D), lambda b, i, j: (b, i, 0)),
                  pl.BlockSpec((1, D), lambda b, i, j: (0, 0)),
                  pl.BlockSpec((1, 1, D), lambda b, i, j: (b, 0, 0)),
                  pl.BlockSpec((1, 1, D), lambda b, i, j: (b, 0, 0)),
                  pl.BlockSpec((tn, D), lambda b, i, j: (j, 0))],
        out_specs=pl.BlockSpec((1, tm // T, tn, T), lambda b, i, j: (b, i, j, 0)),
        out_shape=jax.ShapeDtypeStruct((B, S // T, N, T), BF16),
        scratch_shapes=[pltpu.VMEM((tm, D), BF16)],
        compiler_params=_params(3),
        name="norm_mod_proj_t",
    )(x, g.reshape(1, D), sh, sc, wt)


def _mm_res_body(a_ref, w_ref, r_ref, gate_ref, o_ref):
    acc = jnp.dot(a_ref[0], w_ref[...], preferred_element_type=F32)
    o_ref[0] = r_ref[0] + gate_ref[0] * acc


def _matmul_gated_residual(a, w, res, gate):
    B, S, K = a.shape
    N = w.shape[1]
    tm, tn = 1024, 1024
    return pl.pallas_call(
        _mm_res_body,
        grid=(B, S // tm, N // tn),
        in_specs=[pl.BlockSpec((1, tm, K), lambda b, i, j: (b, i, 0)),
                  pl.BlockSpec((K, tn), lambda b, i, j: (0, j)),
                  pl.BlockSpec((1, tm, tn), lambda b, i, j: (b, i, j)),
                  pl.BlockSpec((1, 1, tn), lambda b, i, j: (b, 0, j))],
        out_specs=pl.BlockSpec((1, tm, tn), lambda b, i, j: (b, i, j)),
        out_shape=jax.ShapeDtypeStruct((B, S, N), F32),
        compiler_params=_params(3),
        name="out_proj_residual",
    )(a, w, res, gate)


def _pv_t(vt, ps, head_row):
    zeros = jnp.zeros_like(vt)
    vcat = jnp.concatenate(
        [jnp.where((head_row >= hh * HEAD_DIM) & (head_row < (hh + 1) * HEAD_DIM), vt, zeros)
         for hh in range(HEADS_PER_TILE)], axis=1)
    return jnp.dot(vcat, jnp.concatenate(ps, axis=0), preferred_element_type=F32)


def _per_head_rows(head_row, vals):
    assert HEADS_PER_TILE == 2
    return jnp.where(head_row < HEAD_DIM, vals[0], vals[1])


def _head_queries(q, lane):
    out = []
    for hh in range(HEADS_PER_TILE):
        in_head = (lane >= hh * HEAD_DIM) & (lane < (hh + 1) * HEAD_DIM)
        out.append(jnp.where(in_head, q, jnp.zeros_like(q)))
    return out


def _rel_bucket(dist):
    max_exact = N_BUCKETS // 2
    d = np.maximum(dist, 1).astype(np.float32)
    ratio = np.log(d / np.float32(max_exact)) / np.float32(math.log(MAX_DISTANCE / max_exact))
    large = max_exact + (ratio * np.float32(N_BUCKETS - max_exact)).astype(np.int32)
    large = np.minimum(large, N_BUCKETS - 1)
    return np.where(dist < max_exact, dist, large)


def _bias_tiles(rel_bias):
    T = ATT_BLOCK
    W = 2 * T
    H = rel_bias.shape[1]
    k = np.arange(W)
    lag = np.where(k <= T, k, k - W)
    dist = np.maximum(np.arange(N_BIAS_TILES)[:, None] * T + lag[None, :], 0)
    bucket = _rel_bucket(dist)
    assert (bucket[N_BIAS_TILES - 1] == N_BUCKETS - 1).all()
    onehot = np.eye(N_BUCKETS, dtype=np.float32)[bucket.reshape(-1)]
    vec = jnp.dot(jnp.asarray(onehot), rel_bias, precision=HIGHEST)
    a = vec.T.reshape(H, N_BIAS_TILES, W)
    tiled = jnp.tile(a, (1, 1, T))
    return tiled[:, :, :T * (W - 1)].reshape(H, N_BIAS_TILES, T, W - 1)[:, :, :, :T]


def _moba_body(q_ref, k_ref, vt_ref, bias_ref, o_ref, kmean_sc, sel_sc, *, nb):
    c = pl.program_id(2)
    T = ATT_BLOCK

    @pl.when(c == 0)
    def _():
        for n in range(nb):
            kb = k_ref[0, n * T:(n + 1) * T, :].astype(F32)
            kmean_sc[n:n + 1, :] = jnp.sum(kb, axis=0, keepdims=True) * (1.0 / T)

    lane = lax.broadcasted_iota(jnp.int32, (T, LANES), 1)
    blk = lax.broadcasted_iota(jnp.int32, (nb, T), 0)
    key_i = lax.broadcasted_iota(jnp.int32, (T, T), 0)
    qry_i = lax.broadcasted_iota(jnp.int32, (T, T), 1)
    causal = key_i <= qry_i
    head_row = lax.broadcasted_iota(jnp.int32, (LANES, T), 0)
    kmean = kmean_sc[...]
    scale = HEAD_DIM ** -0.5

    qs = []
    for hh, qh in enumerate(_head_queries(q_ref[0], lane)):
        gate = lax.dot_general(kmean, qh.astype(F32), _NT,
                               preferred_element_type=F32, precision=HIGHEST)
        gate = jnp.where(blk < c, gate, NEG)
        rank = jnp.zeros((nb, T), jnp.int32)
        for m in range(nb):
            gm = gate[m:m + 1, :]
            beats = (gm > gate) | ((gm == gate) & (blk > m))
            rank = rank + beats.astype(jnp.int32)
        sel_sc[hh] = ((rank < MOBA_TOPK) & (blk < c)).astype(F32)
        qs.append((qh.astype(F32) * scale).astype(BF16))

    c0 = pl.multiple_of(c * T, T)
    kb0 = k_ref[0, pl.ds(c0, T), :]
    stats, ps = [], []
    for hh in range(HEADS_PER_TILE):
        s = lax.dot_general(kb0, qs[hh], _NT, preferred_element_type=F32) + bias_ref[hh, 0]
        s = jnp.where(causal, s, NEG)
        m = jnp.max(s, axis=0, keepdims=True)
        p = jnp.exp(s - m)
        stats += [m, jnp.sum(p, axis=0, keepdims=True)]
        ps.append(p.astype(BF16))
    acc0 = _pv_t(vt_ref[0, c], ps, head_row)

    def body(n, carry):
        n0 = pl.multiple_of(n * T, T)
        kb = k_ref[0, pl.ds(n0, T), :]
        dl = jnp.minimum(c - n, N_BIAS_TILES - 1)
        new_stats, alphas, ps = [], [], []
        for hh in range(HEADS_PER_TILE):
            m, l = carry[2 * hh:2 * hh + 2]
            picked = sel_sc[hh, pl.ds(n, 1), :] > 0.5
            s = lax.dot_general(kb, qs[hh], _NT, preferred_element_type=F32)
            s = s + bias_ref[hh, pl.ds(dl, 1)][0]
            s = jnp.where(picked, s, NEG)
            m_new = jnp.maximum(m, jnp.max(s, axis=0, keepdims=True))
            alpha = jnp.exp(m - m_new)
            p = jnp.exp(s - m_new)
            new_stats += [m_new, alpha * l + jnp.sum(p, axis=0, keepdims=True)]
            alphas.append(alpha)
            ps.append(p.astype(BF16))
        acc = _per_head_rows(head_row, alphas) * carry[-1] + _pv_t(vt_ref[0, n], ps, head_row)
        return tuple(new_stats) + (acc,)

    def pair(i, carry):
        return body(2 * i + 1, body(2 * i, carry))

    fin = lax.fori_loop(0, lax.shift_right_logical(c, 1), pair, tuple(stats) + (acc0,))
    fin = lax.cond((c & 1) == 1, lambda cr: body(c - 1, cr), lambda cr: cr, fin)
    o = fin[-1] / _per_head_rows(head_row, [fin[1], fin[3]])
    o_ref[0] = o.T.astype(o_ref.dtype)


def _moba_attention(qk, vt, bias_tiles):
    B, S, D2 = qk.shape
    D = D2 // 2
    T = ATT_BLOCK
    nb = S // T
    n_hp = D // LANES
    return pl.pallas_call(
        functools.partial(_moba_body, nb=nb),
        grid=(B, n_hp, nb),
        in_specs=[pl.BlockSpec((1, T, LANES), lambda b, h, i: (b, i, h)),
                  pl.BlockSpec((1, S, LANES), lambda b, h, i: (b, 0, n_hp + h)),
                  pl.BlockSpec((1, nb, LANES, T), lambda b, h, i: (b, 0, h, 0)),
                  pl.BlockSpec((HEADS_PER_TILE, N_BIAS_TILES, T, T), lambda b, h, i: (h, 0, 0, 0))],
        out_specs=pl.BlockSpec((1, T, LANES), lambda b, h, i: (b, i, h)),
        out_shape=jax.ShapeDtypeStruct((B, S, D), BF16),
        scratch_shapes=[pltpu.VMEM((nb, LANES), F32), pltpu.VMEM((HEADS_PER_TILE, nb, T), F32)],
        compiler_params=_params(3),
        name="moba_attention",
    )(qk, qk, vt, bias_tiles)


def _sb_body(q_ref, k_ref, vt_ref, o_ref):
    c = pl.program_id(2)
    T = ATT_BLOCK
    lane = lax.broadcasted_iota(jnp.int32, (T, LANES), 1)
    key_i = lax.broadcasted_iota(jnp.int32, (T, T), 0)
    qry_i = lax.broadcasted_iota(jnp.int32, (T, T), 1)
    strict = key_i < qry_i
    later = (qry_i > key_i).astype(BF16)
    later2 = jnp.concatenate([later, later], axis=1)
    head_row = lax.broadcasted_iota(jnp.int32, (LANES, T), 0)
    scale = HEAD_DIM ** -0.5
    qs = [(qh.astype(F32) * scale).astype(BF16) for qh in _head_queries(q_ref[0], lane)]

    def block(n, state, diag, keep=None):
        n0 = pl.multiple_of(n * T, T)
        kb = k_ref[0, pl.ds(n0, T), :]
        vt = vt_ref[0, n]
        if keep is not None:
            vt = jnp.where(keep, vt, jnp.zeros_like(vt))
        rs, ps = [], []
        for hh in range(HEADS_PER_TILE):
            r = state[hh]
            z = lax.dot_general(kb, qs[hh], _NT, preferred_element_type=F32)
            sp = jnp.log(1.0 + jnp.exp(-jnp.abs(z)))
            lb = jnp.minimum(z, 0.0) - sp
            l1 = lb - z
            if diag:
                l1 = jnp.where(strict, l1, 0.0)
            hi = l1.astype(BF16)
            lo = (l1 - hi.astype(F32)).astype(BF16)
            tail = jnp.dot(later2, jnp.concatenate([hi, lo], axis=0), preferred_element_type=F32)
            a = jnp.exp(lb + tail + r)
            if diag:
                a = jnp.where(strict, a, 0.0)
            ps.append(a.astype(BF16))
            step = tail[0:1, :] + l1[0:1, :]
            if keep is not None:
                step = jnp.where(keep, step, 0.0)
            rs.append(r + step)
        return tuple(rs) + (state[-1] + _pv_t(vt, ps, head_row),)

    zero_r = jnp.zeros((1, T), F32)
    state = block(c, (zero_r,) * HEADS_PER_TILE + (jnp.zeros((LANES, T), F32),), True)
    state = block(jnp.maximum(c - 1, 0), state, False, keep=c > 0)

    def cond(carry):
        live = jnp.max(jnp.maximum(carry[1], carry[2])) > SB_DEAD_LOG
        return jnp.logical_and(carry[0] < c, live)

    def body(carry):
        i = carry[0]
        return (i + 1,) + block(c - 1 - i, carry[1:], False)

    fin = lax.while_loop(cond, body, (jnp.int32(1),) + state)
    o_ref[0] = fin[-1].T.astype(o_ref.dtype)


def _sb_attention(q, k, vt):
    B, S, D = q.shape
    T = ATT_BLOCK
    nb = S // T
    n_hp = D // LANES
    return pl.pallas_call(
        _sb_body,
        grid=(B, n_hp, nb),
        in_specs=[pl.BlockSpec((1, T, LANES), lambda b, h, i: (b, i, h)),
                  pl.BlockSpec((1, S, LANES), lambda b, h, i: (b, 0, h)),
                  pl.BlockSpec((1, nb, LANES, T), lambda b, h, i: (b, 0, h, 0))],
        out_specs=pl.BlockSpec((1, T, LANES), lambda b, h, i: (b, i, h)),
        out_shape=jax.ShapeDtypeStruct((B, S, D), BF16),
        compiler_params=_params(3),
        name="stick_breaking_attention",
    )(q, k, vt)


def _swiglu_chunk(xb, wg, wu, w2):
    g = jnp.dot(xb, wg, preferred_element_type=F32)
    u = jnp.dot(xb, wu, preferred_element_type=F32)
    a = (g * jax.nn.sigmoid(g) * u).astype(BF16)
    return jnp.dot(a, w2, preferred_element_type=F32)


def _dense_ffn_body(x_ref, g_ref, sh_ref, sc_ref, gate_ref, wg_ref, wu_ref, w2_ref, o_ref,
                    xb_sc, acc_sc, *, n_chunks):
    j = pl.program_id(1)

    @pl.when(j == 0)
    def _():
        xb_sc[...] = _norm_mod(x_ref[...], g_ref[...], sh_ref[0], sc_ref[0]).astype(BF16)

    y = _swiglu_chunk(xb_sc[...], wg_ref[0], wu_ref[0], w2_ref[0])

    @pl.when(j == 0)
    def _():
        acc_sc[...] = y

    @pl.when(j > 0)
    def _():
        acc_sc[...] += y

    @pl.when(j == n_chunks - 1)
    def _():
        o_ref[...] = x_ref[...] + gate_ref[0] * acc_sc[...]


def _dense_ffn(x, g, sh, sc, gate, w13, w2):
    B, S, D = x.shape
    Fh = w2.shape[1]
    tm, tf = 512, 1408
    n_chunks = Fh // tf
    per_b = S // tm
    xf = x.reshape(B * S, D)
    out = pl.pallas_call(
        functools.partial(_dense_ffn_body, n_chunks=n_chunks),
        grid=(B * S // tm, n_chunks),
        in_specs=[pl.BlockSpec((tm, D), lambda i, j: (i, 0)),
                  pl.BlockSpec((1, D), lambda i, j: (0, 0)),
                  pl.BlockSpec((1, 1, D), lambda i, j: (i // per_b, 0, 0)),
                  pl.BlockSpec((1, 1, D), lambda i, j: (i // per_b, 0, 0)),
                  pl.BlockSpec((1, 1, D), lambda i, j: (i // per_b, 0, 0)),
                  pl.BlockSpec((1, D, tf), lambda i, j: (0, 0, j)),
                  pl.BlockSpec((1, D, tf), lambda i, j: (0, 0, j + n_chunks)),
                  pl.BlockSpec((1, tf, D), lambda i, j: (0, j, 0))],
        out_specs=pl.BlockSpec((tm, D), lambda i, j: (i, 0)),
        out_shape=jax.ShapeDtypeStruct((B * S, D), F32),
        scratch_shapes=[pltpu.VMEM((tm, D), BF16), pltpu.VMEM((tm, D), F32)],
        compiler_params=_params(2),
        name="dense_swiglu",
    )(xf, g.reshape(1, D), sh, sc, gate, w13, w13, w2)
    return out.reshape(B, S, D)


def _router_body(x_ref, g_ref, sh_ref, sc_ref, rwt_ref, h_ref, ei_ref, w_ref):
    h = _norm_mod(x_ref[0], g_ref[...], sh_ref[0], sc_ref[0])
    h_ref[...] = h
    logits = lax.dot_general(rwt_ref[...], h, _NT, preferred_element_type=F32,
                             precision=HIGHEST)
    n_e = logits.shape[0]
    eio = lax.broadcasted_iota(jnp.int32, logits.shape, 0)
    m1 = jnp.max(logits, axis=0, keepdims=True)
    i1 = jnp.min(jnp.where(logits == m1, eio, n_e), axis=0, keepdims=True)
    rest = jnp.where(eio == i1, -jnp.inf, logits)
    m2 = jnp.max(rest, axis=0, keepdims=True)
    i2 = jnp.min(jnp.where(rest == m2, eio, n_e), axis=0, keepdims=True)
    t = jnp.exp(m2 - m1)
    den = 1.0 + t
    ei_ref[...] = jnp.concatenate([i1, i2], axis=0)
    w_ref[...] = jnp.concatenate([1.0 / den, t / den], axis=0)


def _router(x, g, sh, sc, router_w):
    B, S, D = x.shape
    E = router_w.shape[1]
    tm = 512
    per_b = S // tm
    return pl.pallas_call(
        _router_body,
        grid=(B, per_b),
        in_specs=[pl.BlockSpec((1, tm, D), lambda b, i: (b, i, 0)),
                  pl.BlockSpec((1, D), lambda b, i: (0, 0)),
                  pl.BlockSpec((1, 1, D), lambda b, i: (b, 0, 0)),
                  pl.BlockSpec((1, 1, D), lambda b, i: (b, 0, 0)),
                  pl.BlockSpec((E, D), lambda b, i: (0, 0))],
        out_specs=[pl.BlockSpec((tm, D), lambda b, i: (b * per_b + i, 0)),
                   pl.BlockSpec((TOP_K_EXPERTS, tm), lambda b, i: (0, b * per_b + i)),
                   pl.BlockSpec((TOP_K_EXPERTS, tm), lambda b, i: (0, b * per_b + i))],
        out_shape=[jax.ShapeDtypeStruct((B * S, D), F32),
                   jax.ShapeDtypeStruct((TOP_K_EXPERTS, B * S), jnp.int32),
                   jax.ShapeDtypeStruct((TOP_K_EXPERTS, B * S), F32)],
        compiler_params=_params(2),
        name="moe_router",
    )(x, g.reshape(1, D), sh, sc, router_w.T)


def _route_plan(ei, tm, n_tiles):
    n_tok = ei.shape[1]
    e_flat = ei.reshape(-1)
    onehot = (e_flat[:, None] == jnp.arange(N_EXPERTS, dtype=jnp.int32)[None, :]).astype(jnp.int32)
    csum = jnp.cumsum(onehot, axis=0)
    rank = jnp.take_along_axis(csum, e_flat[:, None], axis=1)[:, 0] - 1
    counts = csum[-1]
    padded = ((counts + tm - 1) // tm) * tm
    ends = jnp.cumsum(padded)
    offs = ends - padded
    pos = (offs[e_flat] + rank).astype(jnp.int32)
    flat = jnp.zeros((n_tiles * tm,), jnp.int32).at[pos].set(
        jnp.arange(TOP_K_EXPERTS * n_tok, dtype=jnp.int32))
    starts = jnp.arange(n_tiles, dtype=jnp.int32) * tm
    t_exp = jnp.minimum(jnp.searchsorted(ends, starts, side="right"), N_EXPERTS - 1).astype(jnp.int32)
    t_rows = jnp.clip(offs[t_exp] + counts[t_exp] - starts, 0, tm)
    t_rows = jnp.where(starts < ends[-1], t_rows, 0).astype(jnp.int32)
    return flat, t_exp, t_rows


def _moe_ffn_body(te_ref, tr_ref, flat_ref, h_hbm, wg_ref, wu_ref, w2_ref, ys_hbm,
                  xstage, xb_sc, acc_sc, ybuf, gsem, ssem, *, tm, n_chunks, n_tiles, n_tok):
    i = pl.program_id(0)
    j = pl.program_id(1)
    n_rows = tr_ref[i]
    valid = n_rows > 0

    def gather_start(tile):
        def one(r, _):
            tok = flat_ref[tile * tm + r] & (n_tok - 1)
            pltpu.make_async_copy(h_hbm.at[pl.ds(tok, 1)], xstage.at[pl.ds(r, 1)], gsem).start()
            return 0
        lax.fori_loop(0, tm, one, 0, unroll=8)

    def scatter_row(tile, r):
        return pltpu.make_async_copy(ybuf.at[pl.ds(r, 1)],
                                     ys_hbm.at[pl.ds(flat_ref[tile * tm + r], 1)], ssem)

    def scatter_wait(tile, rows):
        def one(r, _):
            scatter_row(tile, r).wait()
            return 0
        lax.fori_loop(0, rows, one, 0)

    @pl.when(j == 0)
    def _():
        @pl.when(jnp.logical_and(i == 0, valid))
        def _():
            gather_start(0)

        @pl.when(valid)
        def _():
            pltpu.make_async_copy(h_hbm.at[pl.ds(0, tm)], xstage, gsem).wait()
            xb_sc[...] = xstage[...].astype(BF16)

        nxt = jnp.minimum(i + 1, n_tiles - 1)

        @pl.when(jnp.logical_and(i + 1 < n_tiles, tr_ref[nxt] > 0))
        def _():
            gather_start(i + 1)

    @pl.when(valid)
    def _():
        y = _swiglu_chunk(xb_sc[...], wg_ref[0], wu_ref[0], w2_ref[0])

        @pl.when(j == 0)
        def _():
            acc_sc[...] = y

        @pl.when(jnp.logical_and(j > 0, j < n_chunks - 1))
        def _():
            acc_sc[...] += y

        @pl.when(j == n_chunks - 1)
        def _():
            prev = jnp.maximum(i - 1, 0)

            @pl.when(i > 0)
            def _():
                scatter_wait(prev, tr_ref[prev])

            ybuf[...] = acc_sc[...] + y

            def one(r, _):
                scatter_row(i, r).start()
                return 0
            lax.fori_loop(0, n_rows, one, 0)

    @pl.when(j == n_chunks - 1)
    def _():
        prev = jnp.maximum(i - 1, 0)

        @pl.when(jnp.logical_and(jnp.logical_not(valid), jnp.logical_and(i > 0, tr_ref[prev] > 0)))
        def _():
            scatter_wait(prev, tr_ref[prev])

        @pl.when(jnp.logical_and(valid, i == n_tiles - 1))
        def _():
            scatter_wait(i, n_rows)


def _moe_ffn(h, flat, t_exp, t_rows, w13, w2, tm):
    n_tok, D = h.shape
    Fh = w2.shape[1]
    n_tiles = t_exp.shape[0]
    tf = 512
    n_chunks = Fh // tf
    assert n_chunks >= 2 and n_tok & (n_tok - 1) == 0

    def wmap(col_off):
        return lambda i, j, te, tr, fl: (te[i], 0, j * jnp.minimum(tr[i], 1) + col_off)

    return pl.pallas_call(
        functools.partial(_moe_ffn_body, tm=tm, n_chunks=n_chunks, n_tiles=n_tiles, n_tok=n_tok),
        grid_spec=pltpu.PrefetchScalarGridSpec(
            num_scalar_prefetch=3,
            grid=(n_tiles, n_chunks),
            in_specs=[pl.BlockSpec(memory_space=pl.ANY),
                      pl.BlockSpec((1, D, tf), wmap(0)),
                      pl.BlockSpec((1, D, tf), wmap(n_chunks)),
                      pl.BlockSpec((1, tf, D), lambda i, j, te, tr, fl: (te[i], j * jnp.minimum(tr[i], 1), 0))],
            out_specs=pl.BlockSpec(memory_space=pl.ANY),
            scratch_shapes=[pltpu.VMEM((tm, D), F32), pltpu.VMEM((tm, D), BF16),
                            pltpu.VMEM((tm, D), F32), pltpu.VMEM((tm, D), F32),
                            pltpu.SemaphoreType.DMA, pltpu.SemaphoreType.DMA]),
        out_shape=jax.ShapeDtypeStruct((TOP_K_EXPERTS * n_tok, D), F32),
        compiler_params=_params(2),
        name="moe_expert_swiglu",
    )(t_exp, t_rows, flat, h, w13, w13, w2)


def _combine_body(x_ref, y0_ref, y1_ref, w_ref, gate_ref, fg_ref, o_ref, *, final_norm):
    w = w_ref[...]
    y = w[:, 0:1] * y0_ref[0] + w[:, 1:2] * y1_ref[0]
    out = x_ref[...] + gate_ref[0] * y
    if final_norm:
        ms = jnp.mean(out * out, axis=-1, keepdims=True)
        out = (out * lax.rsqrt(ms + EPS)) * fg_ref[...]
    o_ref[...] = out


def _moe_combine(x, ys, wts, gate, final_g, final_norm):
    B, S, D = x.shape
    n_tok = B * S
    rows = 512
    per_b = S // rows
    ys = ys.reshape(TOP_K_EXPERTS, n_tok, D)
    out = pl.pallas_call(
        functools.partial(_combine_body, final_norm=final_norm),
        grid=(n_tok // rows,),
        in_specs=[pl.BlockSpec((rows, D), lambda i: (i, 0)),
                  pl.BlockSpec((1, rows, D), lambda i: (0, i, 0)),
                  pl.BlockSpec((1, rows, D), lambda i: (1, i, 0)),
                  pl.BlockSpec((rows, TOP_K_EXPERTS), lambda i: (i, 0)),
                  pl.BlockSpec((1, 1, D), lambda i: (i // per_b, 0, 0)),
                  pl.BlockSpec((1, D), lambda i: (0, 0))],
        out_specs=pl.BlockSpec((rows, D), lambda i: (i, 0)),
        out_shape=jax.ShapeDtypeStruct((n_tok, D), F32),
        compiler_params=_params(1),
        name="moe_combine",
    )(x.reshape(n_tok, D), ys, ys, wts, gate, final_g.reshape(1, D))
    return out.reshape(B, S, D)


def _moe_layer(x, g, sh, sc, gate, router_w, w13, w2, final_g, final_norm):
    B, S, D = x.shape
    n_tok = B * S
    tm = 1024
    n_tiles = TOP_K_EXPERTS * n_tok // tm + N_EXPERTS
    h, ei, wts = _router(x, g, sh, sc, router_w)
    flat, t_exp, t_rows = _route_plan(ei, tm, n_tiles)
    ys = _moe_ffn(h, flat, t_exp, t_rows, w13, w2, tm)
    return _moe_combine(x, ys, wts.T, gate, final_g, final_norm)


def kernel(x, c, rel_bias, mod_w, mod_b, norm_mix_g, norm_ffn_g, a_wqkv, a_wo, kv_norm_g, kv_mod_w,
           kv_mod_b, b_wkv, b_wq, b_wo, ffn_w13, ffn_w2, router_w, moe_w13, moe_w2, final_norm_g):
    D = D_MODEL
    mods = _modulation(c, mod_w, mod_b)
    kv_mod = _modulation(c, kv_mod_w[None], kv_mod_b[None])[0]
    bias_tiles = _bias_tiles(rel_bias)
    k_sh = vt_sh = None
    for l in range(DEPTH):
        sh_a, sc_a, g_a, sh_f, sc_f, g_f = [mods[l][:, None, i * D:(i + 1) * D] for i in range(6)]
        if l < N_A_LAYERS:
            w = a_wqkv[l].astype(BF16)
            qk = _norm_mod_matmul(x, norm_mix_g[l], sh_a, sc_a, w[:, :2 * D])
            vt = _norm_mod_matmul_t(x, norm_mix_g[l], sh_a, sc_a, w[:, 2 * D:].T)
            attn = _moba_attention(qk, vt, bias_tiles)
            x = _matmul_gated_residual(attn, a_wo[l].astype(BF16), x, g_a)
        else:
            j = l - N_A_LAYERS
            if j == 0:
                w = b_wkv.astype(BF16)
                kv_sh, kv_sc = kv_mod[:, None, :D], kv_mod[:, None, D:]
                k_sh = _norm_mod_matmul(x, kv_norm_g, kv_sh, kv_sc, w[:, :D])
                vt_sh = _norm_mod_matmul_t(x, kv_norm_g, kv_sh, kv_sc, w[:, D:].T)
            q = _norm_mod_matmul(x, norm_mix_g[l], sh_a, sc_a, b_wq[j].astype(BF16))
            attn = _sb_attention(q, k_sh, vt_sh)
            x = _matmul_gated_residual(attn, b_wo[j].astype(BF16), x, g_a)
        if l % 2 == 0:
            x = _dense_ffn(x, norm_ffn_g[l], sh_f, sc_f, g_f,
                           ffn_w13[l // 2][None].astype(BF16), ffn_w2[l // 2][None].astype(BF16))
        else:
            x = _moe_layer(x, norm_ffn_g[l], sh_f, sc_f, g_f, router_w[l // 2],
                           moe_w13[l // 2].astype(BF16), moe_w2[l // 2].astype(BF16),
                           final_norm_g, l == DEPTH - 1)
    return x
```

```python
import functools
import math

import numpy as np
import jax
import jax.numpy as jnp
from jax import lax
from jax.experimental import pallas as pl
from jax.experimental.pallas import tpu as pltpu

D_MODEL = 1024
DEPTH = 4
N_HEADS = 16
HEAD_DIM = D_MODEL // N_HEADS
N_A_LAYERS = DEPTH // 2
MOBA_BLOCK = 256
MOBA_TOPK = 3
N_BUCKETS = 32
MAX_DISTANCE = 1024
N_EXPERTS = 8
TOP_K_EXPERTS = 2
EPS = 1e-6
NEG = -1e30

LANES = 128
HEADS_PER_TILE = LANES // HEAD_DIM
ATT_BLOCK = 256
N_BIAS_TILES = 6
SB_DEAD_LOG = -105.0
VMEM_LIMIT = 56 * 1024 * 1024

F32 = jnp.float32
BF16 = jnp.bfloat16
HIGHEST = lax.Precision.HIGHEST
_NT = (((1,), (1,)), ((), ()))


def _params(n_axes):
    return pltpu.CompilerParams(dimension_semantics=("arbitrary",) * n_axes,
                                vmem_limit_bytes=VMEM_LIMIT)


def _norm_mod(x, g, sh, sc):
    ms = jnp.mean(x * x, axis=-1, keepdims=True)
    y = (x * lax.rsqrt(ms + EPS)) * g
    return y * (1.0 + sc) + sh


def _mod_body(c_ref, w_ref, b_ref, o_ref):
    c = c_ref[...]
    s = c * jax.nn.sigmoid(c)
    o_ref[0] = jnp.dot(s, w_ref[0], preferred_element_type=F32, precision=HIGHEST) + b_ref[0]


def _modulation(c, w, b):
    L, D, M = w.shape
    B = c.shape[0]
    tn = 1024
    return pl.pallas_call(
        _mod_body,
        grid=(L, M // tn),
        in_specs=[pl.BlockSpec((B, D), lambda l, j: (0, 0)),
                  pl.BlockSpec((1, D, tn), lambda l, j: (l, 0, j)),
                  pl.BlockSpec((1, 1, tn), lambda l, j: (l, 0, j))],
        out_specs=pl.BlockSpec((1, B, tn), lambda l, j: (l, 0, j)),
        out_shape=jax.ShapeDtypeStruct((L, B, M), F32),
        compiler_params=_params(2),
        name="adaln_mod",
    )(c, w, b.reshape(L, 1, M))


def _nm_mm_body(x_ref, g_ref, sh_ref, sc_ref, w_ref, o_ref, h_sc):
    @pl.when(pl.program_id(2) == 0)
    def _():
        h_sc[...] = _norm_mod(x_ref[0], g_ref[...], sh_ref[0], sc_ref[0]).astype(BF16)

    o_ref[0] = jnp.dot(h_sc[...], w_ref[...], preferred_element_type=F32).astype(o_ref.dtype)


def _norm_mod_matmul(x, g, sh, sc, w):
    B, S, D = x.shape
    N = w.shape[1]
    tm, tn = 1024, 1024
    return pl.pallas_call(
        _nm_mm_body,
        grid=(B, S // tm, N // tn),
        in_specs=[pl.BlockSpec((1, tm, D), lambda b, i, j: (b, i, 0)),
                  pl.BlockSpec((1, D), lambda b, i, j: (0, 0)),
                  pl.BlockSpec((1, 1, D), lambda b, i, j: (b, 0, 0)),
                  pl.BlockSpec((1, 1, D), lambda b, i, j: (b, 0, 0)),
                  pl.BlockSpec((D, tn), lambda b, i, j: (0, j))],
        out_specs=pl.BlockSpec((1, tm, tn), lambda b, i, j: (b, i, j)),
        out_shape=jax.ShapeDtypeStruct((B, S, N), BF16),
        scratch_shapes=[pltpu.VMEM((tm, D), BF16)],
        compiler_params=_params(3),
        name="norm_mod_proj",
    )(x, g.reshape(1, D), sh, sc, w)


def _nm_mm_t_body(x_ref, g_ref, sh_ref, sc_ref, wt_ref, o_ref, h_sc):
    @pl.when(pl.program_id(2) == 0)
    def _():
        h_sc[...] = _norm_mod(x_ref[0], g_ref[...], sh_ref[0], sc_ref[0]).astype(BF16)

    res = lax.dot_general(wt_ref[...], h_sc[...], _NT, preferred_element_type=F32)
    for r in range(o_ref.shape[1]):
        o_ref[0, r] = res[:, r * ATT_BLOCK:(r + 1) * ATT_BLOCK].astype(o_ref.dtype)


def _norm_mod_matmul_t(x, g, sh, sc, wt):
    B, S, D = x.shape
    N = wt.shape[0]
    tm, tn = 1024, 1024
    T = ATT_BLOCK
    return pl.pallas_call(
        _nm_mm_t_body,
        grid=(B, S // tm, N // tn),
        in_specs=[pl.BlockSpec((1, tm, D), lambda b, i, j: (b, i, 0)),
                  pl.BlockSpec((1, D), lambda b, i, j: (0, 0)),
                  pl.BlockSpec((1, 1, D), lambda b, i, j: (b, 0, 0)),
                  pl.BlockSpec((1, 1, D), lambda b, i, j: (b, 0, 0)),
                  pl.BlockSpec((tn, D), lambda b, i, j: (j, 0))],
        out_specs=pl.BlockSpec((1, tm // T, tn, T), lambda b, i, j: (b, i, j, 0)),
        out_shape=jax.ShapeDtypeStruct((B, S // T, N, T), BF16),
        scratch_shapes=[pltpu.VMEM((tm, D), BF16)],
        compiler_params=_params(3),
        name="norm_mod_proj_t",
    )(x, g.reshape(1, D), sh, sc, wt)


def _mm_res_body(a_ref, w_ref, r_ref, gate_ref, o_ref):
    acc = jnp.dot(a_ref[0], w_ref[...], preferred_element_type=F32)
    o_ref[0] = r_ref[0] + gate_ref[0] * acc


def _matmul_gated_residual(a, w, res, gate):
    B, S, K = a.shape
    N = w.shape[1]
    tm, tn = 1024, 1024
    return pl.pallas_call(
        _mm_res_body,
        grid=(B, S // tm, N // tn),
        in_specs=[pl.BlockSpec((1, tm, K), lambda b, i, j: (b, i, 0)),
                  pl.BlockSpec((K, tn), lambda b, i, j: (0, j)),
                  pl.BlockSpec((1, tm, tn), lambda b, i, j: (b, i, j)),
                  pl.BlockSpec((1, 1, tn), lambda b, i, j: (b, 0, j))],
        out_specs=pl.BlockSpec((1, tm, tn), lambda b, i, j: (b, i, j)),
        out_shape=jax.ShapeDtypeStruct((B, S, N), F32),
        compiler_params=_params(3),
        name="out_proj_residual",
    )(a, w, res, gate)


def _pv_t(vt, ps, head_row):
    zeros = jnp.zeros_like(vt)
    vcat = jnp.concatenate(
        [jnp.where((head_row >= hh * HEAD_DIM) & (head_row < (hh + 1) * HEAD_DIM), vt, zeros)
         for hh in range(HEADS_PER_TILE)], axis=1)
    return jnp.dot(vcat, jnp.concatenate(ps, axis=0), preferred_element_type=F32)


def _per_head_rows(head_row, vals):
    assert HEADS_PER_TILE == 2
    return jnp.where(head_row < HEAD_DIM, vals[0], vals[1])


def _head_queries(q, lane):
    out = []
    for hh in range(HEADS_PER_TILE):
        in_head = (lane >= hh * HEAD_DIM) & (lane < (hh + 1) * HEAD_DIM)
        out.append(jnp.where(in_head, q, jnp.zeros_like(q)))
    return out


def _rel_bucket(dist):
    max_exact = N_BUCKETS // 2
    d = np.maximum(dist, 1).astype(np.float32)
    ratio = np.log(d / np.float32(max_exact)) / np.float32(math.log(MAX_DISTANCE / max_exact))
    large = max_exact + (ratio * np.float32(N_BUCKETS - max_exact)).astype(np.int32)
    large = np.minimum(large, N_BUCKETS - 1)
    return np.where(dist < max_exact, dist, large)


def _bias_tiles(rel_bias):
    T = ATT_BLOCK
    W = 2 * T
    H = rel_bias.shape[1]
    k = np.arange(W)
    lag = np.where(k <= T, k, k - W)
    dist = np.maximum(np.arange(N_BIAS_TILES)[:, None] * T + lag[None, :], 0)
    bucket = _rel_bucket(dist)
    assert (bucket[N_BIAS_TILES - 1] == N_BUCKETS - 1).all()
    onehot = np.eye(N_BUCKETS, dtype=np.float32)[bucket.reshape(-1)]
    vec = jnp.dot(jnp.asarray(onehot), rel_bias, precision=HIGHEST)
    a = vec.T.reshape(H, N_BIAS_TILES, W)
    tiled = jnp.tile(a, (1, 1, T))
    return tiled[:, :, :T * (W - 1)].reshape(H, N_BIAS_TILES, T, W - 1)[:, :, :, :T]


def _moba_body(q_ref, k_ref, vt_ref, bias_ref, o_ref, kmean_sc, sel_sc, s_sc, p_sc, *, nb):
    c = pl.program_id(2)
    T = ATT_BLOCK

    @pl.when(c == 0)
    def _():
        rest = jnp.concatenate(
            [jnp.sum(k_ref[0, n * T:(n + 1) * T, :].astype(F32), axis=0, keepdims=True) * (1.0 / T)
             for n in range(nb)], axis=0)
        for part in range(3):
            term = rest.astype(BF16)
            kmean_sc[part * nb:(part + 1) * nb, :] = term
            rest = rest - term.astype(F32)

    lane = lax.broadcasted_iota(jnp.int32, (T, LANES), 1)
    blk = lax.broadcasted_iota(jnp.int32, (nb, T), 0)
    key_i = lax.broadcasted_iota(jnp.int32, (T, T), 0)
    qry_i = lax.broadcasted_iota(jnp.int32, (T, T), 1)
    causal = key_i <= qry_i
    head_row = lax.broadcasted_iota(jnp.int32, (LANES, T), 0)
    kmean = kmean_sc[...]
    scale = HEAD_DIM ** -0.5

    qs = []
    for hh, qh in enumerate(_head_queries(q_ref[0], lane)):
        terms = lax.dot_general(kmean, qh, _NT, preferred_element_type=F32)
        gate = terms[:nb] + terms[nb:2 * nb] + terms[2 * nb:]
        gate = jnp.where(blk < c, gate, NEG)
        rank = jnp.zeros((nb, T), jnp.int32)
        for m in range(nb):
            gm = gate[m:m + 1, :]
            beats = (gm > gate) | ((gm == gate) & (blk > m))
            rank = rank + beats.astype(jnp.int32)
        sel_sc[hh] = ((rank < MOBA_TOPK) & (blk < c)).astype(F32)
        qs.append((qh.astype(F32) * scale).astype(BF16))

    def scores(n):
        kb = k_ref[0, pl.ds(pl.multiple_of(n * T, T), T), :]
        return [lax.dot_general(kb, qs[hh], _NT, preferred_element_type=F32)
                for hh in range(HEADS_PER_TILE)]

    def soften(n, raw, stats):
        dl = jnp.minimum(c - n, N_BIAS_TILES - 1)
        new_stats, alphas = [], []
        for hh in range(HEADS_PER_TILE):
            m, l = stats[2 * hh:2 * hh + 2]
            picked = sel_sc[hh, pl.ds(n, 1), :] > 0.5
            s = raw[hh] + bias_ref[hh, pl.ds(dl, 1)][0]
            m_new = jnp.where(picked, jnp.maximum(m, jnp.max(s, axis=0, keepdims=True)), m)
            p = jnp.exp(s - jnp.where(picked, m_new, -NEG))
            alpha = jnp.exp(m - m_new)
            new_stats += [m_new, alpha * l + jnp.sum(p, axis=0, keepdims=True)]
            alphas.append(alpha)
            p_sc[hh] = p.astype(BF16)
        return new_stats, alphas

    def apply_pending(blk, alphas, acc):
        ps = [p_sc[hh] for hh in range(HEADS_PER_TILE)]
        return _per_head_rows(head_row, alphas) * acc + _pv_t(vt_ref[0, blk], ps, head_row)

    raw_own = scores(c)
    raw_next = scores(0)
    for hh in range(HEADS_PER_TILE):
        s_sc[hh] = raw_next[hh]

    stats = []
    for hh in range(HEADS_PER_TILE):
        s = jnp.where(causal, raw_own[hh] + bias_ref[hh, 0], NEG)
        m = jnp.max(s, axis=0, keepdims=True)
        p = jnp.exp(s - m)
        stats += [m, jnp.sum(p, axis=0, keepdims=True)]
        p_sc[hh] = p.astype(BF16)
    ones = jnp.ones((1, T), F32)

    def body(t, carry):
        stats, alphas, acc = list(carry[:4]), list(carry[4:6]), carry[6]
        raw_t = [s_sc[hh] for hh in range(HEADS_PER_TILE)]
        acc = apply_pending(jnp.where(t == 0, c, t - 1), alphas, acc)
        raw_n = scores(jnp.minimum(t + 1, c - 1))
        stats, alphas = soften(t, raw_t, stats)
        for hh in range(HEADS_PER_TILE):
            s_sc[hh] = raw_n[hh]
        return tuple(stats) + tuple(alphas) + (acc,)

    init = tuple(stats) + (ones, ones, jnp.zeros((LANES, T), F32))
    fin = lax.fori_loop(0, lax.shift_right_logical(c, 1),
                        lambda i, carry: body(2 * i + 1, body(2 * i, carry)), init)
    fin = lax.cond((c & 1) == 1, lambda cr: body(c - 1, cr), lambda cr: cr, fin)
    acc = apply_pending(jnp.maximum(c - 1, 0), list(fin[4:6]), fin[6])
    o = acc / _per_head_rows(head_row, [fin[1], fin[3]])
    o_ref[0] = o.T.astype(o_ref.dtype)


def _moba_attention(qk, vt, bias_tiles):
    B, S, D2 = qk.shape
    D = D2 // 2
    T = ATT_BLOCK
    nb = S // T
    n_hp = D // LANES
    return pl.pallas_call(
        functools.partial(_moba_body, nb=nb),
        grid=(B, n_hp, nb),
        in_specs=[pl.BlockSpec((1, T, LANES), lambda b, h, i: (b, i, h)),
                  pl.BlockSpec((1, S, LANES), lambda b, h, i: (b, 0, n_hp + h)),
                  pl.BlockSpec((1, nb, LANES, T), lambda b, h, i: (b, 0, h, 0)),
                  pl.BlockSpec((HEADS_PER_TILE, N_BIAS_TILES, T, T), lambda b, h, i: (h, 0, 0, 0))],
        out_specs=pl.BlockSpec((1, T, LANES), lambda b, h, i: (b, i, h)),
        out_shape=jax.ShapeDtypeStruct((B, S, D), BF16),
        scratch_shapes=[pltpu.VMEM((3 * nb, LANES), BF16), pltpu.VMEM((HEADS_PER_TILE, nb, T), F32),
                        pltpu.VMEM((HEADS_PER_TILE, T, T), F32), pltpu.VMEM((HEADS_PER_TILE, T, T), BF16)],
        compiler_params=_params(3),
        name="moba_attention",
    )(qk, qk, vt, bias_tiles)


def _sb_body(q_ref, k_ref, vt_ref, o_ref):
    c = pl.program_id(2)
    T = ATT_BLOCK
    lane = lax.broadcasted_iota(jnp.int32, (T, LANES), 1)
    key_i = lax.broadcasted_iota(jnp.int32, (T, T), 0)
    qry_i = lax.broadcasted_iota(jnp.int32, (T, T), 1)
    strict = key_i < qry_i
    later = (qry_i > key_i).astype(BF16)
    later2 = jnp.concatenate([later, later], axis=1)
    head_row = lax.broadcasted_iota(jnp.int32, (LANES, T), 0)
    scale = HEAD_DIM ** -0.5
    qs = [(qh.astype(F32) * scale).astype(BF16) for qh in _head_queries(q_ref[0], lane)]

    def blocks(specs, state):
        rs, acc = list(state[:HEADS_PER_TILE]), state[-1]
        tiles = [(bi, hh) for bi in range(len(specs)) for hh in range(HEADS_PER_TILE)]
        kbs = [k_ref[0, pl.ds(pl.multiple_of(n * T, T), T), :] for n, _, _ in specs]
        z = {t: lax.dot_general(kbs[t[0]], qs[t[1]], _NT, preferred_element_type=F32) for t in tiles}
        lb, l1, hilo = {}, {}, {}
        for t in tiles:
            sp = jnp.log(1.0 + jnp.exp(-jnp.abs(z[t])))
            lb[t] = jnp.minimum(z[t], 0.0) - sp
            l1[t] = lb[t] - z[t]
            if specs[t[0]][1]:
                l1[t] = jnp.where(strict, l1[t], 0.0)
            hi = l1[t].astype(BF16)
            lo = (l1[t] - hi.astype(F32)).astype(BF16)
            hilo[t] = jnp.concatenate([hi, lo], axis=0)
        tail = {t: jnp.dot(later2, hilo[t], preferred_element_type=F32) for t in tiles}
        ps = {}
        for bi, hh in tiles:
            _, diag, keep = specs[bi]
            a = jnp.exp(lb[bi, hh] + tail[bi, hh] + rs[hh])
            if diag:
                a = jnp.where(strict, a, 0.0)
            ps[bi, hh] = a.astype(BF16)
            step = tail[bi, hh][0:1, :] + l1[bi, hh][0:1, :]
            if keep is not None:
                step = jnp.where(keep, step, 0.0)
            rs[hh] = rs[hh] + step
        for bi, (n, _, keep) in enumerate(specs):
            vt = vt_ref[0, n]
            if keep is not None:
                vt = jnp.where(keep, vt, jnp.zeros_like(vt))
            acc = acc + _pv_t(vt, [ps[bi, hh] for hh in range(HEADS_PER_TILE)], head_row)
        return tuple(rs) + (acc,)

    zero_r = jnp.zeros((1, T), F32)
    state = blocks([(c, True, None), (jnp.maximum(c - 1, 0), False, c > 0)],
                   (zero_r,) * HEADS_PER_TILE + (jnp.zeros((LANES, T), F32),))

    def cond(carry):
        live = jnp.max(jnp.maximum(carry[1], carry[2])) > SB_DEAD_LOG
        return jnp.logical_and(carry[0] < c, live)

    def body(carry):
        i = carry[0]
        return (i + 1,) + blocks([(c - 1 - i, False, None)], carry[1:])

    fin = lax.while_loop(cond, body, (jnp.int32(1),) + state)
    o_ref[0] = fin[-1].T.astype(o_ref.dtype)


def _sb_attention(q, k, vt):
    B, S, D = q.shape
    T = ATT_BLOCK
    nb = S // T
    n_hp = D // LANES
    return pl.pallas_call(
        _sb_body,
        grid=(B, n_hp, nb),
        in_specs=[pl.BlockSpec((1, T, LANES), lambda b, h, i: (b, i, h)),
                  pl.BlockSpec((1, S, LANES), lambda b, h, i: (b, 0, h)),
                  pl.BlockSpec((1, nb, LANES, T), lambda b, h, i: (b, 0, h, 0))],
        out_specs=pl.BlockSpec((1, T, LANES), lambda b, h, i: (b, i, h)),
        out_shape=jax.ShapeDtypeStruct((B, S, D), BF16),
        compiler_params=_params(3),
        name="stick_breaking_attention",
    )(q, k, vt)


def _swiglu_chunk(xb, wg, wu, w2):
    g = jnp.dot(xb, wg, preferred_element_type=F32)
    u = jnp.dot(xb, wu, preferred_element_type=F32)
    a = (g * jax.nn.sigmoid(g) * u).astype(BF16)
    return jnp.dot(a, w2, preferred_element_type=F32)


def _dense_ffn_body(x_ref, g_ref, sh_ref, sc_ref, gate_ref, wg_ref, wu_ref, w2_ref, o_ref,
                    xb_sc, acc_sc, *, n_chunks):
    j = pl.program_id(1)

    @pl.when(j == 0)
    def _():
        xb_sc[...] = _norm_mod(x_ref[...], g_ref[...], sh_ref[0], sc_ref[0]).astype(BF16)

    y = _swiglu_chunk(xb_sc[...], wg_ref[0], wu_ref[0], w2_ref[0])

    @pl.when(j == 0)
    def _():
        acc_sc[...] = y

    @pl.when(j > 0)
    def _():
        acc_sc[...] += y

    @pl.when(j == n_chunks - 1)
    def _():
        o_ref[...] = x_ref[...] + gate_ref[0] * acc_sc[...]


def _dense_ffn(x, g, sh, sc, gate, w13, w2):
    B, S, D = x.shape
    Fh = w2.shape[1]
    tm, tf = 512, 1408
    n_chunks = Fh // tf
    per_b = S // tm
    xf = x.reshape(B * S, D)
    out = pl.pallas_call(
        functools.partial(_dense_ffn_body, n_chunks=n_chunks),
        grid=(B * S // tm, n_chunks),
        in_specs=[pl.BlockSpec((tm, D), lambda i, j: (i, 0)),
                  pl.BlockSpec((1, D), lambda i, j: (0, 0)),
                  pl.BlockSpec((1, 1, D), lambda i, j: (i // per_b, 0, 0)),
                  pl.BlockSpec((1, 1, D), lambda i, j: (i // per_b, 0, 0)),
                  pl.BlockSpec((1, 1, D), lambda i, j: (i // per_b, 0, 0)),
                  pl.BlockSpec((1, D, tf), lambda i, j: (0, 0, j)),
                  pl.BlockSpec((1, D, tf), lambda i, j: (0, 0, j + n_chunks)),
                  pl.BlockSpec((1, tf, D), lambda i, j: (0, j, 0))],
        out_specs=pl.BlockSpec((tm, D), lambda i, j: (i, 0)),
        out_shape=jax.ShapeDtypeStruct((B * S, D), F32),
        scratch_shapes=[pltpu.VMEM((tm, D), BF16), pltpu.VMEM((tm, D), F32)],
        compiler_params=_params(2),
        name="dense_swiglu",
    )(xf, g.reshape(1, D), sh, sc, gate, w13, w13, w2)
    return out.reshape(B, S, D)


def _router_body(x_ref, g_ref, sh_ref, sc_ref, rwt_ref, h_ref, ei_ref, w_ref):
    h = _norm_mod(x_ref[0], g_ref[...], sh_ref[0], sc_ref[0])
    h_ref[...] = h
    logits = lax.dot_general(rwt_ref[...], h, _NT, preferred_element_type=F32,
                             precision=HIGHEST)
    n_e = logits.shape[0]
    eio = lax.broadcasted_iota(jnp.int32, logits.shape, 0)
    m1 = jnp.max(logits, axis=0, keepdims=True)
    i1 = jnp.min(jnp.where(logits == m1, eio, n_e), axis=0, keepdims=True)
    rest = jnp.where(eio == i1, -jnp.inf, logits)
    m2 = jnp.max(rest, axis=0, keepdims=True)
    i2 = jnp.min(jnp.where(rest == m2, eio, n_e), axis=0, keepdims=True)
    t = jnp.exp(m2 - m1)
    den = 1.0 + t
    ei_ref[...] = jnp.concatenate([i1, i2], axis=0)
    w_ref[...] = jnp.concatenate([1.0 / den, t / den], axis=0)


def _router(x, g, sh, sc, router_w):
    B, S, D = x.shape
    E = router_w.shape[1]
    tm = 512
    per_b = S // tm
    return pl.pallas_call(
        _router_body,
        grid=(B, per_b),
        in_specs=[pl.BlockSpec((1, tm, D), lambda b, i: (b, i, 0)),
                  pl.BlockSpec((1, D), lambda b, i: (0, 0)),
                  pl.BlockSpec((1, 1, D), lambda b, i: (b, 0, 0)),
                  pl.BlockSpec((1, 1, D), lambda b, i: (b, 0, 0)),
                  pl.BlockSpec((E, D), lambda b, i: (0, 0))],
        out_specs=[pl.BlockSpec((tm, D), lambda b, i: (b * per_b + i, 0)),
                   pl.BlockSpec((TOP_K_EXPERTS, tm), lambda b, i: (0, b * per_b + i)),
                   pl.BlockSpec((TOP_K_EXPERTS, tm), lambda b, i: (0, b * per_b + i))],
        out_shape=[jax.ShapeDtypeStruct((B * S, D), F32),
                   jax.ShapeDtypeStruct((TOP_K_EXPERTS, B * S), jnp.int32),
                   jax.ShapeDtypeStruct((TOP_K_EXPERTS, B * S), F32)],
        compiler_params=_params(2),
        name="moe_router",
    )(x, g.reshape(1, D), sh, sc, router_w.T)


def _route_plan(ei, tm, n_tiles):
    n_tok = ei.shape[1]
    e_flat = ei.reshape(-1)
    onehot = (e_flat[:, None] == jnp.arange(N_EXPERTS, dtype=jnp.int32)[None, :]).astype(jnp.int32)
    csum = jnp.cumsum(onehot, axis=0)
    rank = jnp.take_along_axis(csum, e_flat[:, None], axis=1)[:, 0] - 1
    counts = csum[-1]
    padded = ((counts + tm - 1) // tm) * tm
    ends = jnp.cumsum(padded)
    offs = ends - padded
    pos = (offs[e_flat] + rank).astype(jnp.int32)
    flat = jnp.zeros((n_tiles * tm,), jnp.int32).at[pos].set(
        jnp.arange(TOP_K_EXPERTS * n_tok, dtype=jnp.int32))
    starts = jnp.arange(n_tiles, dtype=jnp.int32) * tm
    t_exp = jnp.minimum(jnp.searchsorted(ends, starts, side="right"), N_EXPERTS - 1).astype(jnp.int32)
    t_rows = jnp.clip(offs[t_exp] + counts[t_exp] - starts, 0, tm)
    t_rows = jnp.where(starts < ends[-1], t_rows, 0).astype(jnp.int32)
    return flat, t_exp, t_rows


def _moe_ffn_body(te_ref, tr_ref, flat_ref, h_hbm, wg_ref, wu_ref, w2_ref, ys_hbm,
                  xstage, xb_sc, acc_sc, ybuf, gsem, ssem, *, tm, n_chunks, n_tiles, n_tok):
    i = pl.program_id(0)
    j = pl.program_id(1)
    n_rows = tr_ref[i]
    valid = n_rows > 0

    def gather_start(tile):
        def one(r, _):
            tok = flat_ref[tile * tm + r] & (n_tok - 1)
            pltpu.make_async_copy(h_hbm.at[pl.ds(tok, 1)], xstage.at[pl.ds(r, 1)], gsem).start()
            return 0
        lax.fori_loop(0, tm, one, 0, unroll=8)

    def scatter_row(tile, r):
        return pltpu.make_async_copy(ybuf.at[pl.ds(r, 1)],
                                     ys_hbm.at[pl.ds(flat_ref[tile * tm + r], 1)], ssem)

    def scatter_wait(tile, rows):
        def one(r, _):
            scatter_row(tile, r).wait()
            return 0
        lax.fori_loop(0, rows, one, 0)

    @pl.when(j == 0)
    def _():
        @pl.when(jnp.logical_and(i == 0, valid))
        def _():
            gather_start(0)

        @pl.when(valid)
        def _():
            pltpu.make_async_copy(h_hbm.at[pl.ds(0, tm)], xstage, gsem).wait()
            xb_sc[...] = xstage[...].astype(BF16)

        nxt = jnp.minimum(i + 1, n_tiles - 1)

        @pl.when(jnp.logical_and(i + 1 < n_tiles, tr_ref[nxt] > 0))
        def _():
            gather_start(i + 1)

    @pl.when(valid)
    def _():
        y = _swiglu_chunk(xb_sc[...], wg_ref[0], wu_ref[0], w2_ref[0])

        @pl.when(j == 0)
        def _():
            acc_sc[...] = y

        @pl.when(jnp.logical_and(j > 0, j < n_chunks - 1))
        def _():
            acc_sc[...] += y

        @pl.when(j == n_chunks - 1)
        def _():
            prev = jnp.maximum(i - 1, 0)

            @pl.when(i > 0)
            def _():
                scatter_wait(prev, tr_ref[prev])

            ybuf[...] = acc_sc[...] + y

            def one(r, _):
                scatter_row(i, r).start()
                return 0
            lax.fori_loop(0, n_rows, one, 0)

    @pl.when(j == n_chunks - 1)
    def _():
        prev = jnp.maximum(i - 1, 0)

        @pl.when(jnp.logical_and(jnp.logical_not(valid), jnp.logical_and(i > 0, tr_ref[prev] > 0)))
        def _():
            scatter_wait(prev, tr_ref[prev])

        @pl.when(jnp.logical_and(valid, i == n_tiles - 1))
        def _():
            scatter_wait(i, n_rows)


def _moe_ffn(h, flat, t_exp, t_rows, w13, w2, tm):
    n_tok, D = h.shape
    Fh = w2.shape[1]
    n_tiles = t_exp.shape[0]
    tf = 512
    n_chunks = Fh // tf
    assert n_chunks >= 2 and n_tok & (n_tok - 1) == 0

    def wmap(col_off):
        return lambda i, j, te, tr, fl: (te[i], 0, j * jnp.minimum(tr[i], 1) + col_off)

    return pl.pallas_call(
        functools.partial(_moe_ffn_body, tm=tm, n_chunks=n_chunks, n_tiles=n_tiles, n_tok=n_tok),
        grid_spec=pltpu.PrefetchScalarGridSpec(
            num_scalar_prefetch=3,
            grid=(n_tiles, n_chunks),
            in_specs=[pl.BlockSpec(memory_space=pl.ANY),
                      pl.BlockSpec((1, D, tf), wmap(0)),
                      pl.BlockSpec((1, D, tf), wmap(n_chunks)),
                      pl.BlockSpec((1, tf, D), lambda i, j, te, tr, fl: (te[i], j * jnp.minimum(tr[i], 1), 0))],
            out_specs=pl.BlockSpec(memory_space=pl.ANY),
            scratch_shapes=[pltpu.VMEM((tm, D), F32), pltpu.VMEM((tm, D), BF16),
                            pltpu.VMEM((tm, D), F32), pltpu.VMEM((tm, D), F32),
                            pltpu.SemaphoreType.DMA, pltpu.SemaphoreType.DMA]),
        out_shape=jax.ShapeDtypeStruct((TOP_K_EXPERTS * n_tok, D), F32),
        compiler_params=_params(2),
        name="moe_expert_swiglu",
    )(t_exp, t_rows, flat, h, w13, w13, w2)


def _combine_body(x_ref, y0_ref, y1_ref, w_ref, gate_ref, fg_ref, o_ref, *, final_norm):
    w = w_ref[...]
    y = w[:, 0:1] * y0_ref[0] + w[:, 1:2] * y1_ref[0]
    out = x_ref[...] + gate_ref[0] * y
    if final_norm:
        ms = jnp.mean(out * out, axis=-1, keepdims=True)
        out = (out * lax.rsqrt(ms + EPS)) * fg_ref[...]
    o_ref[...] = out


def _moe_combine(x, ys, wts, gate, final_g, final_norm):
    B, S, D = x.shape
    n_tok = B * S
    rows = 512
    per_b = S // rows
    ys = ys.reshape(TOP_K_EXPERTS, n_tok, D)
    out = pl.pallas_call(
        functools.partial(_combine_body, final_norm=final_norm),
        grid=(n_tok // rows,),
        in_specs=[pl.BlockSpec((rows, D), lambda i: (i, 0)),
                  pl.BlockSpec((1, rows, D), lambda i: (0, i, 0)),
                  pl.BlockSpec((1, rows, D), lambda i: (1, i, 0)),
                  pl.BlockSpec((rows, TOP_K_EXPERTS), lambda i: (i, 0)),
                  pl.BlockSpec((1, 1, D), lambda i: (i // per_b, 0, 0)),
                  pl.BlockSpec((1, D), lambda i: (0, 0))],
        out_specs=pl.BlockSpec((rows, D), lambda i: (i, 0)),
        out_shape=jax.ShapeDtypeStruct((n_tok, D), F32),
        compiler_params=_params(1),
        name="moe_combine",
    )(x.reshape(n_tok, D), ys, ys, wts, gate, final_g.reshape(1, D))
    return out.reshape(B, S, D)


def _moe_layer(x, g, sh, sc, gate, router_w, w13, w2, final_g, final_norm):
    B, S, D = x.shape
    n_tok = B * S
    tm = 1024
    n_tiles = TOP_K_EXPERTS * n_tok // tm + N_EXPERTS
    h, ei, wts = _router(x, g, sh, sc, router_w)
    flat, t_exp, t_rows = _route_plan(ei, tm, n_tiles)
    ys = _moe_ffn(h, flat, t_exp, t_rows, w13, w2, tm)
    return _moe_combine(x, ys, wts.T, gate, final_g, final_norm)


def kernel(x, c, rel_bias, mod_w, mod_b, norm_mix_g, norm_ffn_g, a_wqkv, a_wo, kv_norm_g, kv_mod_w,
           kv_mod_b, b_wkv, b_wq, b_wo, ffn_w13, ffn_w2, router_w, moe_w13, moe_w2, final_norm_g):
    D = D_MODEL
    mods = _modulation(c, mod_w, mod_b)
    kv_mod = _modulation(c, kv_mod_w[None], kv_mod_b[None])[0]
    bias_tiles = _bias_tiles(rel_bias)
    k_sh = vt_sh = None
    for l in range(DEPTH):
        sh_a, sc_a, g_a, sh_f, sc_f, g_f = [mods[l][:, None, i * D:(i + 1) * D] for i in range(6)]
        if l < N_A_LAYERS:
            w = a_wqkv[l].astype(BF16)
            qk = _norm_mod_matmul(x, norm_mix_g[l], sh_a, sc_a, w[:, :2 * D])
            vt = _norm_mod_matmul_t(x, norm_mix_g[l], sh_a, sc_a, w[:, 2 * D:].T)
            attn = _moba_attention(qk, vt, bias_tiles)
            x = _matmul_gated_residual(attn, a_wo[l].astype(BF16), x, g_a)
        else:
            j = l - N_A_LAYERS
            if j == 0:
                w = b_wkv.astype(BF16)
                kv_sh, kv_sc = kv_mod[:, None, :D], kv_mod[:, None, D:]
                k_sh = _norm_mod_matmul(x, kv_norm_g, kv_sh, kv_sc, w[:, :D])
                vt_sh = _norm_mod_matmul_t(x, kv_norm_g, kv_sh, kv_sc, w[:, D:].T)
            q = _norm_mod_matmul(x, norm_mix_g[l], sh_a, sc_a, b_wq[j].astype(BF16))
            attn = _sb_attention(q, k_sh, vt_sh)
            x = _matmul_gated_residual(attn, b_wo[j].astype(BF16), x, g_a)
        if l % 2 == 0:
            x = _dense_ffn(x, norm_ffn_g[l], sh_f, sc_f, g_f,
                           ffn_w13[l // 2][None].astype(BF16), ffn_w2[l // 2][None].astype(BF16))
        else:
            x = _moe_layer(x, norm_ffn_g[l], sh_f, sc_f, g_f, router_w[l // 2],
                           moe_w13[l // 2].astype(BF16), moe_w2[l // 2].astype(BF16),
                           final_norm_g, l == DEPTH - 1)
    return x
```

```python
import functools
import math

import numpy as np
import jax
import jax.numpy as jnp
from jax import lax
from jax.experimental import pallas as pl
from jax.experimental.pallas import tpu as pltpu

D_MODEL = 1024
DEPTH = 4
N_HEADS = 16
HEAD_DIM = D_MODEL // N_HEADS
N_A_LAYERS = DEPTH // 2
MOBA_BLOCK = 256
MOBA_TOPK = 3
N_BUCKETS = 32
MAX_DISTANCE = 1024
N_EXPERTS = 8
TOP_K_EXPERTS = 2
EPS = 1e-6
NEG = -1e30

LANES = 128
HEADS_PER_TILE = LANES // HEAD_DIM
ATT_BLOCK = 256
N_BIAS_TILES = 6
SB_DEAD_LOG = -105.0
VMEM_LIMIT = 56 * 1024 * 1024

F32 = jnp.float32
BF16 = jnp.bfloat16
HIGHEST = lax.Precision.HIGHEST
_NT = (((1,), (1,)), ((), ()))


def _params(n_axes):
    return pltpu.CompilerParams(dimension_semantics=("arbitrary",) * n_axes,
                                vmem_limit_bytes=VMEM_LIMIT)


def _norm_mod(x, g, sh, sc):
    ms = jnp.mean(x * x, axis=-1, keepdims=True)
    y = (x * lax.rsqrt(ms + EPS)) * g
    return y * (1.0 + sc) + sh


def _mod_body(c_ref, w_ref, b_ref, o_ref):
    c = c_ref[...]
    s = c * jax.nn.sigmoid(c)
    o_ref[0] = jnp.dot(s, w_ref[0], preferred_element_type=F32, precision=HIGHEST) + b_ref[0]


def _modulation(c, w, b):
    L, D, M = w.shape
    B = c.shape[0]
    tn = 1024
    return pl.pallas_call(
        _mod_body,
        grid=(L, M // tn),
        in_specs=[pl.BlockSpec((B, D), lambda l, j: (0, 0)),
                  pl.BlockSpec((1, D, tn), lambda l, j: (l, 0, j)),
                  pl.BlockSpec((1, 1, tn), lambda l, j: (l, 0, j))],
        out_specs=pl.BlockSpec((1, B, tn), lambda l, j: (l, 0, j)),
        out_shape=jax.ShapeDtypeStruct((L, B, M), F32),
        compiler_params=_params(2),
        name="adaln_mod",
    )(c, w, b.reshape(L, 1, M))


def _nm_mm_body(x_ref, g_ref, sh_ref, sc_ref, w_ref, o_ref, h_sc):
    @pl.when(pl.program_id(2) == 0)
    def _():
        h_sc[...] = _norm_mod(x_ref[0], g_ref[...], sh_ref[0], sc_ref[0]).astype(BF16)

    o_ref[0] = jnp.dot(h_sc[...], w_ref[...], preferred_element_type=F32).astype(o_ref.dtype)


def _norm_mod_matmul(x, g, sh, sc, w):
    B, S, D = x.shape
    N = w.shape[1]
    tm, tn = 1024, 1024
    return pl.pallas_call(
        _nm_mm_body,
        grid=(B, S // tm, N // tn),
        in_specs=[pl.BlockSpec((1, tm, D), lambda b, i, j: (b, i, 0)),
                  pl.BlockSpec((1, D), lambda b, i, j: (0, 0)),
                  pl.BlockSpec((1, 1, D), lambda b, i, j: (b, 0, 0)),
                  pl.BlockSpec((1, 1, D), lambda b, i, j: (b, 0, 0)),
                  pl.BlockSpec((D, tn), lambda b, i, j: (0, j))],
        out_specs=pl.BlockSpec((1, tm, tn), lambda b, i, j: (b, i, j)),
        out_shape=jax.ShapeDtypeStruct((B, S, N), BF16),
        scratch_shapes=[pltpu.VMEM((tm, D), BF16)],
        compiler_params=_params(3),
        name="norm_mod_proj",
    )(x, g.reshape(1, D), sh, sc, w)


def _nm_mm_t_body(x_ref, g_ref, sh_ref, sc_ref, wt_ref, o_ref, h_sc):
    @pl.when(pl.program_id(2) == 0)
    def _():
        h_sc[...] = _norm_mod(x_ref[0], g_ref[...], sh_ref[0], sc_ref[0]).astype(BF16)

    res = lax.dot_general(wt_ref[...], h_sc[...], _NT, preferred_element_type=F32)
    for r in range(o_ref.shape[1]):
        o_ref[0, r] = res[:, r * ATT_BLOCK:(r + 1) * ATT_BLOCK].astype(o_ref.dtype)


def _norm_mod_matmul_t(x, g, sh, sc, wt):
    B, S, D = x.shape
    N = wt.shape[0]
    tm, tn = 1024, 1024
    T = ATT_BLOCK
    return pl.pallas_call(
        _nm_mm_t_body,
        grid=(B, S // tm, N // tn),
        in_specs=[pl.BlockSpec((1, tm, D), lambda b, i, j: (b, i, 0)),
                  pl.BlockSpec((1, D), lambda b, i, j: (0, 0)),
                  pl.BlockSpec((1, 1, D), lambda b, i, j: (b, 0, 0)),
                  pl.BlockSpec((1, 1, D), lambda b, i, j: (b, 0, 0)),
                  pl.BlockSpec((tn, D), lambda b, i, j: (j, 0))],
        out_specs=pl.BlockSpec((1, tm // T, tn, T), lambda b, i, j: (b, i, j, 0)),
        out_shape=jax.ShapeDtypeStruct((B, S // T, N, T), BF16),
        scratch_shapes=[pltpu.VMEM((tm, D), BF16)],
        compiler_params=_params(3),
        name="norm_mod_proj_t",
    )(x, g.reshape(1, D), sh, sc, wt)


def _mm_res_body(a_ref, w_ref, r_ref, gate_ref, o_ref):
    acc = jnp.dot(a_ref[0], w_ref[...], preferred_element_type=F32)
    o_ref[0] = r_ref[0] + gate_ref[0] * acc


def _matmul_gated_residual(a, w, res, gate):
    B, S, K = a.shape
    N = w.shape[1]
    tm, tn = 1024, 1024
    return pl.pallas_call(
        _mm_res_body,
        grid=(B, S // tm, N // tn),
        in_specs=[pl.BlockSpec((1, tm, K), lambda b, i, j: (b, i, 0)),
                  pl.BlockSpec((K, tn), lambda b, i, j: (0, j)),
                  pl.BlockSpec((1, tm, tn), lambda b, i, j: (b, i, j)),
                  pl.BlockSpec((1, 1, tn), lambda b, i, j: (b, 0, j))],
        out_specs=pl.BlockSpec((1, tm, tn), lambda b, i, j: (b, i, j)),
        out_shape=jax.ShapeDtypeStruct((B, S, N), F32),
        compiler_params=_params(3),
        name="out_proj_residual",
    )(a, w, res, gate)


def _pv_t(vt, ps, head_row):
    zeros = jnp.zeros_like(vt)
    vcat = jnp.concatenate(
        [jnp.where((head_row >= hh * HEAD_DIM) & (head_row < (hh + 1) * HEAD_DIM), vt, zeros)
         for hh in range(HEADS_PER_TILE)], axis=1)
    return jnp.dot(vcat, jnp.concatenate(ps, axis=0), preferred_element_type=F32)


def _per_head_rows(head_row, vals):
    assert HEADS_PER_TILE == 2
    return jnp.where(head_row < HEAD_DIM, vals[0], vals[1])


def _head_queries(q, lane):
    out = []
    for hh in range(HEADS_PER_TILE):
        in_head = (lane >= hh * HEAD_DIM) & (lane < (hh + 1) * HEAD_DIM)
        out.append(jnp.where(in_head, q, jnp.zeros_like(q)))
    return out


def _rel_bucket(dist):
    max_exact = N_BUCKETS // 2
    d = np.maximum(dist, 1).astype(np.float32)
    ratio = np.log(d / np.float32(max_exact)) / np.float32(math.log(MAX_DISTANCE / max_exact))
    large = max_exact + (ratio * np.float32(N_BUCKETS - max_exact)).astype(np.int32)
    large = np.minimum(large, N_BUCKETS - 1)
    return np.where(dist < max_exact, dist, large)


def _bias_tiles(rel_bias):
    T = ATT_BLOCK
    W = 2 * T
    H = rel_bias.shape[1]
    k = np.arange(W)
    lag = np.where(k <= T, k, k - W)
    dist = np.maximum(np.arange(N_BIAS_TILES)[:, None] * T + lag[None, :], 0)
    bucket = _rel_bucket(dist)
    assert (bucket[N_BIAS_TILES - 1] == N_BUCKETS - 1).all()
    onehot = np.eye(N_BUCKETS, dtype=np.float32)[bucket.reshape(-1)]
    vec = jnp.dot(jnp.asarray(onehot), rel_bias, precision=HIGHEST)
    a = vec.T.reshape(H, N_BIAS_TILES, W)
    tiled = jnp.tile(a, (1, 1, T))
    return tiled[:, :, :T * (W - 1)].reshape(H, N_BIAS_TILES, T, W - 1)[:, :, :, :T]


def _moba_body(q_ref, k_ref, vt_ref, bias_ref, o_ref, kmean_sc, sel_sc, s_sc, p_sc, *, nb):
    c = pl.program_id(2)
    T = ATT_BLOCK

    @pl.when(c == 0)
    def _():
        rest = jnp.concatenate(
            [jnp.sum(k_ref[0, n * T:(n + 1) * T, :].astype(F32), axis=0, keepdims=True) * (1.0 / T)
             for n in range(nb)], axis=0)
        for part in range(3):
            term = rest.astype(BF16)
            kmean_sc[part * nb:(part + 1) * nb, :] = term
            rest = rest - term.astype(F32)

    lane = lax.broadcasted_iota(jnp.int32, (T, LANES), 1)
    blk = lax.broadcasted_iota(jnp.int32, (nb, T), 0)
    key_i = lax.broadcasted_iota(jnp.int32, (T, T), 0)
    qry_i = lax.broadcasted_iota(jnp.int32, (T, T), 1)
    causal = key_i <= qry_i
    head_row = lax.broadcasted_iota(jnp.int32, (LANES, T), 0)
    kmean = kmean_sc[...]
    scale = HEAD_DIM ** -0.5

    qs = []
    for hh, qh in enumerate(_head_queries(q_ref[0], lane)):
        terms = lax.dot_general(kmean, qh, _NT, preferred_element_type=F32)
        gate = terms[:nb] + terms[nb:2 * nb] + terms[2 * nb:]
        gate = jnp.where(blk < c, gate, NEG)
        rank = jnp.zeros((nb, T), jnp.int32)
        for m in range(nb):
            gm = gate[m:m + 1, :]
            beats = (gm > gate) | ((gm == gate) & (blk > m))
            rank = rank + beats.astype(jnp.int32)
        sel_sc[hh] = ((rank < MOBA_TOPK) & (blk < c)).astype(F32)
        qs.append((qh.astype(F32) * scale).astype(BF16))

    def scores(n):
        kb = k_ref[0, pl.ds(pl.multiple_of(n * T, T), T), :]
        return [lax.dot_general(kb, qs[hh], _NT, preferred_element_type=F32)
                for hh in range(HEADS_PER_TILE)]

    def soften(n, raw, stats):
        dl = jnp.minimum(c - n, N_BIAS_TILES - 1)
        new_stats, alphas = [], []
        for hh in range(HEADS_PER_TILE):
            m, l = stats[2 * hh:2 * hh + 2]
            picked = sel_sc[hh, pl.ds(n, 1), :] > 0.5
            s = raw[hh] + bias_ref[hh, pl.ds(dl, 1)][0]
            m_new = jnp.where(picked, jnp.maximum(m, jnp.max(s, axis=0, keepdims=True)), m)
            p = jnp.exp(s - jnp.where(picked, m_new, -NEG))
            alpha = jnp.exp(m - m_new)
            new_stats += [m_new, alpha * l + jnp.sum(p, axis=0, keepdims=True)]
            alphas.append(alpha)
            p_sc[hh] = p.astype(BF16)
        return new_stats, alphas

    def apply_pending(blk, alphas, acc):
        ps = [p_sc[hh] for hh in range(HEADS_PER_TILE)]
        return _per_head_rows(head_row, alphas) * acc + _pv_t(vt_ref[0, blk], ps, head_row)

    raw_own = scores(c)
    raw_next = scores(0)
    for hh in range(HEADS_PER_TILE):
        s_sc[hh] = raw_next[hh]

    stats = []
    for hh in range(HEADS_PER_TILE):
        s = jnp.where(causal, raw_own[hh] + bias_ref[hh, 0], NEG)
        m = jnp.max(s, axis=0, keepdims=True)
        p = jnp.exp(s - m)
        stats += [m, jnp.sum(p, axis=0, keepdims=True)]
        p_sc[hh] = p.astype(BF16)
    ones = jnp.ones((1, T), F32)

    def body(t, carry):
        stats, alphas, acc = list(carry[:4]), list(carry[4:6]), carry[6]
        raw_t = [s_sc[hh] for hh in range(HEADS_PER_TILE)]
        acc = apply_pending(jnp.where(t == 0, c, t - 1), alphas, acc)
        raw_n = scores(jnp.minimum(t + 1, c - 1))
        stats, alphas = soften(t, raw_t, stats)
        for hh in range(HEADS_PER_TILE):
            s_sc[hh] = raw_n[hh]
        return tuple(stats) + tuple(alphas) + (acc,)

    init = tuple(stats) + (ones, ones, jnp.zeros((LANES, T), F32))
    fin = lax.fori_loop(0, lax.shift_right_logical(c, 1),
                        lambda i, carry: body(2 * i + 1, body(2 * i, carry)), init)
    fin = lax.cond((c & 1) == 1, lambda cr: body(c - 1, cr), lambda cr: cr, fin)
    acc = apply_pending(jnp.maximum(c - 1, 0), list(fin[4:6]), fin[6])
    o = acc / _per_head_rows(head_row, [fin[1], fin[3]])
    o_ref[0] = o.T.astype(o_ref.dtype)


def _moba_attention(qk, vt, bias_tiles):
    B, S, D2 = qk.shape
    D = D2 // 2
    T = ATT_BLOCK
    nb = S // T
    n_hp = D // LANES
    return pl.pallas_call(
        functools.partial(_moba_body, nb=nb),
        grid=(B, n_hp, nb),
        in_specs=[pl.BlockSpec((1, T, LANES), lambda b, h, i: (b, i, h)),
                  pl.BlockSpec((1, S, LANES), lambda b, h, i: (b, 0, n_hp + h)),
                  pl.BlockSpec((1, nb, LANES, T), lambda b, h, i: (b, 0, h, 0)),
                  pl.BlockSpec((HEADS_PER_TILE, N_BIAS_TILES, T, T), lambda b, h, i: (h, 0, 0, 0))],
        out_specs=pl.BlockSpec((1, T, LANES), lambda b, h, i: (b, i, h)),
        out_shape=jax.ShapeDtypeStruct((B, S, D), BF16),
        scratch_shapes=[pltpu.VMEM((3 * nb, LANES), BF16), pltpu.VMEM((HEADS_PER_TILE, nb, T), F32),
                        pltpu.VMEM((HEADS_PER_TILE, T, T), F32), pltpu.VMEM((HEADS_PER_TILE, T, T), BF16)],
        compiler_params=_params(3),
        name="moba_attention",
    )(qk, qk, vt, bias_tiles)


def _sb_body(q_ref, k_ref, vt_ref, o_ref):
    c = pl.program_id(2)
    T = ATT_BLOCK
    lane = lax.broadcasted_iota(jnp.int32, (T, LANES), 1)
    key_i = lax.broadcasted_iota(jnp.int32, (T, T), 0)
    qry_i = lax.broadcasted_iota(jnp.int32, (T, T), 1)
    strict = key_i < qry_i
    later = (qry_i > key_i).astype(BF16)
    later2 = jnp.concatenate([later, later], axis=1)
    head_row = lax.broadcasted_iota(jnp.int32, (LANES, T), 0)
    scale = HEAD_DIM ** -0.5
    qs = [(qh.astype(F32) * scale).astype(BF16) for qh in _head_queries(q_ref[0], lane)]

    def blocks(specs, state):
        rs, acc = list(state[:HEADS_PER_TILE]), state[-1]
        tiles = [(bi, hh) for bi in range(len(specs)) for hh in range(HEADS_PER_TILE)]
        kbs = [k_ref[0, pl.ds(pl.multiple_of(n * T, T), T), :] for n, _, _ in specs]
        z = {t: lax.dot_general(kbs[t[0]], qs[t[1]], _NT, preferred_element_type=F32) for t in tiles}
        lb, l1, hilo = {}, {}, {}
        for t in tiles:
            sp = jnp.log(1.0 + jnp.exp(-jnp.abs(z[t])))
            lb[t] = jnp.minimum(z[t], 0.0) - sp
            l1[t] = lb[t] - z[t]
            if specs[t[0]][1]:
                l1[t] = jnp.where(strict, l1[t], 0.0)
            hi = l1[t].astype(BF16)
            lo = (l1[t] - hi.astype(F32)).astype(BF16)
            hilo[t] = jnp.concatenate([hi, lo], axis=0)
        tail = {t: jnp.dot(later2, hilo[t], preferred_element_type=F32) for t in tiles}
        ps = {}
        for bi, hh in tiles:
            _, diag, keep = specs[bi]
            a = jnp.exp(lb[bi, hh] + tail[bi, hh] + rs[hh])
            if diag:
                a = jnp.where(strict, a, 0.0)
            ps[bi, hh] = a.astype(BF16)
            step = tail[bi, hh][0:1, :] + l1[bi, hh][0:1, :]
            if keep is not None:
                step = jnp.where(keep, step, 0.0)
            rs[hh] = rs[hh] + step
        for bi, (n, _, keep) in enumerate(specs):
            vt = vt_ref[0, n]
            if keep is not None:
                vt = jnp.where(keep, vt, jnp.zeros_like(vt))
            acc = acc + _pv_t(vt, [ps[bi, hh] for hh in range(HEADS_PER_TILE)], head_row)
        return tuple(rs) + (acc,)

    zero_r = jnp.zeros((1, T), F32)
    state = blocks([(c, True, None), (jnp.maximum(c - 1, 0), False, c > 0)],
                   (zero_r,) * HEADS_PER_TILE + (jnp.zeros((LANES, T), F32),))

    def cond(carry):
        live = jnp.max(jnp.maximum(carry[1], carry[2])) > SB_DEAD_LOG
        return jnp.logical_and(carry[0] < c, live)

    def body(carry):
        i = carry[0]
        return (i + 1,) + blocks([(c - 1 - i, False, None)], carry[1:])

    fin = lax.while_loop(cond, body, (jnp.int32(1),) + state)
    o_ref[0] = fin[-1].T.astype(o_ref.dtype)


def _sb_attention(q, k, vt):
    B, S, D = q.shape
    T = ATT_BLOCK
    nb = S // T
    n_hp = D // LANES
    return pl.pallas_call(
        _sb_body,
        grid=(B, n_hp, nb),
        in_specs=[pl.BlockSpec((1, T, LANES), lambda b, h, i: (b, i, h)),
                  pl.BlockSpec((1, S, LANES), lambda b, h, i: (b, 0, h)),
                  pl.BlockSpec((1, nb, LANES, T), lambda b, h, i: (b, 0, h, 0))],
        out_specs=pl.BlockSpec((1, T, LANES), lambda b, h, i: (b, i, h)),
        out_shape=jax.ShapeDtypeStruct((B, S, D), BF16),
        compiler_params=_params(3),
        name="stick_breaking_attention",
    )(q, k, vt)


def _swiglu_chunk(xb, wg, wu, w2):
    g = jnp.dot(xb, wg, preferred_element_type=F32)
    u = jnp.dot(xb, wu, preferred_element_type=F32)
    a = (g * jax.nn.sigmoid(g) * u).astype(BF16)
    return jnp.dot(a, w2, preferred_element_type=F32)


def _dense_ffn_body(x_ref, g_ref, sh_ref, sc_ref, gate_ref, wg_ref, wu_ref, w2_ref, o_ref,
                    xb_sc, acc_sc, *, n_chunks):
    j = pl.program_id(1)

    @pl.when(j == 0)
    def _():
        xb_sc[...] = _norm_mod(x_ref[...], g_ref[...], sh_ref[0], sc_ref[0]).astype(BF16)

    y = _swiglu_chunk(xb_sc[...], wg_ref[0], wu_ref[0], w2_ref[0])

    @pl.when(j == 0)
    def _():
        acc_sc[...] = y

    @pl.when(j > 0)
    def _():
        acc_sc[...] += y

    @pl.when(j == n_chunks - 1)
    def _():
        o_ref[...] = x_ref[...] + gate_ref[0] * acc_sc[...]


def _dense_ffn(x, g, sh, sc, gate, w13, w2):
    B, S, D = x.shape
    Fh = w2.shape[1]
    tm, tf = 512, 1408
    n_chunks = Fh // tf
    per_b = S // tm
    xf = x.reshape(B * S, D)
    out = pl.pallas_call(
        functools.partial(_dense_ffn_body, n_chunks=n_chunks),
        grid=(B * S // tm, n_chunks),
        in_specs=[pl.BlockSpec((tm, D), lambda i, j: (i, 0)),
                  pl.BlockSpec((1, D), lambda i, j: (0, 0)),
                  pl.BlockSpec((1, 1, D), lambda i, j: (i // per_b, 0, 0)),
                  pl.BlockSpec((1, 1, D), lambda i, j: (i // per_b, 0, 0)),
                  pl.BlockSpec((1, 1, D), lambda i, j: (i // per_b, 0, 0)),
                  pl.BlockSpec((1, D, tf), lambda i, j: (0, 0, j)),
                  pl.BlockSpec((1, D, tf), lambda i, j: (0, 0, j + n_chunks)),
                  pl.BlockSpec((1, tf, D), lambda i, j: (0, j, 0))],
        out_specs=pl.BlockSpec((tm, D), lambda i, j: (i, 0)),
        out_shape=jax.ShapeDtypeStruct((B * S, D), F32),
        scratch_shapes=[pltpu.VMEM((tm, D), BF16), pltpu.VMEM((tm, D), F32)],
        compiler_params=_params(2),
        name="dense_swiglu",
    )(xf, g.reshape(1, D), sh, sc, gate, w13, w13, w2)
    return out.reshape(B, S, D)


def _router_body(x_ref, g_ref, sh_ref, sc_ref, rwt_ref, h_ref, ei_ref, w_ref):
    h = _norm_mod(x_ref[0], g_ref[...], sh_ref[0], sc_ref[0])
    h_ref[...] = h
    logits = lax.dot_general(rwt_ref[...], h, _NT, preferred_element_type=F32,
                             precision=HIGHEST)
    n_e = logits.shape[0]
    eio = lax.broadcasted_iota(jnp.int32, logits.shape, 0)
    m1 = jnp.max(logits, axis=0, keepdims=True)
    i1 = jnp.min(jnp.where(logits == m1, eio, n_e), axis=0, keepdims=True)
    rest = jnp.where(eio == i1, -jnp.inf, logits)
    m2 = jnp.max(rest, axis=0, keepdims=True)
    i2 = jnp.min(jnp.where(rest == m2, eio, n_e), axis=0, keepdims=True)
    t = jnp.exp(m2 - m1)
    den = 1.0 + t
    ei_ref[...] = jnp.concatenate([i1, i2], axis=0)
    w_ref[...] = jnp.concatenate([1.0 / den, t / den], axis=0)


def _router(x, g, sh, sc, router_w):
    B, S, D = x.shape
    E = router_w.shape[1]
    tm = 512
    per_b = S // tm
    return pl.pallas_call(
        _router_body,
        grid=(B, per_b),
        in_specs=[pl.BlockSpec((1, tm, D), lambda b, i: (b, i, 0)),
                  pl.BlockSpec((1, D), lambda b, i: (0, 0)),
                  pl.BlockSpec((1, 1, D), lambda b, i: (b, 0, 0)),
                  pl.BlockSpec((1, 1, D), lambda b, i: (b, 0, 0)),
                  pl.BlockSpec((E, D), lambda b, i: (0, 0))],
        out_specs=[pl.BlockSpec((tm, D), lambda b, i: (b * per_b + i, 0)),
                   pl.BlockSpec((TOP_K_EXPERTS, tm), lambda b, i: (0, b * per_b + i)),
                   pl.BlockSpec((TOP_K_EXPERTS, tm), lambda b, i: (0, b * per_b + i))],
        out_shape=[jax.ShapeDtypeStruct((B * S, D), F32),
                   jax.ShapeDtypeStruct((TOP_K_EXPERTS, B * S), jnp.int32),
                   jax.ShapeDtypeStruct((TOP_K_EXPERTS, B * S), F32)],
        compiler_params=_params(2),
        name="moe_router",
    )(x, g.reshape(1, D), sh, sc, router_w.T)


def _route_plan(ei, tm, n_tiles):
    n_tok = ei.shape[1]
    e_flat = ei.reshape(-1)
    onehot = (e_flat[:, None] == jnp.arange(N_EXPERTS, dtype=jnp.int32)[None, :]).astype(jnp.int32)
    csum = jnp.cumsum(onehot, axis=0)
    rank = jnp.take_along_axis(csum, e_flat[:, None], axis=1)[:, 0] - 1
    counts = csum[-1]
    padded = ((counts + tm - 1) // tm) * tm
    ends = jnp.cumsum(padded)
    offs = ends - padded
    pos = (offs[e_flat] + rank).astype(jnp.int32)
    spare = TOP_K_EXPERTS * n_tok + jnp.arange((n_tiles + 1) * tm, dtype=jnp.int32) % tm
    flat = spare.at[tm + pos].set(jnp.arange(TOP_K_EXPERTS * n_tok, dtype=jnp.int32))
    starts = jnp.arange(n_tiles, dtype=jnp.int32) * tm
    t_exp = jnp.minimum(jnp.searchsorted(ends, starts, side="right"), N_EXPERTS - 1).astype(jnp.int32)
    t_rows = jnp.clip(offs[t_exp] + counts[t_exp] - starts, 0, tm)
    t_rows = jnp.where(starts < ends[-1], t_rows, 0).astype(jnp.int32)
    return flat, t_exp, t_rows


def _moe_ffn_body(te_ref, tr_ref, flat_ref, h_hbm, wg_ref, wu_ref, w2_ref, ys_hbm,
                  xstage, xb_sc, acc_sc, ybuf, gsem, ssem, *, tm, n_chunks, n_tiles, n_tok):
    i = pl.program_id(0)
    j = pl.program_id(1)
    valid = tr_ref[i] > 0
    rows_per_step = tm // n_chunks

    def gather_row(tile, q, rr):
        tok = flat_ref[(tile + 1) * tm + q * rows_per_step + rr] & (n_tok - 1)
        return pltpu.make_async_copy(h_hbm.at[pl.ds(tok, 1)], xstage.at[q, pl.ds(rr, 1)], gsem)

    def scatter_row(tile, q, rr):
        dst = flat_ref[(tile + 1) * tm + q * rows_per_step + rr]
        return pltpu.make_async_copy(ybuf.at[q, pl.ds(rr, 1)], ys_hbm.at[pl.ds(dst, 1)], ssem)

    def gather_wait(steps):
        for q in range(steps):
            pltpu.make_async_copy(h_hbm.at[pl.ds(0, rows_per_step)], xstage.at[q], gsem).wait()

    def scatter_wait(steps):
        for q in range(steps):
            pltpu.make_async_copy(ybuf.at[q], ys_hbm.at[pl.ds(0, rows_per_step)], ssem).wait()

    def all_rows(start_row):
        def one(q, _):
            for rr in range(rows_per_step):
                start_row(q, rr).start()
            return 0
        lax.fori_loop(0, n_chunks, one, 0)

    @pl.when(valid)
    def _():
        @pl.when(j == 0)
        def _():
            @pl.when(i == 0)
            def _():
                all_rows(functools.partial(gather_row, 0))
                gather_wait(n_chunks)
                ybuf[...] = jnp.zeros_like(ybuf)

            xb_sc[...] = xstage[...].reshape(tm, -1).astype(BF16)

        for rr in range(rows_per_step):
            gather_row(i + 1, j, rr).start()
            scatter_row(i - 1, j, rr).start()

        xb = xb_sc[...]
        g = jnp.dot(xb, wg_ref[0], preferred_element_type=F32)
        u = jnp.dot(xb, wu_ref[0], preferred_element_type=F32)
        a = (g * jax.nn.sigmoid(g) * u).astype(BF16)
        gather_wait(1)
        scatter_wait(1)
        y = jnp.dot(a, w2_ref[0], preferred_element_type=F32)

        @pl.when(j == 0)
        def _():
            acc_sc[...] = y

        @pl.when(jnp.logical_and(j > 0, j < n_chunks - 1))
        def _():
            acc_sc[...] += y

        @pl.when(j == n_chunks - 1)
        def _():
            ybuf[...] = (acc_sc[...] + y).reshape(ybuf.shape)

    prv = jnp.maximum(i - 1, 0)

    @pl.when(jnp.logical_and(jnp.logical_and(jnp.logical_not(valid), j == 0),
                             jnp.logical_and(i > 0, tr_ref[prv] > 0)))
    def _():
        all_rows(functools.partial(scatter_row, prv))
        scatter_wait(n_chunks)


def _moe_ffn(h, flat, t_exp, t_rows, w13, w2, tm):
    n_tok, D = h.shape
    Fh = w2.shape[1]
    n_tiles = t_exp.shape[0]
    n_chunks = 4
    tf = Fh // n_chunks
    assert tf % LANES == 0 and tm % n_chunks == 0 and n_tok & (n_tok - 1) == 0 and n_tok >= tm

    def wmap(col_off):
        return lambda i, j, te, tr, fl: (te[i], 0, j * jnp.minimum(tr[i], 1) + col_off)

    return pl.pallas_call(
        functools.partial(_moe_ffn_body, tm=tm, n_chunks=n_chunks, n_tiles=n_tiles, n_tok=n_tok),
        grid_spec=pltpu.PrefetchScalarGridSpec(
            num_scalar_prefetch=3,
            grid=(n_tiles, n_chunks),
            in_specs=[pl.BlockSpec(memory_space=pl.ANY),
                      pl.BlockSpec((1, D, tf), wmap(0)),
                      pl.BlockSpec((1, D, tf), wmap(n_chunks)),
                      pl.BlockSpec((1, tf, D), lambda i, j, te, tr, fl: (te[i], j * jnp.minimum(tr[i], 1), 0))],
            out_specs=pl.BlockSpec(memory_space=pl.ANY),
            scratch_shapes=[pltpu.VMEM((n_chunks, tm // n_chunks, D), F32), pltpu.VMEM((tm, D), BF16),
                            pltpu.VMEM((tm, D), F32), pltpu.VMEM((n_chunks, tm // n_chunks, D), F32),
                            pltpu.SemaphoreType.DMA, pltpu.SemaphoreType.DMA]),
        out_shape=jax.ShapeDtypeStruct((TOP_K_EXPERTS * n_tok + tm, D), F32),
        compiler_params=_params(2),
        name="moe_expert_swiglu",
    )(t_exp, t_rows, flat, h, w13, w13, w2)


def _combine_body(x_ref, y0_ref, y1_ref, w_ref, gate_ref, fg_ref, o_ref, *, final_norm):
    w = w_ref[...]
    y = w[:, 0:1] * y0_ref[...] + w[:, 1:2] * y1_ref[...]
    out = x_ref[...] + gate_ref[0] * y
    if final_norm:
        ms = jnp.mean(out * out, axis=-1, keepdims=True)
        out = (out * lax.rsqrt(ms + EPS)) * fg_ref[...]
    o_ref[...] = out


def _moe_combine(x, ys, wts, gate, final_g, final_norm):
    B, S, D = x.shape
    n_tok = B * S
    rows = 512
    per_b = S // rows
    out = pl.pallas_call(
        functools.partial(_combine_body, final_norm=final_norm),
        grid=(n_tok // rows,),
        in_specs=[pl.BlockSpec((rows, D), lambda i: (i, 0)),
                  pl.BlockSpec((rows, D), lambda i: (i, 0)),
                  pl.BlockSpec((rows, D), lambda i: (n_tok // rows + i, 0)),
                  pl.BlockSpec((rows, TOP_K_EXPERTS), lambda i: (i, 0)),
                  pl.BlockSpec((1, 1, D), lambda i: (i // per_b, 0, 0)),
                  pl.BlockSpec((1, D), lambda i: (0, 0))],
        out_specs=pl.BlockSpec((rows, D), lambda i: (i, 0)),
        out_shape=jax.ShapeDtypeStruct((n_tok, D), F32),
        compiler_params=_params(1),
        name="moe_combine",
    )(x.reshape(n_tok, D), ys, ys, wts, gate, final_g.reshape(1, D))
    return out.reshape(B, S, D)


def _moe_layer(x, g, sh, sc, gate, router_w, w13, w2, final_g, final_norm):
    B, S, D = x.shape
    n_tok = B * S
    tm = 1024
    n_tiles = TOP_K_EXPERTS * n_tok // tm + N_EXPERTS
    h, ei, wts = _router(x, g, sh, sc, router_w)
    flat, t_exp, t_rows = _route_plan(ei, tm, n_tiles)
    ys = _moe_ffn(h, flat, t_exp, t_rows, w13, w2, tm)
    return _moe_combine(x, ys, wts.T, gate, final_g, final_norm)


def kernel(x, c, rel_bias, mod_w, mod_b, norm_mix_g, norm_ffn_g, a_wqkv, a_wo, kv_norm_g, kv_mod_w,
           kv_mod_b, b_wkv, b_wq, b_wo, ffn_w13, ffn_w2, router_w, moe_w13, moe_w2, final_norm_g):
    D = D_MODEL
    mods = _modulation(c, mod_w, mod_b)
    kv_mod = _modulation(c, kv_mod_w[None], kv_mod_b[None])[0]
    bias_tiles = _bias_tiles(rel_bias)
    k_sh = vt_sh = None
    for l in range(DEPTH):
        sh_a, sc_a, g_a, sh_f, sc_f, g_f = [mods[l][:, None, i * D:(i + 1) * D] for i in range(6)]
        if l < N_A_LAYERS:
            w = a_wqkv[l].astype(BF16)
            qk = _norm_mod_matmul(x, norm_mix_g[l], sh_a, sc_a, w[:, :2 * D])
            vt = _norm_mod_matmul_t(x, norm_mix_g[l], sh_a, sc_a, w[:, 2 * D:].T)
            attn = _moba_attention(qk, vt, bias_tiles)
            x = _matmul_gated_residual(attn, a_wo[l].astype(BF16), x, g_a)
        else:
            j = l - N_A_LAYERS
            if j == 0:
                w = b_wkv.astype(BF16)
                kv_sh, kv_sc = kv_mod[:, None, :D], kv_mod[:, None, D:]
                k_sh = _norm_mod_matmul(x, kv_norm_g, kv_sh, kv_sc, w[:, :D])
                vt_sh = _norm_mod_matmul_t(x, kv_norm_g, kv_sh, kv_sc, w[:, D:].T)
            q = _norm_mod_matmul(x, norm_mix_g[l], sh_a, sc_a, b_wq[j].astype(BF16))
            attn = _sb_attention(q, k_sh, vt_sh)
            x = _matmul_gated_residual(attn, b_wo[j].astype(BF16), x, g_a)
        if l % 2 == 0:
            x = _dense_ffn(x, norm_ffn_g[l], sh_f, sc_f, g_f,
                           ffn_w13[l // 2][None].astype(BF16), ffn_w2[l // 2][None].astype(BF16))
        else:
            x = _moe_layer(x, norm_ffn_g[l], sh_f, sc_f, g_f, router_w[l // 2],
                           moe_w13[l // 2].astype(BF16), moe_w2[l // 2].astype(BF16),
                           final_norm_g, l == DEPTH - 1)
    return x
```

```python
import functools
import math

import numpy as np
import jax
import jax.numpy as jnp
from jax import lax
from jax.experimental import pallas as pl
from jax.experimental.pallas import tpu as pltpu

D_MODEL = 1024
DEPTH = 4
N_HEADS = 16
HEAD_DIM = D_MODEL // N_HEADS
N_A_LAYERS = DEPTH // 2
MOBA_BLOCK = 256
MOBA_TOPK = 3
N_BUCKETS = 32
MAX_DISTANCE = 1024
N_EXPERTS = 8
TOP_K_EXPERTS = 2
EPS = 1e-6
NEG = -1e30

LANES = 128
HEADS_PER_TILE = LANES // HEAD_DIM
ATT_BLOCK = 256
N_BIAS_TILES = 6
SB_DEAD_LOG = -105.0
VMEM_LIMIT = 56 * 1024 * 1024

F32 = jnp.float32
BF16 = jnp.bfloat16
HIGHEST = lax.Precision.HIGHEST
_NT = (((1,), (1,)), ((), ()))


def _params(n_axes):
    return pltpu.CompilerParams(dimension_semantics=("arbitrary",) * n_axes,
                                vmem_limit_bytes=VMEM_LIMIT)


def _norm_mod(x, g, sh, sc):
    ms = jnp.mean(x * x, axis=-1, keepdims=True)
    y = (x * lax.rsqrt(ms + EPS)) * g
    return y * (1.0 + sc) + sh


def _mod_body(c_ref, w_ref, b_ref, o_ref):
    c = c_ref[...]
    s = c * jax.nn.sigmoid(c)
    o_ref[0] = jnp.dot(s, w_ref[0], preferred_element_type=F32, precision=HIGHEST) + b_ref[0]


def _modulation(c, w, b):
    L, D, M = w.shape
    B = c.shape[0]
    tn = 1024
    return pl.pallas_call(
        _mod_body,
        grid=(L, M // tn),
        in_specs=[pl.BlockSpec((B, D), lambda l, j: (0, 0)),
                  pl.BlockSpec((1, D, tn), lambda l, j: (l, 0, j)),
                  pl.BlockSpec((1, 1, tn), lambda l, j: (l, 0, j))],
        out_specs=pl.BlockSpec((1, B, tn), lambda l, j: (l, 0, j)),
        out_shape=jax.ShapeDtypeStruct((L, B, M), F32),
        compiler_params=_params(2),
        name="adaln_mod",
    )(c, w, b.reshape(L, 1, M))


def _nm_mm_body(x_ref, g_ref, sh_ref, sc_ref, w_ref, o_ref, h_sc):
    @pl.when(pl.program_id(2) == 0)
    def _():
        h_sc[...] = _norm_mod(x_ref[0], g_ref[...], sh_ref[0], sc_ref[0]).astype(BF16)

    o_ref[0] = jnp.dot(h_sc[...], w_ref[...], preferred_element_type=F32).astype(o_ref.dtype)


def _norm_mod_matmul(x, g, sh, sc, w):
    B, S, D = x.shape
    N = w.shape[1]
    tm, tn = 1024, 1024
    return pl.pallas_call(
        _nm_mm_body,
        grid=(B, S // tm, N // tn),
        in_specs=[pl.BlockSpec((1, tm, D), lambda b, i, j: (b, i, 0)),
                  pl.BlockSpec((1, D), lambda b, i, j: (0, 0)),
                  pl.BlockSpec((1, 1, D), lambda b, i, j: (b, 0, 0)),
                  pl.BlockSpec((1, 1, D), lambda b, i, j: (b, 0, 0)),
                  pl.BlockSpec((D, tn), lambda b, i, j: (0, j))],
        out_specs=pl.BlockSpec((1, tm, tn), lambda b, i, j: (b, i, j)),
        out_shape=jax.ShapeDtypeStruct((B, S, N), BF16),
        scratch_shapes=[pltpu.VMEM((tm, D), BF16)],
        compiler_params=_params(3),
        name="norm_mod_proj",
    )(x, g.reshape(1, D), sh, sc, w)


def _nm_mm_t_body(x_ref, g_ref, sh_ref, sc_ref, wt_ref, o_ref, h_sc):
    @pl.when(pl.program_id(2) == 0)
    def _():
        h_sc[...] = _norm_mod(x_ref[0], g_ref[...], sh_ref[0], sc_ref[0]).astype(BF16)

    res = lax.dot_general(wt_ref[...], h_sc[...], _NT, preferred_element_type=F32)
    for r in range(o_ref.shape[1]):
        o_ref[0, r] = res[:, r * ATT_BLOCK:(r + 1) * ATT_BLOCK].astype(o_ref.dtype)


def _norm_mod_matmul_t(x, g, sh, sc, wt):
    B, S, D = x.shape
    N = wt.shape[0]
    tm, tn = 1024, 1024
    T = ATT_BLOCK
    return pl.pallas_call(
        _nm_mm_t_body,
        grid=(B, S // tm, N // tn),
        in_specs=[pl.BlockSpec((1, tm, D), lambda b, i, j: (b, i, 0)),
                  pl.BlockSpec((1, D), lambda b, i, j: (0, 0)),
                  pl.BlockSpec((1, 1, D), lambda b, i, j: (b, 0, 0)),
                  pl.BlockSpec((1, 1, D), lambda b, i, j: (b, 0, 0)),
                  pl.BlockSpec((tn, D), lambda b, i, j: (j, 0))],
        out_specs=pl.BlockSpec((1, tm // T, tn, T), lambda b, i, j: (b, i, j, 0)),
        out_shape=jax.ShapeDtypeStruct((B, S // T, N, T), BF16),
        scratch_shapes=[pltpu.VMEM((tm, D), BF16)],
        compiler_params=_params(3),
        name="norm_mod_proj_t",
    )(x, g.reshape(1, D), sh, sc, wt)


def _mm_res_body(a_ref, w_ref, r_ref, gate_ref, o_ref):
    acc = jnp.dot(a_ref[0], w_ref[...], preferred_element_type=F32)
    o_ref[0] = r_ref[0] + gate_ref[0] * acc


def _matmul_gated_residual(a, w, res, gate):
    B, S, K = a.shape
    N = w.shape[1]
    tm, tn = 1024, 1024
    return pl.pallas_call(
        _mm_res_body,
        grid=(B, S // tm, N // tn),
        in_specs=[pl.BlockSpec((1, tm, K), lambda b, i, j: (b, i, 0)),
                  pl.BlockSpec((K, tn), lambda b, i, j: (0, j)),
                  pl.BlockSpec((1, tm, tn), lambda b, i, j: (b, i, j)),
                  pl.BlockSpec((1, 1, tn), lambda b, i, j: (b, 0, j))],
        out_specs=pl.BlockSpec((1, tm, tn), lambda b, i, j: (b, i, j)),
        out_shape=jax.ShapeDtypeStruct((B, S, N), F32),
        compiler_params=_params(3),
        name="out_proj_residual",
    )(a, w, res, gate)


def _pv_t(vt, ps, head_row):
    zeros = jnp.zeros_like(vt)
    vcat = jnp.concatenate(
        [jnp.where((head_row >= hh * HEAD_DIM) & (head_row < (hh + 1) * HEAD_DIM), vt, zeros)
         for hh in range(HEADS_PER_TILE)], axis=1)
    return jnp.dot(vcat, jnp.concatenate(ps, axis=0), preferred_element_type=F32)


def _per_head_rows(head_row, vals):
    assert HEADS_PER_TILE == 2
    return jnp.where(head_row < HEAD_DIM, vals[0], vals[1])


def _head_queries(q, lane):
    out = []
    for hh in range(HEADS_PER_TILE):
        in_head = (lane >= hh * HEAD_DIM) & (lane < (hh + 1) * HEAD_DIM)
        out.append(jnp.where(in_head, q, jnp.zeros_like(q)))
    return out


def _rel_bucket(dist):
    max_exact = N_BUCKETS // 2
    d = np.maximum(dist, 1).astype(np.float32)
    ratio = np.log(d / np.float32(max_exact)) / np.float32(math.log(MAX_DISTANCE / max_exact))
    large = max_exact + (ratio * np.float32(N_BUCKETS - max_exact)).astype(np.int32)
    large = np.minimum(large, N_BUCKETS - 1)
    return np.where(dist < max_exact, dist, large)


def _bias_tiles(rel_bias):
    T = ATT_BLOCK
    W = 2 * T
    H = rel_bias.shape[1]
    k = np.arange(W)
    lag = np.where(k <= T, k, k - W)
    dist = np.maximum(np.arange(N_BIAS_TILES)[:, None] * T + lag[None, :], 0)
    bucket = _rel_bucket(dist)
    assert (bucket[N_BIAS_TILES - 1] == N_BUCKETS - 1).all()
    onehot = np.eye(N_BUCKETS, dtype=np.float32)[bucket.reshape(-1)]
    vec = jnp.dot(jnp.asarray(onehot), rel_bias, precision=HIGHEST)
    a = vec.T.reshape(H, N_BIAS_TILES, W)
    tiled = jnp.tile(a, (1, 1, T))
    return tiled[:, :, :T * (W - 1)].reshape(H, N_BIAS_TILES, T, W - 1)[:, :, :, :T]


def _moba_body(q_ref, k_ref, vt_ref, bias_ref, o_ref, kmean_sc, sel_sc, s_sc, p_sc, *, nb):
    c = pl.program_id(2)
    T = ATT_BLOCK

    @pl.when(c == 0)
    def _():
        rest = jnp.concatenate(
            [jnp.sum(k_ref[0, n * T:(n + 1) * T, :].astype(F32), axis=0, keepdims=True) * (1.0 / T)
             for n in range(nb)], axis=0)
        for part in range(3):
            term = rest.astype(BF16)
            kmean_sc[part * nb:(part + 1) * nb, :] = term
            rest = rest - term.astype(F32)

    lane = lax.broadcasted_iota(jnp.int32, (T, LANES), 1)
    blk = lax.broadcasted_iota(jnp.int32, (nb, T), 0)
    key_i = lax.broadcasted_iota(jnp.int32, (T, T), 0)
    qry_i = lax.broadcasted_iota(jnp.int32, (T, T), 1)
    causal = key_i <= qry_i
    head_row = lax.broadcasted_iota(jnp.int32, (LANES, T), 0)
    kmean = kmean_sc[...]
    scale = HEAD_DIM ** -0.5

    qs = []
    for hh, qh in enumerate(_head_queries(q_ref[0], lane)):
        terms = lax.dot_general(kmean, qh, _NT, preferred_element_type=F32)
        gate = terms[:nb] + terms[nb:2 * nb] + terms[2 * nb:]
        gate = jnp.where(blk < c, gate, NEG)
        rank = jnp.zeros((nb, T), jnp.int32)
        for m in range(nb):
            gm = gate[m:m + 1, :]
            beats = (gm > gate) | ((gm == gate) & (blk > m))
            rank = rank + beats.astype(jnp.int32)
        sel_sc[hh] = ((rank < MOBA_TOPK) & (blk < c)).astype(F32)
        qs.append((qh.astype(F32) * scale).astype(BF16))

    def scores(n):
        kb = k_ref[0, pl.ds(pl.multiple_of(n * T, T), T), :]
        return [lax.dot_general(kb, qs[hh], _NT, preferred_element_type=F32)
                for hh in range(HEADS_PER_TILE)]

    acc_rows = LANES + 16
    acc_row = lax.broadcasted_iota(jnp.int32, (acc_rows, T), 0)
    first_head = (acc_row < HEAD_DIM) | ((acc_row >= LANES) & (acc_row < LANES + 8))
    grp = lax.broadcasted_iota(jnp.int32, (16, T), 0) // 8
    ones_rows = [(grp == hh).astype(BF16) for hh in range(HEADS_PER_TILE)]

    def soften(n, raw, ms):
        dl = jnp.minimum(c - n, N_BIAS_TILES - 1)
        new_ms, alphas = [], []
        for hh in range(HEADS_PER_TILE):
            m = ms[hh]
            picked = sel_sc[hh, pl.ds(n, 1), :] > 0.5
            s = raw[hh] + bias_ref[hh, pl.ds(dl, 1)][0]
            m_new = jnp.where(picked, jnp.maximum(m, jnp.max(s, axis=0, keepdims=True)), m)
            p_sc[hh] = jnp.exp(s - jnp.where(picked, m_new, -NEG)).astype(BF16)
            new_ms.append(m_new)
            alphas.append(jnp.exp(m - m_new))
        return new_ms, alphas

    def apply_pending(blk, alphas, acc):
        vt = vt_ref[0, blk]
        zeros = jnp.zeros_like(vt)
        lhs = jnp.concatenate(
            [jnp.concatenate([jnp.where((head_row >= hh * HEAD_DIM) & (head_row < (hh + 1) * HEAD_DIM),
                                        vt, zeros), ones_rows[hh]], axis=0)
             for hh in range(HEADS_PER_TILE)], axis=1)
        ps = jnp.concatenate([p_sc[hh] for hh in range(HEADS_PER_TILE)], axis=0)
        return (jnp.where(first_head, alphas[0], alphas[1]) * acc
                + jnp.dot(lhs, ps, preferred_element_type=F32))

    raw_own = scores(c)
    raw_next = scores(0)
    for hh in range(HEADS_PER_TILE):
        s_sc[hh] = raw_next[hh]

    ms = []
    for hh in range(HEADS_PER_TILE):
        s = jnp.where(causal, raw_own[hh] + bias_ref[hh, 0], NEG)
        m = jnp.max(s, axis=0, keepdims=True)
        p_sc[hh] = jnp.exp(s - m).astype(BF16)
        ms.append(m)
    ones = jnp.ones((1, T), F32)

    def body(t, carry):
        ms, alphas, acc = list(carry[:2]), list(carry[2:4]), carry[4]
        raw_t = [s_sc[hh] for hh in range(HEADS_PER_TILE)]
        acc = apply_pending(jnp.where(t == 0, c, t - 1), alphas, acc)
        raw_n = scores(jnp.minimum(t + 1, c - 1))
        ms, alphas = soften(t, raw_t, ms)
        for hh in range(HEADS_PER_TILE):
            s_sc[hh] = raw_n[hh]
        return tuple(ms) + tuple(alphas) + (acc,)

    n_pairs = lax.shift_right_logical(c + 1, 1)
    init = tuple(ms) + (ones, ones, jnp.zeros((acc_rows, T), F32))
    fin = lax.fori_loop(0, n_pairs, lambda i, carry: body(2 * i + 1, body(2 * i, carry)), init)
    acc = apply_pending(jnp.maximum(2 * n_pairs - 1, 0), list(fin[2:4]), fin[4])
    norm = _per_head_rows(head_row, [acc[LANES:LANES + 1], acc[LANES + 8:LANES + 9]])
    o_ref[0] = (acc[:LANES] / norm).T.astype(o_ref.dtype)


def _moba_attention(qk, vt, bias_tiles):
    B, S, D2 = qk.shape
    D = D2 // 2
    T = ATT_BLOCK
    nb = S // T
    n_hp = D // LANES
    return pl.pallas_call(
        functools.partial(_moba_body, nb=nb),
        grid=(B, n_hp, nb),
        in_specs=[pl.BlockSpec((1, T, LANES), lambda b, h, i: (b, i, h)),
                  pl.BlockSpec((1, S, LANES), lambda b, h, i: (b, 0, n_hp + h)),
                  pl.BlockSpec((1, nb, LANES, T), lambda b, h, i: (b, 0, h, 0)),
                  pl.BlockSpec((HEADS_PER_TILE, N_BIAS_TILES, T, T), lambda b, h, i: (h, 0, 0, 0))],
        out_specs=pl.BlockSpec((1, T, LANES), lambda b, h, i: (b, i, h)),
        out_shape=jax.ShapeDtypeStruct((B, S, D), BF16),
        scratch_shapes=[pltpu.VMEM((3 * nb, LANES), BF16), pltpu.VMEM((HEADS_PER_TILE, nb, T), F32),
                        pltpu.VMEM((HEADS_PER_TILE, T, T), F32), pltpu.VMEM((HEADS_PER_TILE, T, T), BF16)],
        compiler_params=_params(3),
        name="moba_attention",
    )(qk, qk, vt, bias_tiles)


SB_TILES_PER_STEP = 2


def _sb_body(q_ref, k_ref, vt_ref, o_ref):
    step = pl.program_id(2)
    T = ATT_BLOCK
    lane = lax.broadcasted_iota(jnp.int32, (T, LANES), 1)
    key_i = lax.broadcasted_iota(jnp.int32, (T, T), 0)
    qry_i = lax.broadcasted_iota(jnp.int32, (T, T), 1)
    strict = key_i < qry_i
    later = (qry_i > key_i).astype(BF16)
    later2 = jnp.concatenate([later, later], axis=1)
    head_row = lax.broadcasted_iota(jnp.int32, (LANES, T), 0)
    scale = HEAD_DIM ** -0.5
    cs = [step * SB_TILES_PER_STEP + u for u in range(SB_TILES_PER_STEP)]
    qs = [[(qh.astype(F32) * scale).astype(BF16)
           for qh in _head_queries(q_ref[0, u * T:(u + 1) * T, :], lane)] for u in range(SB_TILES_PER_STEP)]

    def run(jobs, states):
        rs = {u: list(st[:HEADS_PER_TILE]) for u, st in states.items()}
        accs = {u: st[-1] for u, st in states.items()}
        items = [(ji, hh) for ji in range(len(jobs)) for hh in range(HEADS_PER_TILE)]
        kbs = [k_ref[0, pl.ds(pl.multiple_of(n * T, T), T), :] for _, n, _, _ in jobs]
        z = {(ji, hh): lax.dot_general(kbs[ji], qs[jobs[ji][0]][hh], _NT, preferred_element_type=F32)
             for ji, hh in items}
        lb, l1, hilo = {}, {}, {}
        for t in items:
            sp = jnp.log(1.0 + jnp.exp(-jnp.abs(z[t])))
            lb[t] = jnp.minimum(z[t], 0.0) - sp
            l1[t] = lb[t] - z[t]
            if jobs[t[0]][2]:
                l1[t] = jnp.where(strict, l1[t], 0.0)
            hi = l1[t].astype(BF16)
            lo = (l1[t] - hi.astype(F32)).astype(BF16)
            hilo[t] = jnp.concatenate([hi, lo], axis=0)
        tail = {t: jnp.dot(later2, hilo[t], preferred_element_type=F32) for t in items}
        ps = {}
        for ji, hh in items:
            u, _, diag, keep = jobs[ji]
            a = jnp.exp(lb[ji, hh] + tail[ji, hh] + rs[u][hh])
            if diag:
                a = jnp.where(strict, a, 0.0)
            ps[ji, hh] = a.astype(BF16)
            inc = tail[ji, hh][0:1, :] + l1[ji, hh][0:1, :]
            if keep is not None:
                inc = jnp.where(keep, inc, 0.0)
            rs[u][hh] = rs[u][hh] + inc
        for ji, (u, n, _, keep) in enumerate(jobs):
            vt = vt_ref[0, n]
            if keep is not None:
                vt = jnp.where(keep, vt, jnp.zeros_like(vt))
            accs[u] = accs[u] + _pv_t(vt, [ps[ji, hh] for hh in range(HEADS_PER_TILE)], head_row)
        return {u: tuple(rs[u]) + (accs[u],) for u in states}

    zero = (jnp.zeros((1, T), F32),) * HEADS_PER_TILE + (jnp.zeros((LANES, T), F32),)
    jobs = []
    for u, c in enumerate(cs):
        jobs += [(u, c, True, None), (u, jnp.maximum(c - 1, 0), False, (c > 0) if u == 0 else None)]
    states = run(jobs, {u: zero for u in range(SB_TILES_PER_STEP)})

    for u, c in enumerate(cs):
        def cond(carry, c=c):
            live = jnp.max(jnp.maximum(carry[1], carry[2])) > SB_DEAD_LOG
            return jnp.logical_and(carry[0] < c, live)

        def body(carry, u=u, c=c):
            i = carry[0]
            return (i + 1,) + run([(u, c - 1 - i, False, None)], {u: carry[1:]})[u]

        fin = lax.while_loop(cond, body, (jnp.int32(1),) + states[u])
        o_ref[0, u * T:(u + 1) * T, :] = fin[-1].T.astype(o_ref.dtype)


def _sb_attention(q, k, vt):
    B, S, D = q.shape
    T = ATT_BLOCK
    nb = S // T
    n_hp = D // LANES
    tq = SB_TILES_PER_STEP * T
    return pl.pallas_call(
        _sb_body,
        grid=(B, n_hp, S // tq),
        in_specs=[pl.BlockSpec((1, tq, LANES), lambda b, h, i: (b, i, h)),
                  pl.BlockSpec((1, S, LANES), lambda b, h, i: (b, 0, h)),
                  pl.BlockSpec((1, nb, LANES, T), lambda b, h, i: (b, 0, h, 0))],
        out_specs=pl.BlockSpec((1, tq, LANES), lambda b, h, i: (b, i, h)),
        out_shape=jax.ShapeDtypeStruct((B, S, D), BF16),
        compiler_params=_params(3),
        name="stick_breaking_attention",
    )(q, k, vt)


def _swiglu_chunk(xb, wg, wu, w2):
    g = jnp.dot(xb, wg, preferred_element_type=F32)
    u = jnp.dot(xb, wu, preferred_element_type=F32)
    a = (g * jax.nn.sigmoid(g) * u).astype(BF16)
    return jnp.dot(a, w2, preferred_element_type=F32)


def _dense_ffn_body(x_ref, g_ref, sh_ref, sc_ref, gate_ref, wg_ref, wu_ref, w2_ref, o_ref,
                    xb_sc, acc_sc, *, n_chunks):
    j = pl.program_id(1)

    @pl.when(j == 0)
    def _():
        xb_sc[...] = _norm_mod(x_ref[...], g_ref[...], sh_ref[0], sc_ref[0]).astype(BF16)

    y = _swiglu_chunk(xb_sc[...], wg_ref[0], wu_ref[0], w2_ref[0])

    @pl.when(j == 0)
    def _():
        acc_sc[...] = y

    @pl.when(j > 0)
    def _():
        acc_sc[...] += y

    @pl.when(j == n_chunks - 1)
    def _():
        o_ref[...] = x_ref[...] + gate_ref[0] * acc_sc[...]


def _dense_ffn(x, g, sh, sc, gate, w13, w2):
    B, S, D = x.shape
    Fh = w2.shape[1]
    tm, tf = 512, 1408
    n_chunks = Fh // tf
    per_b = S // tm
    xf = x.reshape(B * S, D)
    out = pl.pallas_call(
        functools.partial(_dense_ffn_body, n_chunks=n_chunks),
        grid=(B * S // tm, n_chunks),
        in_specs=[pl.BlockSpec((tm, D), lambda i, j: (i, 0)),
                  pl.BlockSpec((1, D), lambda i, j: (0, 0)),
                  pl.BlockSpec((1, 1, D), lambda i, j: (i // per_b, 0, 0)),
                  pl.BlockSpec((1, 1, D), lambda i, j: (i // per_b, 0, 0)),
                  pl.BlockSpec((1, 1, D), lambda i, j: (i // per_b, 0, 0)),
                  pl.BlockSpec((1, D, tf), lambda i, j: (0, 0, j)),
                  pl.BlockSpec((1, D, tf), lambda i, j: (0, 0, j + n_chunks)),
                  pl.BlockSpec((1, tf, D), lambda i, j: (0, j, 0))],
        out_specs=pl.BlockSpec((tm, D), lambda i, j: (i, 0)),
        out_shape=jax.ShapeDtypeStruct((B * S, D), F32),
        scratch_shapes=[pltpu.VMEM((tm, D), BF16), pltpu.VMEM((tm, D), F32)],
        compiler_params=_params(2),
        name="dense_swiglu",
    )(xf, g.reshape(1, D), sh, sc, gate, w13, w13, w2)
    return out.reshape(B, S, D)


def _router_body(x_ref, g_ref, sh_ref, sc_ref, rwt_ref, h_ref, ei_ref, w_ref):
    h = _norm_mod(x_ref[0], g_ref[...], sh_ref[0], sc_ref[0])
    h_ref[...] = h
    logits = lax.dot_general(rwt_ref[...], h, _NT, preferred_element_type=F32,
                             precision=HIGHEST)
    n_e = logits.shape[0]
    eio = lax.broadcasted_iota(jnp.int32, logits.shape, 0)
    m1 = jnp.max(logits, axis=0, keepdims=True)
    i1 = jnp.min(jnp.where(logits == m1, eio, n_e), axis=0, keepdims=True)
    rest = jnp.where(eio == i1, -jnp.inf, logits)
    m2 = jnp.max(rest, axis=0, keepdims=True)
    i2 = jnp.min(jnp.where(rest == m2, eio, n_e), axis=0, keepdims=True)
    t = jnp.exp(m2 - m1)
    den = 1.0 + t
    ei_ref[...] = jnp.concatenate([i1, i2], axis=0)
    w_ref[...] = jnp.concatenate([1.0 / den, t / den], axis=0)


def _router(x, g, sh, sc, router_w):
    B, S, D = x.shape
    E = router_w.shape[1]
    tm = 512
    per_b = S // tm
    return pl.pallas_call(
        _router_body,
        grid=(B, per_b),
        in_specs=[pl.BlockSpec((1, tm, D), lambda b, i: (b, i, 0)),
                  pl.BlockSpec((1, D), lambda b, i: (0, 0)),
                  pl.BlockSpec((1, 1, D), lambda b, i: (b, 0, 0)),
                  pl.BlockSpec((1, 1, D), lambda b, i: (b, 0, 0)),
                  pl.BlockSpec((E, D), lambda b, i: (0, 0))],
        out_specs=[pl.BlockSpec((tm, D), lambda b, i: (b * per_b + i, 0)),
                   pl.BlockSpec((TOP_K_EXPERTS, tm), lambda b, i: (0, b * per_b + i)),
                   pl.BlockSpec((TOP_K_EXPERTS, tm), lambda b, i: (0, b * per_b + i))],
        out_shape=[jax.ShapeDtypeStruct((B * S, D), F32),
                   jax.ShapeDtypeStruct((TOP_K_EXPERTS, B * S), jnp.int32),
                   jax.ShapeDtypeStruct((TOP_K_EXPERTS, B * S), F32)],
        compiler_params=_params(2),
        name="moe_router",
    )(x, g.reshape(1, D), sh, sc, router_w.T)


def _route_plan(ei, tm, n_tiles):
    n_tok = ei.shape[1]
    e_flat = ei.reshape(-1)
    onehot = (e_flat[:, None] == jnp.arange(N_EXPERTS, dtype=jnp.int32)[None, :]).astype(jnp.int32)
    csum = jnp.cumsum(onehot, axis=0)
    rank = jnp.take_along_axis(csum, e_flat[:, None], axis=1)[:, 0] - 1
    counts = csum[-1]
    padded = ((counts + tm - 1) // tm) * tm
    ends = jnp.cumsum(padded)
    offs = ends - padded
    pos = (offs[e_flat] + rank).astype(jnp.int32)
    spare = TOP_K_EXPERTS * n_tok + jnp.arange((n_tiles + 1) * tm, dtype=jnp.int32) % tm
    flat = spare.at[tm + pos].set(jnp.arange(TOP_K_EXPERTS * n_tok, dtype=jnp.int32))
    starts = jnp.arange(n_tiles, dtype=jnp.int32) * tm
    t_exp = jnp.minimum(jnp.searchsorted(ends, starts, side="right"), N_EXPERTS - 1).astype(jnp.int32)
    t_rows = jnp.clip(offs[t_exp] + counts[t_exp] - starts, 0, tm)
    t_rows = jnp.where(starts < ends[-1], t_rows, 0).astype(jnp.int32)
    return flat, t_exp, t_rows


def _moe_ffn_body(te_ref, tr_ref, flat_ref, h_hbm, wg_ref, wu_ref, w2_ref, ys_hbm,
                  xstage, xb_sc, acc_sc, ybuf, gsem, ssem, *, tm, n_chunks, n_tiles, n_tok):
    i = pl.program_id(0)
    j = pl.program_id(1)
    valid = tr_ref[i] > 0
    rows_per_step = tm // n_chunks

    def gather_row(tile, q, rr):
        tok = flat_ref[(tile + 1) * tm + q * rows_per_step + rr] & (n_tok - 1)
        return pltpu.make_async_copy(h_hbm.at[pl.ds(tok, 1)], xstage.at[q, pl.ds(rr, 1)], gsem)

    def scatter_row(tile, q, rr):
        dst = flat_ref[(tile + 1) * tm + q * rows_per_step + rr]
        return pltpu.make_async_copy(ybuf.at[q, pl.ds(rr, 1)], ys_hbm.at[pl.ds(dst, 1)], ssem)

    def gather_wait(steps):
        for q in range(steps):
            pltpu.make_async_copy(h_hbm.at[pl.ds(0, rows_per_step)], xstage.at[q], gsem).wait()

    def scatter_wait(steps):
        for q in range(steps):
            pltpu.make_async_copy(ybuf.at[q], ys_hbm.at[pl.ds(0, rows_per_step)], ssem).wait()

    def all_rows(start_row):
        def one(q, _):
            for rr in range(rows_per_step):
                start_row(q, rr).start()
            return 0
        lax.fori_loop(0, n_chunks, one, 0)

    @pl.when(valid)
    def _():
        @pl.when(j == 0)
        def _():
            @pl.when(i == 0)
            def _():
                all_rows(functools.partial(gather_row, 0))
                gather_wait(n_chunks)
                ybuf[...] = jnp.zeros_like(ybuf)

            xb_sc[...] = xstage[...].reshape(tm, -1).astype(BF16)

        for rr in range(rows_per_step):
            gather_row(i + 1, j, rr).start()
            scatter_row(i - 1, j, rr).start()

        xb = xb_sc[...]
        g = jnp.dot(xb, wg_ref[0], preferred_element_type=F32)
        u = jnp.dot(xb, wu_ref[0], preferred_element_type=F32)
        a = (g * jax.nn.sigmoid(g) * u).astype(BF16)
        gather_wait(1)
        scatter_wait(1)
        y = jnp.dot(a, w2_ref[0], preferred_element_type=F32)

        @pl.when(j == 0)
        def _():
            acc_sc[...] = y

        @pl.when(jnp.logical_and(j > 0, j < n_chunks - 1))
        def _():
            acc_sc[...] += y

        @pl.when(j == n_chunks - 1)
        def _():
            ybuf[...] = (acc_sc[...] + y).reshape(ybuf.shape)

    prv = jnp.maximum(i - 1, 0)

    @pl.when(jnp.logical_and(jnp.logical_and(jnp.logical_not(valid), j == 0),
                             jnp.logical_and(i > 0, tr_ref[prv] > 0)))
    def _():
        all_rows(functools.partial(scatter_row, prv))
        scatter_wait(n_chunks)


def _moe_ffn(h, flat, t_exp, t_rows, w13, w2, tm):
    n_tok, D = h.shape
    Fh = w2.shape[1]
    n_tiles = t_exp.shape[0]
    n_chunks = 4
    tf = Fh // n_chunks
    assert tf % LANES == 0 and tm % n_chunks == 0 and n_tok & (n_tok - 1) == 0 and n_tok >= tm

    def wmap(col_off):
        return lambda i, j, te, tr, fl: (te[i], 0, j * jnp.minimum(tr[i], 1) + col_off)

    return pl.pallas_call(
        functools.partial(_moe_ffn_body, tm=tm, n_chunks=n_chunks, n_tiles=n_tiles, n_tok=n_tok),
        grid_spec=pltpu.PrefetchScalarGridSpec(
            num_scalar_prefetch=3,
            grid=(n_tiles, n_chunks),
            in_specs=[pl.BlockSpec(memory_space=pl.ANY),
                      pl.BlockSpec((1, D, tf), wmap(0)),
                      pl.BlockSpec((1, D, tf), wmap(n_chunks)),
                      pl.BlockSpec((1, tf, D), lambda i, j, te, tr, fl: (te[i], j * jnp.minimum(tr[i], 1), 0))],
            out_specs=pl.BlockSpec(memory_space=pl.ANY),
            scratch_shapes=[pltpu.VMEM((n_chunks, tm // n_chunks, D), F32), pltpu.VMEM((tm, D), BF16),
                            pltpu.VMEM((tm, D), F32), pltpu.VMEM((n_chunks, tm // n_chunks, D), F32),
                            pltpu.SemaphoreType.DMA, pltpu.SemaphoreType.DMA]),
        out_shape=jax.ShapeDtypeStruct((TOP_K_EXPERTS * n_tok + tm, D), F32),
        compiler_params=_params(2),
        name="moe_expert_swiglu",
    )(t_exp, t_rows, flat, h, w13, w13, w2)


def _combine_body(x_ref, y0_ref, y1_ref, w_ref, gate_ref, fg_ref, o_ref, *, final_norm):
    w = w_ref[...]
    y = w[:, 0:1] * y0_ref[...] + w[:, 1:2] * y1_ref[...]
    out = x_ref[...] + gate_ref[0] * y
    if final_norm:
        ms = jnp.mean(out * out, axis=-1, keepdims=True)
        out = (out * lax.rsqrt(ms + EPS)) * fg_ref[...]
    o_ref[...] = out


def _moe_combine(x, ys, wts, gate, final_g, final_norm):
    B, S, D = x.shape
    n_tok = B * S
    rows = 512
    per_b = S // rows
    out = pl.pallas_call(
        functools.partial(_combine_body, final_norm=final_norm),
        grid=(n_tok // rows,),
        in_specs=[pl.BlockSpec((rows, D), lambda i: (i, 0)),
                  pl.BlockSpec((rows, D), lambda i: (i, 0)),
                  pl.BlockSpec((rows, D), lambda i: (n_tok // rows + i, 0)),
                  pl.BlockSpec((rows, TOP_K_EXPERTS), lambda i: (i, 0)),
                  pl.BlockSpec((1, 1, D), lambda i: (i // per_b, 0, 0)),
                  pl.BlockSpec((1, D), lambda i: (0, 0))],
        out_specs=pl.BlockSpec((rows, D), lambda i: (i, 0)),
        out_shape=jax.ShapeDtypeStruct((n_tok, D), F32),
        compiler_params=_params(1),
        name="moe_combine",
    )(x.reshape(n_tok, D), ys, ys, wts, gate, final_g.reshape(1, D))
    return out.reshape(B, S, D)


def _moe_layer(x, g, sh, sc, gate, router_w, w13, w2, final_g, final_norm):
    B, S, D = x.shape
    n_tok = B * S
    tm = 1024
    n_tiles = TOP_K_EXPERTS * n_tok // tm + N_EXPERTS
    h, ei, wts = _router(x, g, sh, sc, router_w)
    flat, t_exp, t_rows = _route_plan(ei, tm, n_tiles)
    ys = _moe_ffn(h, flat, t_exp, t_rows, w13, w2, tm)
    return _moe_combine(x, ys, wts.T, gate, final_g, final_norm)


def kernel(x, c, rel_bias, mod_w, mod_b, norm_mix_g, norm_ffn_g, a_wqkv, a_wo, kv_norm_g, kv_mod_w,
           kv_mod_b, b_wkv, b_wq, b_wo, ffn_w13, ffn_w2, router_w, moe_w13, moe_w2, final_norm_g):
    D = D_MODEL
    mods = _modulation(c, mod_w, mod_b)
    kv_mod = _modulation(c, kv_mod_w[None], kv_mod_b[None])[0]
    bias_tiles = _bias_tiles(rel_bias)
    k_sh = vt_sh = None
    for l in range(DEPTH):
        sh_a, sc_a, g_a, sh_f, sc_f, g_f = [mods[l][:, None, i * D:(i + 1) * D] for i in range(6)]
        if l < N_A_LAYERS:
            w = a_wqkv[l].astype(BF16)
            qk = _norm_mod_matmul(x, norm_mix_g[l], sh_a, sc_a, w[:, :2 * D])
            vt = _norm_mod_matmul_t(x, norm_mix_g[l], sh_a, sc_a, w[:, 2 * D:].T)
            attn = _moba_attention(qk, vt, bias_tiles)
            x = _matmul_gated_residual(attn, a_wo[l].astype(BF16), x, g_a)
        else:
            j = l - N_A_LAYERS
            if j == 0:
                w = b_wkv.astype(BF16)
                kv_sh, kv_sc = kv_mod[:, None, :D], kv_mod[:, None, D:]
                k_sh = _norm_mod_matmul(x, kv_norm_g, kv_sh, kv_sc, w[:, :D])
                vt_sh = _norm_mod_matmul_t(x, kv_norm_g, kv_sh, kv_sc, w[:, D:].T)
            q = _norm_mod_matmul(x, norm_mix_g[l], sh_a, sc_a, b_wq[j].astype(BF16))
            attn = _sb_attention(q, k_sh, vt_sh)
            x = _matmul_gated_residual(attn, b_wo[j].astype(BF16), x, g_a)
        if l % 2 == 0:
            x = _dense_ffn(x, norm_ffn_g[l], sh_f, sc_f, g_f,
                           ffn_w13[l // 2][None].astype(BF16), ffn_w2[l // 2][None].astype(BF16))
        else:
            x = _moe_layer(x, norm_ffn_g[l], sh_f, sc_f, g_f, router_w[l // 2],
                           moe_w13[l // 2].astype(BF16), moe_w2[l // 2].astype(BF16),
                           final_norm_g, l == DEPTH - 1)
    return x
```

```python
import functools
import math

import numpy as np
import jax
import jax.numpy as jnp
from jax import lax
from jax.experimental import pallas as pl
from jax.experimental.pallas import tpu as pltpu

D_MODEL = 1024
DEPTH = 4
N_HEADS = 16
HEAD_DIM = D_MODEL // N_HEADS
N_A_LAYERS = DEPTH // 2
MOBA_BLOCK = 256
MOBA_TOPK = 3
N_BUCKETS = 32
MAX_DISTANCE = 1024
N_EXPERTS = 8
TOP_K_EXPERTS = 2
EPS = 1e-6
NEG = -1e30

LANES = 128
HEADS_PER_TILE = LANES // HEAD_DIM
ATT_BLOCK = 256
N_BIAS_TILES = 6
SB_DEAD_LOG = -105.0
VMEM_LIMIT = 56 * 1024 * 1024

F32 = jnp.float32
BF16 = jnp.bfloat16
HIGHEST = lax.Precision.HIGHEST
_NT = (((1,), (1,)), ((), ()))


def _params(n_axes):
    return pltpu.CompilerParams(dimension_semantics=("arbitrary",) * n_axes,
                                vmem_limit_bytes=VMEM_LIMIT)


def _norm_mod(x, g, sh, sc):
    ms = jnp.mean(x * x, axis=-1, keepdims=True)
    y = (x * lax.rsqrt(ms + EPS)) * g
    return y * (1.0 + sc) + sh


def _mod_body(c_ref, w_ref, b_ref, o_ref):
    c = c_ref[...]
    s = c * jax.nn.sigmoid(c)
    o_ref[0] = jnp.dot(s, w_ref[0], preferred_element_type=F32, precision=HIGHEST) + b_ref[0]


def _modulation(c, w, b):
    L, D, M = w.shape
    B = c.shape[0]
    tn = 1024
    return pl.pallas_call(
        _mod_body,
        grid=(L, M // tn),
        in_specs=[pl.BlockSpec((B, D), lambda l, j: (0, 0)),
                  pl.BlockSpec((1, D, tn), lambda l, j: (l, 0, j)),
                  pl.BlockSpec((1, 1, tn), lambda l, j: (l, 0, j))],
        out_specs=pl.BlockSpec((1, B, tn), lambda l, j: (l, 0, j)),
        out_shape=jax.ShapeDtypeStruct((L, B, M), F32),
        compiler_params=_params(2),
        name="adaln_mod",
    )(c, w, b.reshape(L, 1, M))


def _nm_mm_body(x_ref, g_ref, sh_ref, sc_ref, w_ref, o_ref, h_sc):
    @pl.when(pl.program_id(2) == 0)
    def _():
        h_sc[...] = _norm_mod(x_ref[0], g_ref[...], sh_ref[0], sc_ref[0]).astype(BF16)

    o_ref[0] = jnp.dot(h_sc[...], w_ref[...], preferred_element_type=F32).astype(o_ref.dtype)


def _norm_mod_matmul(x, g, sh, sc, w):
    B, S, D = x.shape
    N = w.shape[1]
    tm, tn = 1024, 1024
    return pl.pallas_call(
        _nm_mm_body,
        grid=(B, S // tm, N // tn),
        in_specs=[pl.BlockSpec((1, tm, D), lambda b, i, j: (b, i, 0)),
                  pl.BlockSpec((1, D), lambda b, i, j: (0, 0)),
                  pl.BlockSpec((1, 1, D), lambda b, i, j: (b, 0, 0)),
                  pl.BlockSpec((1, 1, D), lambda b, i, j: (b, 0, 0)),
                  pl.BlockSpec((D, tn), lambda b, i, j: (0, j))],
        out_specs=pl.BlockSpec((1, tm, tn), lambda b, i, j: (b, i, j)),
        out_shape=jax.ShapeDtypeStruct((B, S, N), BF16),
        scratch_shapes=[pltpu.VMEM((tm, D), BF16)],
        compiler_params=_params(3),
        name="norm_mod_proj",
    )(x, g.reshape(1, D), sh, sc, w)


def _nm_mm_t_body(x_ref, g_ref, sh_ref, sc_ref, wt_ref, o_ref, h_sc):
    @pl.when(pl.program_id(2) == 0)
    def _():
        h_sc[...] = _norm_mod(x_ref[0], g_ref[...], sh_ref[0], sc_ref[0]).astype(BF16)

    res = lax.dot_general(wt_ref[...], h_sc[...], _NT, preferred_element_type=F32)
    for r in range(o_ref.shape[1]):
        o_ref[0, r] = res[:, r * ATT_BLOCK:(r + 1) * ATT_BLOCK].astype(o_ref.dtype)


def _norm_mod_matmul_t(x, g, sh, sc, wt):
    B, S, D = x.shape
    N = wt.shape[0]
    tm, tn = 1024, 1024
    T = ATT_BLOCK
    return pl.pallas_call(
        _nm_mm_t_body,
        grid=(B, S // tm, N // tn),
        in_specs=[pl.BlockSpec((1, tm, D), lambda b, i, j: (b, i, 0)),
                  pl.BlockSpec((1, D), lambda b, i, j: (0, 0)),
                  pl.BlockSpec((1, 1, D), lambda b, i, j: (b, 0, 0)),
                  pl.BlockSpec((1, 1, D), lambda b, i, j: (b, 0, 0)),
                  pl.BlockSpec((tn, D), lambda b, i, j: (j, 0))],
        out_specs=pl.BlockSpec((1, tm // T, tn, T), lambda b, i, j: (b, i, j, 0)),
        out_shape=jax.ShapeDtypeStruct((B, S // T, N, T), BF16),
        scratch_shapes=[pltpu.VMEM((tm, D), BF16)],
        compiler_params=_params(3),
        name="norm_mod_proj_t",
    )(x, g.reshape(1, D), sh, sc, wt)


def _mm_res_body(a_ref, w_ref, r_ref, gate_ref, o_ref):
    acc = jnp.dot(a_ref[0], w_ref[...], preferred_element_type=F32)
    o_ref[0] = r_ref[0] + gate_ref[0] * acc


def _matmul_gated_residual(a, w, res, gate):
    B, S, K = a.shape
    N = w.shape[1]
    tm, tn = 1024, 1024
    return pl.pallas_call(
        _mm_res_body,
        grid=(B, S // tm, N // tn),
        in_specs=[pl.BlockSpec((1, tm, K), lambda b, i, j: (b, i, 0)),
                  pl.BlockSpec((K, tn), lambda b, i, j: (0, j)),
                  pl.BlockSpec((1, tm, tn), lambda b, i, j: (b, i, j)),
                  pl.BlockSpec((1, 1, tn), lambda b, i, j: (b, 0, j))],
        out_specs=pl.BlockSpec((1, tm, tn), lambda b, i, j: (b, i, j)),
        out_shape=jax.ShapeDtypeStruct((B, S, N), F32),
        compiler_params=_params(3),
        name="out_proj_residual",
    )(a, w, res, gate)


def _pv_t(vt, ps, head_row):
    zeros = jnp.zeros_like(vt)
    vcat = jnp.concatenate(
        [jnp.where((head_row >= hh * HEAD_DIM) & (head_row < (hh + 1) * HEAD_DIM), vt, zeros)
         for hh in range(HEADS_PER_TILE)], axis=1)
    return jnp.dot(vcat, jnp.concatenate(ps, axis=0), preferred_element_type=F32)


def _per_head_rows(head_row, vals):
    assert HEADS_PER_TILE == 2
    return jnp.where(head_row < HEAD_DIM, vals[0], vals[1])


def _head_queries(q, lane):
    out = []
    for hh in range(HEADS_PER_TILE):
        in_head = (lane >= hh * HEAD_DIM) & (lane < (hh + 1) * HEAD_DIM)
        out.append(jnp.where(in_head, q, jnp.zeros_like(q)))
    return out


def _rel_bucket(dist):
    max_exact = N_BUCKETS // 2
    d = np.maximum(dist, 1).astype(np.float32)
    ratio = np.log(d / np.float32(max_exact)) / np.float32(math.log(MAX_DISTANCE / max_exact))
    large = max_exact + (ratio * np.float32(N_BUCKETS - max_exact)).astype(np.int32)
    large = np.minimum(large, N_BUCKETS - 1)
    return np.where(dist < max_exact, dist, large)


def _bias_tiles(rel_bias):
    T = ATT_BLOCK
    W = 2 * T
    H = rel_bias.shape[1]
    k = np.arange(W)
    lag = np.where(k <= T, k, k - W)
    dist = np.maximum(np.arange(N_BIAS_TILES)[:, None] * T + lag[None, :], 0)
    bucket = _rel_bucket(dist)
    assert (bucket[N_BIAS_TILES - 1] == N_BUCKETS - 1).all()
    onehot = np.eye(N_BUCKETS, dtype=np.float32)[bucket.reshape(-1)]
    vec = jnp.dot(jnp.asarray(onehot), rel_bias, precision=HIGHEST)
    a = vec.T.reshape(H, N_BIAS_TILES, W)
    tiled = jnp.tile(a, (1, 1, T))
    return tiled[:, :, :T * (W - 1)].reshape(H, N_BIAS_TILES, T, W - 1)[:, :, :, :T]


def _moba_body(q_ref, k_ref, vt_ref, bias_ref, o_ref, kmean_sc, sel_sc, s_sc, p_sc, *, nb):
    c = pl.program_id(2)
    T = ATT_BLOCK

    @pl.when(c == 0)
    def _():
        rest = jnp.concatenate(
            [jnp.sum(k_ref[0, n * T:(n + 1) * T, :].astype(F32), axis=0, keepdims=True) * (1.0 / T)
             for n in range(nb)], axis=0)
        for part in range(3):
            term = rest.astype(BF16)
            kmean_sc[part * nb:(part + 1) * nb, :] = term
            rest = rest - term.astype(F32)

    lane = lax.broadcasted_iota(jnp.int32, (T, LANES), 1)
    blk = lax.broadcasted_iota(jnp.int32, (nb, T), 0)
    key_i = lax.broadcasted_iota(jnp.int32, (T, T), 0)
    qry_i = lax.broadcasted_iota(jnp.int32, (T, T), 1)
    causal = key_i <= qry_i
    head_row = lax.broadcasted_iota(jnp.int32, (LANES, T), 0)
    kmean = kmean_sc[...]
    scale = HEAD_DIM ** -0.5

    qs = []
    for hh, qh in enumerate(_head_queries(q_ref[0], lane)):
        terms = lax.dot_general(kmean, qh, _NT, preferred_element_type=F32)
        gate = terms[:nb] + terms[nb:2 * nb] + terms[2 * nb:]
        gate = jnp.where(blk < c, gate, NEG)
        rank = jnp.zeros((nb, T), jnp.int32)
        for m in range(nb):
            gm = gate[m:m + 1, :]
            beats = (gm > gate) | ((gm == gate) & (blk > m))
            rank = rank + beats.astype(jnp.int32)
        sel_sc[hh] = ((rank < MOBA_TOPK) & (blk < c)).astype(F32)
        qs.append((qh.astype(F32) * scale).astype(BF16))

    def scores(n):
        kb = k_ref[0, pl.ds(pl.multiple_of(n * T, T), T), :]
        return [lax.dot_general(kb, qs[hh], _NT, preferred_element_type=F32)
                for hh in range(HEADS_PER_TILE)]

    acc_rows = LANES + 16
    acc_row = lax.broadcasted_iota(jnp.int32, (acc_rows, T), 0)
    first_head = (acc_row < HEAD_DIM) | ((acc_row >= LANES) & (acc_row < LANES + 8))
    grp = lax.broadcasted_iota(jnp.int32, (16, T), 0) // 8
    ones_rows = [(grp == hh).astype(BF16) for hh in range(HEADS_PER_TILE)]

    def soften(n, raw, ms):
        dl = jnp.minimum(c - n, N_BIAS_TILES - 1)
        new_ms, alphas = [], []
        for hh in range(HEADS_PER_TILE):
            m = ms[hh]
            picked = sel_sc[hh, pl.ds(n, 1), :] > 0.5
            s = raw[hh] + bias_ref[hh, pl.ds(dl, 1)][0]
            m_new = jnp.where(picked, jnp.maximum(m, jnp.max(s, axis=0, keepdims=True)), m)
            p_sc[hh] = jnp.exp(s - jnp.where(picked, m_new, -NEG)).astype(BF16)
            new_ms.append(m_new)
            alphas.append(jnp.exp(m - m_new))
        return new_ms, alphas

    def apply_pending(blk, alphas, acc):
        vt = vt_ref[0, blk]
        zeros = jnp.zeros_like(vt)
        lhs = jnp.concatenate(
            [jnp.concatenate([jnp.where((head_row >= hh * HEAD_DIM) & (head_row < (hh + 1) * HEAD_DIM),
                                        vt, zeros), ones_rows[hh]], axis=0)
             for hh in range(HEADS_PER_TILE)], axis=1)
        ps = jnp.concatenate([p_sc[hh] for hh in range(HEADS_PER_TILE)], axis=0)
        return (jnp.where(first_head, alphas[0], alphas[1]) * acc
                + jnp.dot(lhs, ps, preferred_element_type=F32))

    raw_own = scores(c)
    raw_next = scores(0)
    for hh in range(HEADS_PER_TILE):
        s_sc[hh] = raw_next[hh]

    ms = []
    for hh in range(HEADS_PER_TILE):
        s = jnp.where(causal, raw_own[hh] + bias_ref[hh, 0], NEG)
        m = jnp.max(s, axis=0, keepdims=True)
        p_sc[hh] = jnp.exp(s - m).astype(BF16)
        ms.append(m)
    ones = jnp.ones((1, T), F32)

    def body(t, carry):
        ms, alphas, acc = list(carry[:2]), list(carry[2:4]), carry[4]
        raw_t = [s_sc[hh] for hh in range(HEADS_PER_TILE)]
        acc = apply_pending(jnp.where(t == 0, c, t - 1), alphas, acc)
        raw_n = scores(jnp.minimum(t + 1, c - 1))
        ms, alphas = soften(t, raw_t, ms)
        for hh in range(HEADS_PER_TILE):
            s_sc[hh] = raw_n[hh]
        return tuple(ms) + tuple(alphas) + (acc,)

    n_pairs = lax.shift_right_logical(c + 1, 1)
    init = tuple(ms) + (ones, ones, jnp.zeros((acc_rows, T), F32))
    fin = lax.fori_loop(0, n_pairs, lambda i, carry: body(2 * i + 1, body(2 * i, carry)), init)
    acc = apply_pending(jnp.maximum(2 * n_pairs - 1, 0), list(fin[2:4]), fin[4])
    norm = _per_head_rows(head_row, [acc[LANES:LANES + 1], acc[LANES + 8:LANES + 9]])
    o_ref[0] = (acc[:LANES] / norm).T.astype(o_ref.dtype)


def _moba_attention(qk, vt, bias_tiles):
    B, S, D2 = qk.shape
    D = D2 // 2
    T = ATT_BLOCK
    nb = S // T
    n_hp = D // LANES
    return pl.pallas_call(
        functools.partial(_moba_body, nb=nb),
        grid=(B, n_hp, nb),
        in_specs=[pl.BlockSpec((1, T, LANES), lambda b, h, i: (b, i, h)),
                  pl.BlockSpec((1, S, LANES), lambda b, h, i: (b, 0, n_hp + h)),
                  pl.BlockSpec((1, nb, LANES, T), lambda b, h, i: (b, 0, h, 0)),
                  pl.BlockSpec((HEADS_PER_TILE, N_BIAS_TILES, T, T), lambda b, h, i: (h, 0, 0, 0))],
        out_specs=pl.BlockSpec((1, T, LANES), lambda b, h, i: (b, i, h)),
        out_shape=jax.ShapeDtypeStruct((B, S, D), BF16),
        scratch_shapes=[pltpu.VMEM((3 * nb, LANES), BF16), pltpu.VMEM((HEADS_PER_TILE, nb, T), F32),
                        pltpu.VMEM((HEADS_PER_TILE, T, T), F32), pltpu.VMEM((HEADS_PER_TILE, T, T), BF16)],
        compiler_params=_params(3),
        name="moba_attention",
    )(qk, qk, vt, bias_tiles)


SB_TILES_PER_STEP = 2


def _sb_body(q_ref, k_ref, vt_ref, o_ref):
    step = pl.program_id(2)
    T = ATT_BLOCK
    lane = lax.broadcasted_iota(jnp.int32, (T, LANES), 1)
    key_i = lax.broadcasted_iota(jnp.int32, (T, T), 0)
    qry_i = lax.broadcasted_iota(jnp.int32, (T, T), 1)
    strict = key_i < qry_i
    later = (qry_i > key_i).astype(BF16)
    later2 = jnp.concatenate([later, later], axis=1)
    head_row = lax.broadcasted_iota(jnp.int32, (LANES, T), 0)
    scale = HEAD_DIM ** -0.5
    cs = [step * SB_TILES_PER_STEP + u for u in range(SB_TILES_PER_STEP)]
    qs = [[(qh.astype(F32) * scale).astype(BF16)
           for qh in _head_queries(q_ref[0, u * T:(u + 1) * T, :], lane)] for u in range(SB_TILES_PER_STEP)]

    def run(jobs, states):
        rs = {u: list(st[:HEADS_PER_TILE]) for u, st in states.items()}
        accs = {u: st[-1] for u, st in states.items()}
        items = [(ji, hh) for ji in range(len(jobs)) for hh in range(HEADS_PER_TILE)]
        kbs = [k_ref[0, pl.ds(pl.multiple_of(n * T, T), T), :] for _, n, _, _ in jobs]
        z = {(ji, hh): lax.dot_general(kbs[ji], qs[jobs[ji][0]][hh], _NT, preferred_element_type=F32)
             for ji, hh in items}
        lb, l1, hilo = {}, {}, {}
        for t in items:
            sp = jnp.log(1.0 + jnp.exp(-jnp.abs(z[t])))
            lb[t] = jnp.minimum(z[t], 0.0) - sp
            l1[t] = lb[t] - z[t]
            if jobs[t[0]][2]:
                l1[t] = jnp.where(strict, l1[t], 0.0)
            hi = l1[t].astype(BF16)
            lo = (l1[t] - hi.astype(F32)).astype(BF16)
            hilo[t] = jnp.concatenate([hi, lo], axis=0)
        tail = {t: jnp.dot(later2, hilo[t], preferred_element_type=F32) for t in items}
        ps = {}
        for ji, hh in items:
            u, _, diag, keep = jobs[ji]
            a = jnp.exp(lb[ji, hh] + tail[ji, hh] + rs[u][hh])
            if diag:
                a = jnp.where(strict, a, 0.0)
            ps[ji, hh] = a.astype(BF16)
            inc = tail[ji, hh][0:1, :] + l1[ji, hh][0:1, :]
            if keep is not None:
                inc = jnp.where(keep, inc, 0.0)
            rs[u][hh] = rs[u][hh] + inc
        for ji, (u, n, _, keep) in enumerate(jobs):
            vt = vt_ref[0, n]
            if keep is not None:
                vt = jnp.where(keep, vt, jnp.zeros_like(vt))
            accs[u] = accs[u] + _pv_t(vt, [ps[ji, hh] for hh in range(HEADS_PER_TILE)], head_row)
        return {u: tuple(rs[u]) + (accs[u],) for u in states}

    zero = (jnp.zeros((1, T), F32),) * HEADS_PER_TILE + (jnp.zeros((LANES, T), F32),)
    jobs = []
    for u, c in enumerate(cs):
        jobs += [(u, c, True, None), (u, jnp.maximum(c - 1, 0), False, (c > 0) if u == 0 else None)]
    states = run(jobs, {u: zero for u in range(SB_TILES_PER_STEP)})

    for u, c in enumerate(cs):
        def cond(carry, c=c):
            live = jnp.max(jnp.maximum(carry[1], carry[2])) > SB_DEAD_LOG
            return jnp.logical_and(carry[0] < c, live)

        def body(carry, u=u, c=c):
            i = carry[0]
            return (i + 1,) + run([(u, c - 1 - i, False, None)], {u: carry[1:]})[u]

        fin = lax.while_loop(cond, body, (jnp.int32(1),) + states[u])
        o_ref[0, u * T:(u + 1) * T, :] = fin[-1].T.astype(o_ref.dtype)


def _sb_attention(q, k, vt):
    B, S, D = q.shape
    T = ATT_BLOCK
    nb = S // T
    n_hp = D // LANES
    tq = SB_TILES_PER_STEP * T
    return pl.pallas_call(
        _sb_body,
        grid=(B, n_hp, S // tq),
        in_specs=[pl.BlockSpec((1, tq, LANES), lambda b, h, i: (b, i, h)),
                  pl.BlockSpec((1, S, LANES), lambda b, h, i: (b, 0, h)),
                  pl.BlockSpec((1, nb, LANES, T), lambda b, h, i: (b, 0, h, 0))],
        out_specs=pl.BlockSpec((1, tq, LANES), lambda b, h, i: (b, i, h)),
        out_shape=jax.ShapeDtypeStruct((B, S, D), BF16),
        compiler_params=_params(3),
        name="stick_breaking_attention",
    )(q, k, vt)


def _swiglu_chunk(xb, wg, wu, w2):
    g = jnp.dot(xb, wg, preferred_element_type=F32)
    u = jnp.dot(xb, wu, preferred_element_type=F32)
    a = (g * jax.nn.sigmoid(g) * u).astype(BF16)
    return jnp.dot(a, w2, preferred_element_type=F32)


def _dense_ffn_body(x_ref, g_ref, sh_ref, sc_ref, gate_ref, wg_ref, wu_ref, w2_ref, o_ref,
                    xb_sc, acc_sc, *, n_chunks):
    j = pl.program_id(1)

    @pl.when(j == 0)
    def _():
        xb_sc[...] = _norm_mod(x_ref[...], g_ref[...], sh_ref[0], sc_ref[0]).astype(BF16)

    y = _swiglu_chunk(xb_sc[...], wg_ref[0], wu_ref[0], w2_ref[0])

    @pl.when(j == 0)
    def _():
        acc_sc[...] = y

    @pl.when(j > 0)
    def _():
        acc_sc[...] += y

    @pl.when(j == n_chunks - 1)
    def _():
        o_ref[...] = x_ref[...] + gate_ref[0] * acc_sc[...]


def _dense_ffn(x, g, sh, sc, gate, w13, w2):
    B, S, D = x.shape
    Fh = w2.shape[1]
    tm, tf = 512, 1408
    n_chunks = Fh // tf
    per_b = S // tm
    xf = x.reshape(B * S, D)
    out = pl.pallas_call(
        functools.partial(_dense_ffn_body, n_chunks=n_chunks),
        grid=(B * S // tm, n_chunks),
        in_specs=[pl.BlockSpec((tm, D), lambda i, j: (i, 0)),
                  pl.BlockSpec((1, D), lambda i, j: (0, 0)),
                  pl.BlockSpec((1, 1, D), lambda i, j: (i // per_b, 0, 0)),
                  pl.BlockSpec((1, 1, D), lambda i, j: (i // per_b, 0, 0)),
                  pl.BlockSpec((1, 1, D), lambda i, j: (i // per_b, 0, 0)),
                  pl.BlockSpec((1, D, tf), lambda i, j: (0, 0, j)),
                  pl.BlockSpec((1, D, tf), lambda i, j: (0, 0, j + n_chunks)),
                  pl.BlockSpec((1, tf, D), lambda i, j: (0, j, 0))],
        out_specs=pl.BlockSpec((tm, D), lambda i, j: (i, 0)),
        out_shape=jax.ShapeDtypeStruct((B * S, D), F32),
        scratch_shapes=[pltpu.VMEM((tm, D), BF16), pltpu.VMEM((tm, D), F32)],
        compiler_params=_params(2),
        name="dense_swiglu",
    )(xf, g.reshape(1, D), sh, sc, gate, w13, w13, w2)
    return out.reshape(B, S, D)


def _router_body(x_ref, g_ref, sh_ref, sc_ref, rwt_ref, h_ref, ei_ref, w_ref):
    h = _norm_mod(x_ref[0], g_ref[...], sh_ref[0], sc_ref[0])
    h_ref[...] = h
    logits = lax.dot_general(rwt_ref[...], h, _NT, preferred_element_type=F32,
                             precision=HIGHEST)
    n_e = logits.shape[0]
    eio = lax.broadcasted_iota(jnp.int32, logits.shape, 0)
    m1 = jnp.max(logits, axis=0, keepdims=True)
    i1 = jnp.min(jnp.where(logits == m1, eio, n_e), axis=0, keepdims=True)
    rest = jnp.where(eio == i1, -jnp.inf, logits)
    m2 = jnp.max(rest, axis=0, keepdims=True)
    i2 = jnp.min(jnp.where(rest == m2, eio, n_e), axis=0, keepdims=True)
    t = jnp.exp(m2 - m1)
    den = 1.0 + t
    ei_ref[...] = jnp.concatenate([i1, i2], axis=0)
    w_ref[...] = jnp.concatenate([1.0 / den, t / den], axis=0)


def _router(x, g, sh, sc, router_w):
    B, S, D = x.shape
    E = router_w.shape[1]
    tm = 512
    per_b = S // tm
    return pl.pallas_call(
        _router_body,
        grid=(B, per_b),
        in_specs=[pl.BlockSpec((1, tm, D), lambda b, i: (b, i, 0)),
                  pl.BlockSpec((1, D), lambda b, i: (0, 0)),
                  pl.BlockSpec((1, 1, D), lambda b, i: (b, 0, 0)),
                  pl.BlockSpec((1, 1, D), lambda b, i: (b, 0, 0)),
                  pl.BlockSpec((E, D), lambda b, i: (0, 0))],
        out_specs=[pl.BlockSpec((tm, D), lambda b, i: (b * per_b + i, 0)),
                   pl.BlockSpec((TOP_K_EXPERTS, tm), lambda b, i: (0, b * per_b + i)),
                   pl.BlockSpec((TOP_K_EXPERTS, tm), lambda b, i: (0, b * per_b + i))],
        out_shape=[jax.ShapeDtypeStruct((B * S, D), F32),
                   jax.ShapeDtypeStruct((TOP_K_EXPERTS, B * S), jnp.int32),
                   jax.ShapeDtypeStruct((TOP_K_EXPERTS, B * S), F32)],
        compiler_params=_params(2),
        name="moe_router",
    )(x, g.reshape(1, D), sh, sc, router_w.T)


def _route_plan(ei, tm, n_tiles):
    n_tok = ei.shape[1]
    e_flat = ei.reshape(-1)
    onehot = (e_flat[:, None] == jnp.arange(N_EXPERTS, dtype=jnp.int32)[None, :]).astype(jnp.int32)
    csum = jnp.cumsum(onehot, axis=0)
    rank = jnp.take_along_axis(csum, e_flat[:, None], axis=1)[:, 0] - 1
    counts = csum[-1]
    padded = ((counts + tm - 1) // tm) * tm
    ends = jnp.cumsum(padded)
    offs = ends - padded
    pos = (offs[e_flat] + rank).astype(jnp.int32)
    spare = TOP_K_EXPERTS * n_tok + jnp.arange((n_tiles + 1) * tm, dtype=jnp.int32) % tm
    flat = spare.at[tm + pos].set(jnp.arange(TOP_K_EXPERTS * n_tok, dtype=jnp.int32))
    starts = jnp.arange(n_tiles, dtype=jnp.int32) * tm
    t_exp = jnp.minimum(jnp.searchsorted(ends, starts, side="right"), N_EXPERTS - 1).astype(jnp.int32)
    t_rows = jnp.clip(offs[t_exp] + counts[t_exp] - starts, 0, tm)
    t_rows = jnp.where(starts < ends[-1], t_rows, 0).astype(jnp.int32)
    return flat, t_exp, t_rows


def _moe_ffn_body(te_ref, tr_ref, flat_ref, h_hbm, wg_ref, wu_ref, w2_ref, ys_hbm,
                  xstage, xb_sc, acc_sc, ybuf, gsem, ssem, *, tm, n_chunks, n_tiles, n_tok):
    i = pl.program_id(0)
    j = pl.program_id(1)
    valid = tr_ref[i] > 0
    rows_per_step = tm // n_chunks

    def gather_row(tile, q, rr):
        tok = flat_ref[(tile + 1) * tm + q * rows_per_step + rr] & (n_tok - 1)
        return pltpu.make_async_copy(h_hbm.at[pl.ds(tok, 1)], xstage.at[q, pl.ds(rr, 1)], gsem)

    def scatter_row(tile, q, rr):
        dst = flat_ref[(tile + 1) * tm + q * rows_per_step + rr]
        return pltpu.make_async_copy(ybuf.at[q, pl.ds(rr, 1)], ys_hbm.at[pl.ds(dst, 1)], ssem)

    def gather_wait(steps):
        for q in range(steps):
            pltpu.make_async_copy(h_hbm.at[pl.ds(0, rows_per_step)], xstage.at[q], gsem).wait()

    def scatter_wait(steps):
        for q in range(steps):
            pltpu.make_async_copy(ybuf.at[q], ys_hbm.at[pl.ds(0, rows_per_step)], ssem).wait()

    def all_rows(start_row):
        def one(q, _):
            for rr in range(rows_per_step):
                start_row(q, rr).start()
            return 0
        lax.fori_loop(0, n_chunks, one, 0)

    @pl.when(valid)
    def _():
        @pl.when(j == 0)
        def _():
            @pl.when(i == 0)
            def _():
                all_rows(functools.partial(gather_row, 0))
                gather_wait(n_chunks)
                ybuf[...] = jnp.zeros_like(ybuf)

            xb_sc[...] = xstage[...].reshape(tm, -1).astype(BF16)

        for rr in range(rows_per_step):
            gather_row(i + 1, j, rr).start()
            scatter_row(i - 1, j, rr).start()

        xb = xb_sc[...]
        g = jnp.dot(xb, wg_ref[0].astype(BF16), preferred_element_type=F32)
        u = jnp.dot(xb, wu_ref[0].astype(BF16), preferred_element_type=F32)
        a = (g * jax.nn.sigmoid(g) * u).astype(BF16)
        gather_wait(1)
        scatter_wait(1)
        y = jnp.dot(a, w2_ref[0].astype(BF16), preferred_element_type=F32)

        @pl.when(j == 0)
        def _():
            acc_sc[...] = y

        @pl.when(jnp.logical_and(j > 0, j < n_chunks - 1))
        def _():
            acc_sc[...] += y

        @pl.when(j == n_chunks - 1)
        def _():
            ybuf[...] = (acc_sc[...] + y).reshape(ybuf.shape)

    prv = jnp.maximum(i - 1, 0)

    @pl.when(jnp.logical_and(jnp.logical_and(jnp.logical_not(valid), j == 0),
                             jnp.logical_and(i > 0, tr_ref[prv] > 0)))
    def _():
        all_rows(functools.partial(scatter_row, prv))
        scatter_wait(n_chunks)


def _moe_ffn(h, flat, t_exp, t_rows, w13, w2, tm):
    n_tok, D = h.shape
    Fh = w2.shape[1]
    n_tiles = t_exp.shape[0]
    n_chunks = 4
    tf = Fh // n_chunks
    assert tf % LANES == 0 and tm % n_chunks == 0 and n_tok & (n_tok - 1) == 0 and n_tok >= tm

    def wmap(col_off):
        return lambda i, j, te, tr, fl: (te[i], 0, j * jnp.minimum(tr[i], 1) + col_off)

    return pl.pallas_call(
        functools.partial(_moe_ffn_body, tm=tm, n_chunks=n_chunks, n_tiles=n_tiles, n_tok=n_tok),
        grid_spec=pltpu.PrefetchScalarGridSpec(
            num_scalar_prefetch=3,
            grid=(n_tiles, n_chunks),
            in_specs=[pl.BlockSpec(memory_space=pl.ANY),
                      pl.BlockSpec((1, D, tf), wmap(0)),
                      pl.BlockSpec((1, D, tf), wmap(n_chunks)),
                      pl.BlockSpec((1, tf, D), lambda i, j, te, tr, fl: (te[i], j * jnp.minimum(tr[i], 1), 0))],
            out_specs=pl.BlockSpec(memory_space=pl.ANY),
            scratch_shapes=[pltpu.VMEM((n_chunks, tm // n_chunks, D), F32), pltpu.VMEM((tm, D), BF16),
                            pltpu.VMEM((tm, D), F32), pltpu.VMEM((n_chunks, tm // n_chunks, D), F32),
                            pltpu.SemaphoreType.DMA, pltpu.SemaphoreType.DMA]),
        out_shape=jax.ShapeDtypeStruct((TOP_K_EXPERTS * n_tok + tm, D), F32),
        compiler_params=_params(2),
        name="moe_expert_swiglu",
    )(t_exp, t_rows, flat, h, w13, w13, w2)


def _combine_body(x_ref, y0_ref, y1_ref, w_ref, gate_ref, fg_ref, o_ref, *, final_norm):
    w = w_ref[...]
    y = w[:, 0:1] * y0_ref[...] + w[:, 1:2] * y1_ref[...]
    out = x_ref[...] + gate_ref[0] * y
    if final_norm:
        ms = jnp.mean(out * out, axis=-1, keepdims=True)
        out = (out * lax.rsqrt(ms + EPS)) * fg_ref[...]
    o_ref[...] = out


def _moe_combine(x, ys, wts, gate, final_g, final_norm):
    B, S, D = x.shape
    n_tok = B * S
    rows = 512
    per_b = S // rows
    out = pl.pallas_call(
        functools.partial(_combine_body, final_norm=final_norm),
        grid=(n_tok // rows,),
        in_specs=[pl.BlockSpec((rows, D), lambda i: (i, 0)),
                  pl.BlockSpec((rows, D), lambda i: (i, 0)),
                  pl.BlockSpec((rows, D), lambda i: (n_tok // rows + i, 0)),
                  pl.BlockSpec((rows, TOP_K_EXPERTS), lambda i: (i, 0)),
                  pl.BlockSpec((1, 1, D), lambda i: (i // per_b, 0, 0)),
                  pl.BlockSpec((1, D), lambda i: (0, 0))],
        out_specs=pl.BlockSpec((rows, D), lambda i: (i, 0)),
        out_shape=jax.ShapeDtypeStruct((n_tok, D), F32),
        compiler_params=_params(1),
        name="moe_combine",
    )(x.reshape(n_tok, D), ys, ys, wts, gate, final_g.reshape(1, D))
    return out.reshape(B, S, D)


def _moe_layer(x, g, sh, sc, gate, router_w, w13, w2, first_expert, final_g, final_norm):
    B, S, D = x.shape
    n_tok = B * S
    tm = 1024
    n_tiles = TOP_K_EXPERTS * n_tok // tm + N_EXPERTS
    h, ei, wts = _router(x, g, sh, sc, router_w)
    flat, t_exp, t_rows = _route_plan(ei, tm, n_tiles)
    ys = _moe_ffn(h, flat, t_exp + first_expert, t_rows, w13, w2, tm)
    return _moe_combine(x, ys, wts.T, gate, final_g, final_norm)


def kernel(x, c, rel_bias, mod_w, mod_b, norm_mix_g, norm_ffn_g, a_wqkv, a_wo, kv_norm_g, kv_mod_w,
           kv_mod_b, b_wkv, b_wq, b_wo, ffn_w13, ffn_w2, router_w, moe_w13, moe_w2, final_norm_g):
    D = D_MODEL
    mods = _modulation(c, mod_w, mod_b)
    kv_mod = _modulation(c, kv_mod_w[None], kv_mod_b[None])[0]
    bias_tiles = _bias_tiles(rel_bias)
    k_sh = vt_sh = None
    for l in range(DEPTH):
        sh_a, sc_a, g_a, sh_f, sc_f, g_f = [mods[l][:, None, i * D:(i + 1) * D] for i in range(6)]
        if l < N_A_LAYERS:
            w = a_wqkv[l].astype(BF16)
            qk = _norm_mod_matmul(x, norm_mix_g[l], sh_a, sc_a, w[:, :2 * D])
            vt = _norm_mod_matmul_t(x, norm_mix_g[l], sh_a, sc_a, w[:, 2 * D:].T)
            attn = _moba_attention(qk, vt, bias_tiles)
            x = _matmul_gated_residual(attn, a_wo[l].astype(BF16), x, g_a)
        else:
            j = l - N_A_LAYERS
            if j == 0:
                w = b_wkv.astype(BF16)
                kv_sh, kv_sc = kv_mod[:, None, :D], kv_mod[:, None, D:]
                k_sh = _norm_mod_matmul(x, kv_norm_g, kv_sh, kv_sc, w[:, :D])
                vt_sh = _norm_mod_matmul_t(x, kv_norm_g, kv_sh, kv_sc, w[:, D:].T)
            q = _norm_mod_matmul(x, norm_mix_g[l], sh_a, sc_a, b_wq[j].astype(BF16))
            attn = _sb_attention(q, k_sh, vt_sh)
            x = _matmul_gated_residual(attn, b_wo[j].astype(BF16), x, g_a)
        if l % 2 == 0:
            x = _dense_ffn(x, norm_ffn_g[l], sh_f, sc_f, g_f,
                           ffn_w13[l // 2][None].astype(BF16), ffn_w2[l // 2][None].astype(BF16))
        else:
            x = _moe_layer(x, norm_ffn_g[l], sh_f, sc_f, g_f, router_w[l // 2],
                           moe_w13.reshape((-1,) + moe_w13.shape[2:]),
                           moe_w2.reshape((-1,) + moe_w2.shape[2:]), (l // 2) * N_EXPERTS,
                           final_norm_g, l == DEPTH - 1)
    return x
```

```python
import functools
import math

import numpy as np
import jax
import jax.numpy as jnp
from jax import lax
from jax.experimental import pallas as pl
from jax.experimental.pallas import tpu as pltpu

D_MODEL = 1024
DEPTH = 4
N_HEADS = 16
HEAD_DIM = D_MODEL // N_HEADS
N_A_LAYERS = DEPTH // 2
MOBA_BLOCK = 256
MOBA_TOPK = 3
N_BUCKETS = 32
MAX_DISTANCE = 1024
N_EXPERTS = 8
TOP_K_EXPERTS = 2
EPS = 1e-6
NEG = -1e30

LANES = 128
HEADS_PER_TILE = LANES // HEAD_DIM
ATT_BLOCK = 256
N_BIAS_TILES = 6
SB_DEAD_LOG = -105.0
VMEM_LIMIT = 56 * 1024 * 1024

F32 = jnp.float32
BF16 = jnp.bfloat16
HIGHEST = lax.Precision.HIGHEST
_NT = (((1,), (1,)), ((), ()))


def _params(n_axes):
    return pltpu.CompilerParams(dimension_semantics=("arbitrary",) * n_axes,
                                vmem_limit_bytes=VMEM_LIMIT)


def _norm_mod(x, g, sh, sc):
    ms = jnp.mean(x * x, axis=-1, keepdims=True)
    y = (x * lax.rsqrt(ms + EPS)) * g
    return y * (1.0 + sc) + sh


def _mod_body(c_ref, w_ref, b_ref, o_ref):
    c = c_ref[...]
    s = c * jax.nn.sigmoid(c)
    o_ref[0] = jnp.dot(s, w_ref[0], preferred_element_type=F32, precision=HIGHEST) + b_ref[0]


def _modulation(c, w, b):
    L, D, M = w.shape
    B = c.shape[0]
    tn = 1024
    return pl.pallas_call(
        _mod_body,
        grid=(L, M // tn),
        in_specs=[pl.BlockSpec((B, D), lambda l, j: (0, 0)),
                  pl.BlockSpec((1, D, tn), lambda l, j: (l, 0, j)),
                  pl.BlockSpec((1, 1, tn), lambda l, j: (l, 0, j))],
        out_specs=pl.BlockSpec((1, B, tn), lambda l, j: (l, 0, j)),
        out_shape=jax.ShapeDtypeStruct((L, B, M), F32),
        compiler_params=_params(2),
        name="adaln_mod",
    )(c, w, b.reshape(L, 1, M))


def _nm_mm_body(x_ref, g_ref, sh_ref, sc_ref, w_ref, o_ref, h_sc):
    @pl.when(pl.program_id(2) == 0)
    def _():
        h_sc[...] = _norm_mod(x_ref[0], g_ref[...], sh_ref[0], sc_ref[0]).astype(BF16)

    o_ref[0] = jnp.dot(h_sc[...], w_ref[...], preferred_element_type=F32).astype(o_ref.dtype)


def _norm_mod_matmul(x, g, sh, sc, w):
    B, S, D = x.shape
    N = w.shape[1]
    tm, tn = 1024, 1024
    return pl.pallas_call(
        _nm_mm_body,
        grid=(B, S // tm, N // tn),
        in_specs=[pl.BlockSpec((1, tm, D), lambda b, i, j: (b, i, 0)),
                  pl.BlockSpec((1, D), lambda b, i, j: (0, 0)),
                  pl.BlockSpec((1, 1, D), lambda b, i, j: (b, 0, 0)),
                  pl.BlockSpec((1, 1, D), lambda b, i, j: (b, 0, 0)),
                  pl.BlockSpec((D, tn), lambda b, i, j: (0, j))],
        out_specs=pl.BlockSpec((1, tm, tn), lambda b, i, j: (b, i, j)),
        out_shape=jax.ShapeDtypeStruct((B, S, N), BF16),
        scratch_shapes=[pltpu.VMEM((tm, D), BF16)],
        compiler_params=_params(3),
        name="norm_mod_proj",
    )(x, g.reshape(1, D), sh, sc, w)


def _nm_mm_t_body(x_ref, g_ref, sh_ref, sc_ref, wt_ref, o_ref, h_sc):
    @pl.when(pl.program_id(2) == 0)
    def _():
        h_sc[...] = _norm_mod(x_ref[0], g_ref[...], sh_ref[0], sc_ref[0]).astype(BF16)

    res = lax.dot_general(wt_ref[...], h_sc[...], _NT, preferred_element_type=F32)
    for r in range(o_ref.shape[1]):
        o_ref[0, r] = res[:, r * ATT_BLOCK:(r + 1) * ATT_BLOCK].astype(o_ref.dtype)


def _norm_mod_matmul_t(x, g, sh, sc, wt):
    B, S, D = x.shape
    N = wt.shape[0]
    tm, tn = 1024, 1024
    T = ATT_BLOCK
    return pl.pallas_call(
        _nm_mm_t_body,
        grid=(B, S // tm, N // tn),
        in_specs=[pl.BlockSpec((1, tm, D), lambda b, i, j: (b, i, 0)),
                  pl.BlockSpec((1, D), lambda b, i, j: (0, 0)),
                  pl.BlockSpec((1, 1, D), lambda b, i, j: (b, 0, 0)),
                  pl.BlockSpec((1, 1, D), lambda b, i, j: (b, 0, 0)),
                  pl.BlockSpec((tn, D), lambda b, i, j: (j, 0))],
        out_specs=pl.BlockSpec((1, tm // T, tn, T), lambda b, i, j: (b, i, j, 0)),
        out_shape=jax.ShapeDtypeStruct((B, S // T, N, T), BF16),
        scratch_shapes=[pltpu.VMEM((tm, D), BF16)],
        compiler_params=_params(3),
        name="norm_mod_proj_t",
    )(x, g.reshape(1, D), sh, sc, wt)


def _mm_res_body(a_ref, w_ref, r_ref, gate_ref, o_ref):
    acc = jnp.dot(a_ref[0], w_ref[...], preferred_element_type=F32)
    o_ref[0] = r_ref[0] + gate_ref[0] * acc


def _matmul_gated_residual(a, w, res, gate):
    B, S, K = a.shape
    N = w.shape[1]
    tm, tn = 1024, 1024
    return pl.pallas_call(
        _mm_res_body,
        grid=(B, S // tm, N // tn),
        in_specs=[pl.BlockSpec((1, tm, K), lambda b, i, j: (b, i, 0)),
                  pl.BlockSpec((K, tn), lambda b, i, j: (0, j)),
                  pl.BlockSpec((1, tm, tn), lambda b, i, j: (b, i, j)),
                  pl.BlockSpec((1, 1, tn), lambda b, i, j: (b, 0, j))],
        out_specs=pl.BlockSpec((1, tm, tn), lambda b, i, j: (b, i, j)),
        out_shape=jax.ShapeDtypeStruct((B, S, N), F32),
        compiler_params=_params(3),
        name="out_proj_residual",
    )(a, w, res, gate)


def _pv_t(vt, ps, head_row):
    zeros = jnp.zeros_like(vt)
    vcat = jnp.concatenate(
        [jnp.where((head_row >= hh * HEAD_DIM) & (head_row < (hh + 1) * HEAD_DIM), vt, zeros)
         for hh in range(HEADS_PER_TILE)], axis=1)
    return jnp.dot(vcat, jnp.concatenate(ps, axis=0), preferred_element_type=F32)


def _per_head_rows(head_row, vals):
    assert HEADS_PER_TILE == 2
    return jnp.where(head_row < HEAD_DIM, vals[0], vals[1])


def _head_queries(q, lane):
    out = []
    for hh in range(HEADS_PER_TILE):
        in_head = (lane >= hh * HEAD_DIM) & (lane < (hh + 1) * HEAD_DIM)
        out.append(jnp.where(in_head, q, jnp.zeros_like(q)))
    return out


def _rel_bucket(dist):
    max_exact = N_BUCKETS // 2
    d = np.maximum(dist, 1).astype(np.float32)
    ratio = np.log(d / np.float32(max_exact)) / np.float32(math.log(MAX_DISTANCE / max_exact))
    large = max_exact + (ratio * np.float32(N_BUCKETS - max_exact)).astype(np.int32)
    large = np.minimum(large, N_BUCKETS - 1)
    return np.where(dist < max_exact, dist, large)


def _bias_tiles(rel_bias):
    T = ATT_BLOCK
    W = 2 * T
    H = rel_bias.shape[1]
    k = np.arange(W)
    lag = np.where(k <= T, k, k - W)
    dist = np.maximum(np.arange(N_BIAS_TILES)[:, None] * T + lag[None, :], 0)
    bucket = _rel_bucket(dist)
    assert (bucket[N_BIAS_TILES - 1] == N_BUCKETS - 1).all()
    onehot = np.eye(N_BUCKETS, dtype=np.float32)[bucket.reshape(-1)]
    vec = jnp.dot(jnp.asarray(onehot), rel_bias, precision=HIGHEST)
    a = vec.T.reshape(H, N_BIAS_TILES, W)
    tiled = jnp.tile(a, (1, 1, T))
    return tiled[:, :, :T * (W - 1)].reshape(H, N_BIAS_TILES, T, W - 1)[:, :, :, :T]


MOBA_TILES_PER_STEP = 2


def _moba_body(q_ref, k_ref, vt_ref, bias_ref, o_ref, kmean_sc, sel_sc, s_sc, p_sc, *, nb):
    step = pl.program_id(2)
    T = ATT_BLOCK
    tiles = range(MOBA_TILES_PER_STEP)
    heads = range(HEADS_PER_TILE)
    assert MOBA_TILES_PER_STEP == 2
    c0 = step * MOBA_TILES_PER_STEP
    cs = [c0 + u for u in tiles]

    @pl.when(step == 0)
    def _():
        rest = jnp.concatenate(
            [jnp.sum(k_ref[0, n * T:(n + 1) * T, :].astype(F32), axis=0, keepdims=True) * (1.0 / T)
             for n in range(nb)], axis=0)
        for part in range(3):
            term = rest.astype(BF16)
            kmean_sc[part * nb:(part + 1) * nb, :] = term
            rest = rest - term.astype(F32)

    lane = lax.broadcasted_iota(jnp.int32, (T, LANES), 1)
    blk = lax.broadcasted_iota(jnp.int32, (nb, T), 0)
    key_i = lax.broadcasted_iota(jnp.int32, (T, T), 0)
    qry_i = lax.broadcasted_iota(jnp.int32, (T, T), 1)
    causal = key_i <= qry_i
    head_row = lax.broadcasted_iota(jnp.int32, (LANES, T), 0)
    kmean = kmean_sc[...]
    scale = HEAD_DIM ** -0.5

    qs = []
    for u in tiles:
        qs.append([])
        for hh, qh in enumerate(_head_queries(q_ref[0, u * T:(u + 1) * T, :], lane)):
            terms = lax.dot_general(kmean, qh, _NT, preferred_element_type=F32)
            gate = terms[:nb] + terms[nb:2 * nb] + terms[2 * nb:]
            gate = jnp.where(blk < cs[u], gate, NEG)
            rank = jnp.zeros((nb, T), jnp.int32)
            for m in range(nb):
                gm = gate[m:m + 1, :]
                beats = (gm > gate) | ((gm == gate) & (blk > m))
                rank = rank + beats.astype(jnp.int32)
            sel_sc[u, hh] = ((rank < MOBA_TOPK) & (blk < cs[u])).astype(F32)
            qs[u].append((qh.astype(F32) * scale).astype(BF16))

    def scores(n, which=tiles):
        kb = k_ref[0, pl.ds(pl.multiple_of(n * T, T), T), :]
        return {u: [lax.dot_general(kb, qs[u][hh], _NT, preferred_element_type=F32) for hh in heads]
                for u in which}

    acc_rows = LANES + 16
    acc_row = lax.broadcasted_iota(jnp.int32, (acc_rows, T), 0)
    first_head = (acc_row < HEAD_DIM) | ((acc_row >= LANES) & (acc_row < LANES + 8))
    grp = lax.broadcasted_iota(jnp.int32, (16, T), 0) // 8
    ones_rows = [(grp == hh).astype(BF16) for hh in heads]

    def soften(u, n, raw, ms):
        dl = jnp.minimum(cs[u] - n, N_BIAS_TILES - 1)
        new_ms, alphas = [], []
        for hh in heads:
            m = ms[hh]
            picked = sel_sc[u, hh, pl.ds(n, 1), :] > 0.5
            s = raw[hh] + bias_ref[hh, pl.ds(dl, 1)][0]
            m_new = jnp.where(picked, jnp.maximum(m, jnp.max(s, axis=0, keepdims=True)), m)
            p_sc[u, hh] = jnp.exp(s - jnp.where(picked, m_new, -NEG)).astype(BF16)
            new_ms.append(m_new)
            alphas.append(jnp.exp(m - m_new))
        return new_ms, alphas

    def value_rows(blk_idx):
        vt = vt_ref[0, blk_idx]
        zeros = jnp.zeros_like(vt)
        return jnp.concatenate(
            [jnp.concatenate([jnp.where((head_row >= hh * HEAD_DIM) & (head_row < (hh + 1) * HEAD_DIM),
                                        vt, zeros), ones_rows[hh]], axis=0) for hh in heads],
            axis=1)

    def apply_pending(u, lhs, alphas, acc):
        ps = jnp.concatenate([p_sc[u, hh] for hh in heads], axis=0)
        return (jnp.where(first_head, alphas[0], alphas[1]) * acc
                + jnp.dot(lhs, ps, preferred_element_type=F32))

    def own_block(u, raw):
        ms = []
        for hh in heads:
            s = jnp.where(causal, raw[hh] + bias_ref[hh, 0], NEG)
            m = jnp.max(s, axis=0, keepdims=True)
            p_sc[u, hh] = jnp.exp(s - m).astype(BF16)
            ms.append(m)
        return ms

    raw_own = {u: scores(cs[u], [u])[u] for u in tiles}
    raw_extra = scores(c0, [1])[1]
    raw_next = scores(0)
    for u in tiles:
        for hh in heads:
            s_sc[u, hh] = raw_next[u][hh]

    ones = jnp.ones((1, T), F32)
    zero_acc = jnp.zeros((acc_rows, T), F32)
    ms = {u: own_block(u, raw_own[u]) for u in tiles}
    alphas = {0: [ones, ones]}
    accs = {0: zero_acc}
    accs[1] = apply_pending(1, value_rows(cs[1]), [ones, ones], zero_acc)
    ms[1], alphas[1] = soften(1, c0, raw_extra, ms[1])

    def pack(ms, alphas, accs):
        return tuple(ms[0]) + tuple(ms[1]) + tuple(alphas[0]) + tuple(alphas[1]) + (accs[0], accs[1])

    def unpack(carry):
        return ({0: list(carry[0:2]), 1: list(carry[2:4])}, {0: list(carry[4:6]), 1: list(carry[6:8])},
                {0: carry[8], 1: carry[9]})

    def body(t, carry):
        ms, alphas, accs = unpack(carry)
        raw_t = {u: [s_sc[u, hh] for hh in heads] for u in tiles}
        lhs = value_rows(jnp.where(t == 0, c0, t - 1))
        for u in tiles:
            accs[u] = apply_pending(u, lhs, alphas[u], accs[u])
        raw_n = scores(jnp.minimum(t + 1, c0 - 1))
        for u in tiles:
            ms[u], alphas[u] = soften(u, t, raw_t[u], ms[u])
        for u in tiles:
            for hh in heads:
                s_sc[u, hh] = raw_n[u][hh]
        return pack(ms, alphas, accs)

    ms, alphas, accs = unpack(lax.fori_loop(
        0, step, lambda i, carry: body(2 * i + 1, body(2 * i, carry)), pack(ms, alphas, accs)))
    lhs = value_rows(jnp.maximum(c0 - 1, 0))
    for u in tiles:
        acc = apply_pending(u, lhs, alphas[u], accs[u])
        norm = _per_head_rows(head_row, [acc[LANES:LANES + 1], acc[LANES + 8:LANES + 9]])
        o_ref[0, u * T:(u + 1) * T, :] = (acc[:LANES] / norm).T.astype(o_ref.dtype)


def _moba_attention(qk, vt, bias_tiles):
    B, S, D2 = qk.shape
    D = D2 // 2
    T = ATT_BLOCK
    nb = S // T
    n_hp = D // LANES
    tq = MOBA_TILES_PER_STEP * T
    return pl.pallas_call(
        functools.partial(_moba_body, nb=nb),
        grid=(B, n_hp, S // tq),
        in_specs=[pl.BlockSpec((1, tq, LANES), lambda b, h, i: (b, i, h)),
                  pl.BlockSpec((1, S, LANES), lambda b, h, i: (b, 0, n_hp + h)),
                  pl.BlockSpec((1, nb, LANES, T), lambda b, h, i: (b, 0, h, 0)),
                  pl.BlockSpec((HEADS_PER_TILE, N_BIAS_TILES, T, T), lambda b, h, i: (h, 0, 0, 0))],
        out_specs=pl.BlockSpec((1, tq, LANES), lambda b, h, i: (b, i, h)),
        out_shape=jax.ShapeDtypeStruct((B, S, D), BF16),
        scratch_shapes=[pltpu.VMEM((3 * nb, LANES), BF16),
                        pltpu.VMEM((MOBA_TILES_PER_STEP, HEADS_PER_TILE, nb, T), F32),
                        pltpu.VMEM((MOBA_TILES_PER_STEP, HEADS_PER_TILE, T, T), F32),
                        pltpu.VMEM((MOBA_TILES_PER_STEP, HEADS_PER_TILE, T, T), BF16)],
        compiler_params=_params(3),
        name="moba_attention",
    )(qk, qk, vt, bias_tiles)


SB_TILES_PER_STEP = 2


def _sb_body(q_ref, k_ref, vt_ref, o_ref):
    step = pl.program_id(2)
    T = ATT_BLOCK
    lane = lax.broadcasted_iota(jnp.int32, (T, LANES), 1)
    key_i = lax.broadcasted_iota(jnp.int32, (T, T), 0)
    qry_i = lax.broadcasted_iota(jnp.int32, (T, T), 1)
    strict = key_i < qry_i
    later = (qry_i > key_i).astype(BF16)
    later2 = jnp.concatenate([later, later], axis=1)
    head_row = lax.broadcasted_iota(jnp.int32, (LANES, T), 0)
    scale = HEAD_DIM ** -0.5
    cs = [step * SB_TILES_PER_STEP + u for u in range(SB_TILES_PER_STEP)]
    qs = [[(qh.astype(F32) * scale).astype(BF16)
           for qh in _head_queries(q_ref[0, u * T:(u + 1) * T, :], lane)] for u in range(SB_TILES_PER_STEP)]

    def run(jobs, states):
        rs = {u: list(st[:HEADS_PER_TILE]) for u, st in states.items()}
        accs = {u: st[-1] for u, st in states.items()}
        items = [(ji, hh) for ji in range(len(jobs)) for hh in range(HEADS_PER_TILE)]
        kbs = [k_ref[0, pl.ds(pl.multiple_of(n * T, T), T), :] for _, n, _, _ in jobs]
        z = {(ji, hh): lax.dot_general(kbs[ji], qs[jobs[ji][0]][hh], _NT, preferred_element_type=F32)
             for ji, hh in items}
        lb, l1, hilo = {}, {}, {}
        for t in items:
            sp = jnp.log(1.0 + jnp.exp(-jnp.abs(z[t])))
            lb[t] = jnp.minimum(z[t], 0.0) - sp
            l1[t] = lb[t] - z[t]
            if jobs[t[0]][2]:
                l1[t] = jnp.where(strict, l1[t], 0.0)
            hi = l1[t].astype(BF16)
            lo = (l1[t] - hi.astype(F32)).astype(BF16)
            hilo[t] = jnp.concatenate([hi, lo], axis=0)
        tail = {t: jnp.dot(later2, hilo[t], preferred_element_type=F32) for t in items}
        ps = {}
        for ji, hh in items:
            u, _, diag, keep = jobs[ji]
            a = jnp.exp(lb[ji, hh] + tail[ji, hh] + rs[u][hh])
            if diag:
                a = jnp.where(strict, a, 0.0)
            ps[ji, hh] = a.astype(BF16)
            inc = tail[ji, hh][0:1, :] + l1[ji, hh][0:1, :]
            if keep is not None:
                inc = jnp.where(keep, inc, 0.0)
            rs[u][hh] = rs[u][hh] + inc
        for ji, (u, n, _, keep) in enumerate(jobs):
            vt = vt_ref[0, n]
            if keep is not None:
                vt = jnp.where(keep, vt, jnp.zeros_like(vt))
            accs[u] = accs[u] + _pv_t(vt, [ps[ji, hh] for hh in range(HEADS_PER_TILE)], head_row)
        return {u: tuple(rs[u]) + (accs[u],) for u in states}

    zero = (jnp.zeros((1, T), F32),) * HEADS_PER_TILE + (jnp.zeros((LANES, T), F32),)
    jobs = []
    for u, c in enumerate(cs):
        jobs += [(u, c, True, None), (u, jnp.maximum(c - 1, 0), False, (c > 0) if u == 0 else None)]
    states = run(jobs, {u: zero for u in range(SB_TILES_PER_STEP)})

    for u, c in enumerate(cs):
        def cond(carry, c=c):
            live = jnp.max(jnp.maximum(carry[1], carry[2])) > SB_DEAD_LOG
            return jnp.logical_and(carry[0] < c, live)

        def body(carry, u=u, c=c):
            i = carry[0]
            return (i + 1,) + run([(u, c - 1 - i, False, None)], {u: carry[1:]})[u]

        fin = lax.while_loop(cond, body, (jnp.int32(1),) + states[u])
        o_ref[0, u * T:(u + 1) * T, :] = fin[-1].T.astype(o_ref.dtype)


def _sb_attention(q, k, vt):
    B, S, D = q.shape
    T = ATT_BLOCK
    nb = S // T
    n_hp = D // LANES
    tq = SB_TILES_PER_STEP * T
    return pl.pallas_call(
        _sb_body,
        grid=(B, n_hp, S // tq),
        in_specs=[pl.BlockSpec((1, tq, LANES), lambda b, h, i: (b, i, h)),
                  pl.BlockSpec((1, S, LANES), lambda b, h, i: (b, 0, h)),
                  pl.BlockSpec((1, nb, LANES, T), lambda b, h, i: (b, 0, h, 0))],
        out_specs=pl.BlockSpec((1, tq, LANES), lambda b, h, i: (b, i, h)),
        out_shape=jax.ShapeDtypeStruct((B, S, D), BF16),
        compiler_params=_params(3),
        name="stick_breaking_attention",
    )(q, k, vt)


def _swiglu_chunk(xb, wg, wu, w2):
    g = jnp.dot(xb, wg, preferred_element_type=F32)
    u = jnp.dot(xb, wu, preferred_element_type=F32)
    a = (g * jax.nn.sigmoid(g) * u).astype(BF16)
    return jnp.dot(a, w2, preferred_element_type=F32)


def _dense_ffn_body(x_ref, g_ref, sh_ref, sc_ref, gate_ref, wg_ref, wu_ref, w2_ref, o_ref,
                    xb_sc, acc_sc, *, n_chunks):
    j = pl.program_id(1)

    @pl.when(j == 0)
    def _():
        xb_sc[...] = _norm_mod(x_ref[...], g_ref[...], sh_ref[0], sc_ref[0]).astype(BF16)

    y = _swiglu_chunk(xb_sc[...], wg_ref[0], wu_ref[0], w2_ref[0])

    @pl.when(j == 0)
    def _():
        acc_sc[...] = y

    @pl.when(j > 0)
    def _():
        acc_sc[...] += y

    @pl.when(j == n_chunks - 1)
    def _():
        o_ref[...] = x_ref[...] + gate_ref[0] * acc_sc[...]


def _dense_ffn(x, g, sh, sc, gate, w13, w2):
    B, S, D = x.shape
    Fh = w2.shape[1]
    tm, tf = 512, 1408
    n_chunks = Fh // tf
    per_b = S // tm
    xf = x.reshape(B * S, D)
    out = pl.pallas_call(
        functools.partial(_dense_ffn_body, n_chunks=n_chunks),
        grid=(B * S // tm, n_chunks),
        in_specs=[pl.BlockSpec((tm, D), lambda i, j: (i, 0)),
                  pl.BlockSpec((1, D), lambda i, j: (0, 0)),
                  pl.BlockSpec((1, 1, D), lambda i, j: (i // per_b, 0, 0)),
                  pl.BlockSpec((1, 1, D), lambda i, j: (i // per_b, 0, 0)),
                  pl.BlockSpec((1, 1, D), lambda i, j: (i // per_b, 0, 0)),
                  pl.BlockSpec((1, D, tf), lambda i, j: (0, 0, j)),
                  pl.BlockSpec((1, D, tf), lambda i, j: (0, 0, j + n_chunks)),
                  pl.BlockSpec((1, tf, D), lambda i, j: (0, j, 0))],
        out_specs=pl.BlockSpec((tm, D), lambda i, j: (i, 0)),
        out_shape=jax.ShapeDtypeStruct((B * S, D), F32),
        scratch_shapes=[pltpu.VMEM((tm, D), BF16), pltpu.VMEM((tm, D), F32)],
        compiler_params=_params(2),
        name="dense_swiglu",
    )(xf, g.reshape(1, D), sh, sc, gate, w13, w13, w2)
    return out.reshape(B, S, D)


def _router_body(x_ref, g_ref, sh_ref, sc_ref, rwt_ref, h_ref, ei_ref, w_ref):
    h = _norm_mod(x_ref[0], g_ref[...], sh_ref[0], sc_ref[0])
    h_ref[...] = h
    logits = lax.dot_general(rwt_ref[...], h, _NT, preferred_element_type=F32,
                             precision=HIGHEST)
    n_e = logits.shape[0]
    eio = lax.broadcasted_iota(jnp.int32, logits.shape, 0)
    m1 = jnp.max(logits, axis=0, keepdims=True)
    i1 = jnp.min(jnp.where(logits == m1, eio, n_e), axis=0, keepdims=True)
    rest = jnp.where(eio == i1, -jnp.inf, logits)
    m2 = jnp.max(rest, axis=0, keepdims=True)
    i2 = jnp.min(jnp.where(rest == m2, eio, n_e), axis=0, keepdims=True)
    t = jnp.exp(m2 - m1)
    den = 1.0 + t
    ei_ref[...] = jnp.concatenate([i1, i2], axis=0)
    w_ref[...] = jnp.concatenate([1.0 / den, t / den], axis=0)


def _router(x, g, sh, sc, router_w):
    B, S, D = x.shape
    E = router_w.shape[1]
    tm = 512
    per_b = S // tm
    return pl.pallas_call(
        _router_body,
        grid=(B, per_b),
        in_specs=[pl.BlockSpec((1, tm, D), lambda b, i: (b, i, 0)),
                  pl.BlockSpec((1, D), lambda b, i: (0, 0)),
                  pl.BlockSpec((1, 1, D), lambda b, i: (b, 0, 0)),
                  pl.BlockSpec((1, 1, D), lambda b, i: (b, 0, 0)),
                  pl.BlockSpec((E, D), lambda b, i: (0, 0))],
        out_specs=[pl.BlockSpec((tm, D), lambda b, i: (b * per_b + i, 0)),
                   pl.BlockSpec((TOP_K_EXPERTS, tm), lambda b, i: (0, b * per_b + i)),
                   pl.BlockSpec((TOP_K_EXPERTS, tm), lambda b, i: (0, b * per_b + i))],
        out_shape=[jax.ShapeDtypeStruct((B * S, D), F32),
                   jax.ShapeDtypeStruct((TOP_K_EXPERTS, B * S), jnp.int32),
                   jax.ShapeDtypeStruct((TOP_K_EXPERTS, B * S), F32)],
        compiler_params=_params(2),
        name="moe_router",
    )(x, g.reshape(1, D), sh, sc, router_w.T)


def _route_plan(ei, tm, n_tiles):
    n_tok = ei.shape[1]
    e_flat = ei.reshape(-1)
    onehot = (e_flat[:, None] == jnp.arange(N_EXPERTS, dtype=jnp.int32)[None, :]).astype(jnp.int32)
    csum = jnp.cumsum(onehot, axis=0)
    rank = jnp.take_along_axis(csum, e_flat[:, None], axis=1)[:, 0] - 1
    counts = csum[-1]
    padded = ((counts + tm - 1) // tm) * tm
    ends = jnp.cumsum(padded)
    offs = ends - padded
    pos = (offs[e_flat] + rank).astype(jnp.int32)
    spare = TOP_K_EXPERTS * n_tok + jnp.arange((n_tiles + 1) * tm, dtype=jnp.int32) % tm
    flat = spare.at[tm + pos].set(jnp.arange(TOP_K_EXPERTS * n_tok, dtype=jnp.int32))
    starts = jnp.arange(n_tiles, dtype=jnp.int32) * tm
    t_exp = jnp.minimum(jnp.searchsorted(ends, starts, side="right"), N_EXPERTS - 1).astype(jnp.int32)
    t_rows = jnp.clip(offs[t_exp] + counts[t_exp] - starts, 0, tm)
    t_rows = jnp.where(starts < ends[-1], t_rows, 0).astype(jnp.int32)
    return flat, t_exp, t_rows


def _moe_ffn_body(te_ref, tr_ref, flat_ref, h_hbm, wg_ref, wu_ref, w2_ref, ys_hbm,
                  xstage, xb_sc, acc_sc, ybuf, gsem, ssem, *, tm, n_chunks, n_tiles, n_tok):
    i = pl.program_id(0)
    j = pl.program_id(1)
    valid = tr_ref[i] > 0
    rows_per_step = tm // n_chunks

    def gather_row(tile, q, rr):
        tok = flat_ref[(tile + 1) * tm + q * rows_per_step + rr] & (n_tok - 1)
        return pltpu.make_async_copy(h_hbm.at[pl.ds(tok, 1)], xstage.at[q, pl.ds(rr, 1)], gsem)

    def scatter_row(tile, q, rr):
        dst = flat_ref[(tile + 1) * tm + q * rows_per_step + rr]
        return pltpu.make_async_copy(ybuf.at[q, pl.ds(rr, 1)], ys_hbm.at[pl.ds(dst, 1)], ssem)

    def gather_wait(steps):
        for q in range(steps):
            pltpu.make_async_copy(h_hbm.at[pl.ds(0, rows_per_step)], xstage.at[q], gsem).wait()

    def scatter_wait(steps):
        for q in range(steps):
            pltpu.make_async_copy(ybuf.at[q], ys_hbm.at[pl.ds(0, rows_per_step)], ssem).wait()

    def all_rows(start_row):
        def one(q, _):
            for rr in range(rows_per_step):
                start_row(q, rr).start()
            return 0
        lax.fori_loop(0, n_chunks, one, 0)

    @pl.when(valid)
    def _():
        @pl.when(j == 0)
        def _():
            @pl.when(i == 0)
            def _():
                all_rows(functools.partial(gather_row, 0))
                gather_wait(n_chunks)
                ybuf[...] = jnp.zeros_like(ybuf)

            xb_sc[...] = xstage[...].reshape(tm, -1).astype(BF16)

        for rr in range(rows_per_step):
            gather_row(i + 1, j, rr).start()
            scatter_row(i - 1, j, rr).start()

        xb = xb_sc[...]
        g = jnp.dot(xb, wg_ref[0].astype(BF16), preferred_element_type=F32)
        u = jnp.dot(xb, wu_ref[0].astype(BF16), preferred_element_type=F32)
        a = (g * jax.nn.sigmoid(g) * u).astype(BF16)
        gather_wait(1)
        scatter_wait(1)
        y = jnp.dot(a, w2_ref[0].astype(BF16), preferred_element_type=F32)

        @pl.when(j == 0)
        def _():
            acc_sc[...] = y

        @pl.when(jnp.logical_and(j > 0, j < n_chunks - 1))
        def _():
            acc_sc[...] += y

        @pl.when(j == n_chunks - 1)
        def _():
            ybuf[...] = (acc_sc[...] + y).reshape(ybuf.shape)

    prv = jnp.maximum(i - 1, 0)

    @pl.when(jnp.logical_and(jnp.logical_and(jnp.logical_not(valid), j == 0),
                             jnp.logical_and(i > 0, tr_ref[prv] > 0)))
    def _():
        all_rows(functools.partial(scatter_row, prv))
        scatter_wait(n_chunks)


def _moe_ffn(h, flat, t_exp, t_rows, w13, w2, tm):
    n_tok, D = h.shape
    Fh = w2.shape[1]
    n_tiles = t_exp.shape[0]
    n_chunks = 4
    tf = Fh // n_chunks
    assert tf % LANES == 0 and tm % n_chunks == 0 and n_tok & (n_tok - 1) == 0 and n_tok >= tm

    def wmap(col_off):
        return lambda i, j, te, tr, fl: (te[i], 0, j * jnp.minimum(tr[i], 1) + col_off)

    return pl.pallas_call(
        functools.partial(_moe_ffn_body, tm=tm, n_chunks=n_chunks, n_tiles=n_tiles, n_tok=n_tok),
        grid_spec=pltpu.PrefetchScalarGridSpec(
            num_scalar_prefetch=3,
            grid=(n_tiles, n_chunks),
            in_specs=[pl.BlockSpec(memory_space=pl.ANY),
                      pl.BlockSpec((1, D, tf), wmap(0)),
                      pl.BlockSpec((1, D, tf), wmap(n_chunks)),
                      pl.BlockSpec((1, tf, D), lambda i, j, te, tr, fl: (te[i], j * jnp.minimum(tr[i], 1), 0))],
            out_specs=pl.BlockSpec(memory_space=pl.ANY),
            scratch_shapes=[pltpu.VMEM((n_chunks, tm // n_chunks, D), F32), pltpu.VMEM((tm, D), BF16),
                            pltpu.VMEM((tm, D), F32), pltpu.VMEM((n_chunks, tm // n_chunks, D), F32),
                            pltpu.SemaphoreType.DMA, pltpu.SemaphoreType.DMA]),
        out_shape=jax.ShapeDtypeStruct((TOP_K_EXPERTS * n_tok + tm, D), F32),
        compiler_params=_params(2),
        name="moe_expert_swiglu",
    )(t_exp, t_rows, flat, h, w13, w13, w2)


def _combine_body(x_ref, y0_ref, y1_ref, w_ref, gate_ref, fg_ref, o_ref, *, final_norm):
    w = w_ref[...]
    y = w[:, 0:1] * y0_ref[...] + w[:, 1:2] * y1_ref[...]
    out = x_ref[...] + gate_ref[0] * y
    if final_norm:
        ms = jnp.mean(out * out, axis=-1, keepdims=True)
        out = (out * lax.rsqrt(ms + EPS)) * fg_ref[...]
    o_ref[...] = out


def _moe_combine(x, ys, wts, gate, final_g, final_norm):
    B, S, D = x.shape
    n_tok = B * S
    rows = 512
    per_b = S // rows
    out = pl.pallas_call(
        functools.partial(_combine_body, final_norm=final_norm),
        grid=(n_tok // rows,),
        in_specs=[pl.BlockSpec((rows, D), lambda i: (i, 0)),
                  pl.BlockSpec((rows, D), lambda i: (i, 0)),
                  pl.BlockSpec((rows, D), lambda i: (n_tok // rows + i, 0)),
                  pl.BlockSpec((rows, TOP_K_EXPERTS), lambda i: (i, 0)),
                  pl.BlockSpec((1, 1, D), lambda i: (i // per_b, 0, 0)),
                  pl.BlockSpec((1, D), lambda i: (0, 0))],
        out_specs=pl.BlockSpec((rows, D), lambda i: (i, 0)),
        out_shape=jax.ShapeDtypeStruct((n_tok, D), F32),
        compiler_params=_params(1),
        name="moe_combine",
    )(x.reshape(n_tok, D), ys, ys, wts, gate, final_g.reshape(1, D))
    return out.reshape(B, S, D)


def _moe_layer(x, g, sh, sc, gate, router_w, w13, w2, first_expert, final_g, final_norm):
    B, S, D = x.shape
    n_tok = B * S
    tm = 1024
    n_tiles = TOP_K_EXPERTS * n_tok // tm + N_EXPERTS
    h, ei, wts = _router(x, g, sh, sc, router_w)
    flat, t_exp, t_rows = _route_plan(ei, tm, n_tiles)
    ys = _moe_ffn(h, flat, t_exp + first_expert, t_rows, w13, w2, tm)
    return _moe_combine(x, ys, wts.T, gate, final_g, final_norm)


def kernel(x, c, rel_bias, mod_w, mod_b, norm_mix_g, norm_ffn_g, a_wqkv, a_wo, kv_norm_g, kv_mod_w,
           kv_mod_b, b_wkv, b_wq, b_wo, ffn_w13, ffn_w2, router_w, moe_w13, moe_w2, final_norm_g):
    D = D_MODEL
    mods = _modulation(c, mod_w, mod_b)
    kv_mod = _modulation(c, kv_mod_w[None], kv_mod_b[None])[0]
    bias_tiles = _bias_tiles(rel_bias)
    k_sh = vt_sh = None
    for l in range(DEPTH):
        sh_a, sc_a, g_a, sh_f, sc_f, g_f = [mods[l][:, None, i * D:(i + 1) * D] for i in range(6)]
        if l < N_A_LAYERS:
            w = a_wqkv[l].astype(BF16)
            qk = _norm_mod_matmul(x, norm_mix_g[l], sh_a, sc_a, w[:, :2 * D])
            vt = _norm_mod_matmul_t(x, norm_mix_g[l], sh_a, sc_a, w[:, 2 * D:].T)
            attn = _moba_attention(qk, vt, bias_tiles)
            x = _matmul_gated_residual(attn, a_wo[l].astype(BF16), x, g_a)
        else:
            j = l - N_A_LAYERS
            if j == 0:
                w = b_wkv.astype(BF16)
                kv_sh, kv_sc = kv_mod[:, None, :D], kv_mod[:, None, D:]
                k_sh = _norm_mod_matmul(x, kv_norm_g, kv_sh, kv_sc, w[:, :D])
                vt_sh = _norm_mod_matmul_t(x, kv_norm_g, kv_sh, kv_sc, w[:, D:].T)
            q = _norm_mod_matmul(x, norm_mix_g[l], sh_a, sc_a, b_wq[j].astype(BF16))
            attn = _sb_attention(q, k_sh, vt_sh)
            x = _matmul_gated_residual(attn, b_wo[j].astype(BF16), x, g_a)
        if l % 2 == 0:
            x = _dense_ffn(x, norm_ffn_g[l], sh_f, sc_f, g_f,
                           ffn_w13[l // 2][None].astype(BF16), ffn_w2[l // 2][None].astype(BF16))
        else:
            x = _moe_layer(x, norm_ffn_g[l], sh_f, sc_f, g_f, router_w[l // 2],
                           moe_w13.reshape((-1,) + moe_w13.shape[2:]),
                           moe_w2.reshape((-1,) + moe_w2.shape[2:]), (l // 2) * N_EXPERTS,
                           final_norm_g, l == DEPTH - 1)
    return x
```

```python
import functools
import math

import numpy as np
import jax
import jax.numpy as jnp
from jax import lax
from jax.experimental import pallas as pl
from jax.experimental.pallas import tpu as pltpu

D_MODEL = 1024
DEPTH = 4
N_HEADS = 16
HEAD_DIM = D_MODEL // N_HEADS
N_A_LAYERS = DEPTH // 2
MOBA_BLOCK = 256
MOBA_TOPK = 3
N_BUCKETS = 32
MAX_DISTANCE = 1024
N_EXPERTS = 8
TOP_K_EXPERTS = 2
EPS = 1e-6
NEG = -1e30

LANES = 128
HEADS_PER_TILE = LANES // HEAD_DIM
ATT_BLOCK = 256
N_BIAS_TILES = 6
SB_DEAD_LOG = -105.0
VMEM_LIMIT = 56 * 1024 * 1024

F32 = jnp.float32
BF16 = jnp.bfloat16
HIGHEST = lax.Precision.HIGHEST
_NT = (((1,), (1,)), ((), ()))


def _params(n_axes):
    return pltpu.CompilerParams(dimension_semantics=("arbitrary",) * n_axes,
                                vmem_limit_bytes=VMEM_LIMIT)


def _norm_mod(x, g, sh, sc):
    ms = jnp.mean(x * x, axis=-1, keepdims=True)
    y = (x * lax.rsqrt(ms + EPS)) * g
    return y * (1.0 + sc) + sh


def _mod_body(c_ref, w_ref, b_ref, o_ref):
    c = c_ref[...]
    s = c * jax.nn.sigmoid(c)
    o_ref[0] = jnp.dot(s, w_ref[0], preferred_element_type=F32, precision=HIGHEST) + b_ref[0]


def _modulation(c, w, b):
    L, D, M = w.shape
    B = c.shape[0]
    tn = 1024
    return pl.pallas_call(
        _mod_body,
        grid=(L, M // tn),
        in_specs=[pl.BlockSpec((B, D), lambda l, j: (0, 0)),
                  pl.BlockSpec((1, D, tn), lambda l, j: (l, 0, j)),
                  pl.BlockSpec((1, 1, tn), lambda l, j: (l, 0, j))],
        out_specs=pl.BlockSpec((1, B, tn), lambda l, j: (l, 0, j)),
        out_shape=jax.ShapeDtypeStruct((L, B, M), F32),
        compiler_params=_params(2),
        name="adaln_mod",
    )(c, w, b.reshape(L, 1, M))


def _nm_mm_body(x_ref, g_ref, sh_ref, sc_ref, w_ref, o_ref, h_sc):
    @pl.when(pl.program_id(2) == 0)
    def _():
        h_sc[...] = _norm_mod(x_ref[0], g_ref[...], sh_ref[0], sc_ref[0]).astype(BF16)

    o_ref[0] = jnp.dot(h_sc[...], w_ref[...], preferred_element_type=F32).astype(o_ref.dtype)


def _norm_mod_matmul(x, g, sh, sc, w):
    B, S, D = x.shape
    N = w.shape[1]
    tm, tn = 1024, 1024
    return pl.pallas_call(
        _nm_mm_body,
        grid=(B, S // tm, N // tn),
        in_specs=[pl.BlockSpec((1, tm, D), lambda b, i, j: (b, i, 0)),
                  pl.BlockSpec((1, D), lambda b, i, j: (0, 0)),
                  pl.BlockSpec((1, 1, D), lambda b, i, j: (b, 0, 0)),
                  pl.BlockSpec((1, 1, D), lambda b, i, j: (b, 0, 0)),
                  pl.BlockSpec((D, tn), lambda b, i, j: (0, j))],
        out_specs=pl.BlockSpec((1, tm, tn), lambda b, i, j: (b, i, j)),
        out_shape=jax.ShapeDtypeStruct((B, S, N), BF16),
        scratch_shapes=[pltpu.VMEM((tm, D), BF16)],
        compiler_params=_params(3),
        name="norm_mod_proj",
    )(x, g.reshape(1, D), sh, sc, w)


def _nm_mm_t_body(x_ref, g_ref, sh_ref, sc_ref, wt_ref, o_ref, h_sc):
    @pl.when(pl.program_id(2) == 0)
    def _():
        h_sc[...] = _norm_mod(x_ref[0], g_ref[...], sh_ref[0], sc_ref[0]).astype(BF16)

    res = lax.dot_general(wt_ref[...], h_sc[...], _NT, preferred_element_type=F32)
    for r in range(o_ref.shape[1]):
        o_ref[0, r] = res[:, r * ATT_BLOCK:(r + 1) * ATT_BLOCK].astype(o_ref.dtype)


def _norm_mod_matmul_t(x, g, sh, sc, wt):
    B, S, D = x.shape
    N = wt.shape[0]
    tm, tn = 1024, 1024
    T = ATT_BLOCK
    return pl.pallas_call(
        _nm_mm_t_body,
        grid=(B, S // tm, N // tn),
        in_specs=[pl.BlockSpec((1, tm, D), lambda b, i, j: (b, i, 0)),
                  pl.BlockSpec((1, D), lambda b, i, j: (0, 0)),
                  pl.BlockSpec((1, 1, D), lambda b, i, j: (b, 0, 0)),
                  pl.BlockSpec((1, 1, D), lambda b, i, j: (b, 0, 0)),
                  pl.BlockSpec((tn, D), lambda b, i, j: (j, 0))],
        out_specs=pl.BlockSpec((1, tm // T, tn, T), lambda b, i, j: (b, i, j, 0)),
        out_shape=jax.ShapeDtypeStruct((B, S // T, N, T), BF16),
        scratch_shapes=[pltpu.VMEM((tm, D), BF16)],
        compiler_params=_params(3),
        name="norm_mod_proj_t",
    )(x, g.reshape(1, D), sh, sc, wt)


def _mm_res_body(a_ref, w_ref, r_ref, gate_ref, o_ref):
    acc = jnp.dot(a_ref[0], w_ref[...], preferred_element_type=F32)
    o_ref[0] = r_ref[0] + gate_ref[0] * acc


def _matmul_gated_residual(a, w, res, gate):
    B, S, K = a.shape
    N = w.shape[1]
    tm, tn = 1024, 1024
    return pl.pallas_call(
        _mm_res_body,
        grid=(B, S // tm, N // tn),
        in_specs=[pl.BlockSpec((1, tm, K), lambda b, i, j: (b, i, 0)),
                  pl.BlockSpec((K, tn), lambda b, i, j: (0, j)),
                  pl.BlockSpec((1, tm, tn), lambda b, i, j: (b, i, j)),
                  pl.BlockSpec((1, 1, tn), lambda b, i, j: (b, 0, j))],
        out_specs=pl.BlockSpec((1, tm, tn), lambda b, i, j: (b, i, j)),
        out_shape=jax.ShapeDtypeStruct((B, S, N), F32),
        compiler_params=_params(3),
        name="out_proj_residual",
    )(a, w, res, gate)


def _pv_t(vt, ps, head_row):
    zeros = jnp.zeros_like(vt)
    vcat = jnp.concatenate(
        [jnp.where((head_row >= hh * HEAD_DIM) & (head_row < (hh + 1) * HEAD_DIM), vt, zeros)
         for hh in range(HEADS_PER_TILE)], axis=1)
    return jnp.dot(vcat, jnp.concatenate(ps, axis=0), preferred_element_type=F32)


def _per_head_rows(head_row, vals):
    assert HEADS_PER_TILE == 2
    return jnp.where(head_row < HEAD_DIM, vals[0], vals[1])


def _head_queries(q, lane):
    out = []
    for hh in range(HEADS_PER_TILE):
        in_head = (lane >= hh * HEAD_DIM) & (lane < (hh + 1) * HEAD_DIM)
        out.append(jnp.where(in_head, q, jnp.zeros_like(q)))
    return out


def _rel_bucket(dist):
    max_exact = N_BUCKETS // 2
    d = np.maximum(dist, 1).astype(np.float32)
    ratio = np.log(d / np.float32(max_exact)) / np.float32(math.log(MAX_DISTANCE / max_exact))
    large = max_exact + (ratio * np.float32(N_BUCKETS - max_exact)).astype(np.int32)
    large = np.minimum(large, N_BUCKETS - 1)
    return np.where(dist < max_exact, dist, large)


def _bias_tiles(rel_bias):
    T = ATT_BLOCK
    W = 2 * T
    H = rel_bias.shape[1]
    k = np.arange(W)
    lag = np.where(k <= T, k, k - W)
    dist = np.maximum(np.arange(N_BIAS_TILES)[:, None] * T + lag[None, :], 0)
    bucket = _rel_bucket(dist)
    assert (bucket[N_BIAS_TILES - 1] == N_BUCKETS - 1).all()
    onehot = np.eye(N_BUCKETS, dtype=np.float32)[bucket.reshape(-1)]
    vec = jnp.dot(jnp.asarray(onehot), rel_bias, precision=HIGHEST)
    a = vec.T.reshape(H, N_BIAS_TILES, W)
    tiled = jnp.tile(a, (1, 1, T))
    return tiled[:, :, :T * (W - 1)].reshape(H, N_BIAS_TILES, T, W - 1)[:, :, :, :T]


MOBA_TILES_PER_STEP = 2


def _moba_body(q_ref, k_ref, vt_ref, bias_ref, o_ref, kmean_sc, sel_sc, s_sc, p_sc, *, nb):
    step = pl.program_id(2)
    T = ATT_BLOCK
    tiles = range(MOBA_TILES_PER_STEP)
    heads = range(HEADS_PER_TILE)
    assert MOBA_TILES_PER_STEP == 2
    c0 = step * MOBA_TILES_PER_STEP
    cs = [c0 + u for u in tiles]

    @pl.when(step == 0)
    def _():
        rest = jnp.concatenate(
            [jnp.sum(k_ref[0, n * T:(n + 1) * T, :].astype(F32), axis=0, keepdims=True) * (1.0 / T)
             for n in range(nb)], axis=0)
        for part in range(3):
            term = rest.astype(BF16)
            kmean_sc[part * nb:(part + 1) * nb, :] = term
            rest = rest - term.astype(F32)

    lane = lax.broadcasted_iota(jnp.int32, (T, LANES), 1)
    blk = lax.broadcasted_iota(jnp.int32, (nb, T), 0)
    key_i = lax.broadcasted_iota(jnp.int32, (T, T), 0)
    qry_i = lax.broadcasted_iota(jnp.int32, (T, T), 1)
    causal = key_i <= qry_i
    head_row = lax.broadcasted_iota(jnp.int32, (LANES, T), 0)
    kmean = kmean_sc[...]
    scale = HEAD_DIM ** -0.5

    qs = []
    for u in tiles:
        qs.append([])
        for hh, qh in enumerate(_head_queries(q_ref[0, u * T:(u + 1) * T, :], lane)):
            terms = lax.dot_general(kmean, qh, _NT, preferred_element_type=F32)
            gate = terms[:nb] + terms[nb:2 * nb] + terms[2 * nb:]
            gate = jnp.where(blk < cs[u], gate, NEG)
            rank = jnp.zeros((nb, T), jnp.int32)
            for m in range(nb):
                gm = gate[m:m + 1, :]
                beats = (gm > gate) | ((gm == gate) & (blk > m))
                rank = rank + beats.astype(jnp.int32)
            sel_sc[u, hh] = ((rank < MOBA_TOPK) & (blk < cs[u])).astype(F32)
            qs[u].append((qh.astype(F32) * scale).astype(BF16))

    def scores(n, which=tiles):
        kb = k_ref[0, pl.ds(pl.multiple_of(n * T, T), T), :]
        return {u: [lax.dot_general(kb, qs[u][hh], _NT, preferred_element_type=F32) for hh in heads]
                for u in which}

    acc_rows = LANES + 16
    acc_row = lax.broadcasted_iota(jnp.int32, (acc_rows, T), 0)
    first_head = (acc_row < HEAD_DIM) | ((acc_row >= LANES) & (acc_row < LANES + 8))
    grp = lax.broadcasted_iota(jnp.int32, (16, T), 0) // 8
    ones_rows = [(grp == hh).astype(BF16) for hh in heads]

    def soften(u, n, raw, ms):
        dl = jnp.minimum(cs[u] - n, N_BIAS_TILES - 1)
        new_ms, alphas = [], []
        for hh in heads:
            m = ms[hh]
            picked = sel_sc[u, hh, pl.ds(n, 1), :] > 0.5
            s = raw[hh] + bias_ref[hh, pl.ds(dl, 1)][0]
            m_new = jnp.where(picked, jnp.maximum(m, jnp.max(s, axis=0, keepdims=True)), m)
            p_sc[u, hh] = jnp.exp(s - jnp.where(picked, m_new, -NEG)).astype(BF16)
            new_ms.append(m_new)
            alphas.append(jnp.exp(m - m_new))
        return new_ms, alphas

    def value_rows(blk_idx):
        vt = vt_ref[0, blk_idx]
        zeros = jnp.zeros_like(vt)
        return jnp.concatenate(
            [jnp.concatenate([jnp.where((head_row >= hh * HEAD_DIM) & (head_row < (hh + 1) * HEAD_DIM),
                                        vt, zeros), ones_rows[hh]], axis=0) for hh in heads],
            axis=1)

    def apply_pending(u, lhs, alphas, acc):
        ps = jnp.concatenate([p_sc[u, hh] for hh in heads], axis=0)
        return (jnp.where(first_head, alphas[0], alphas[1]) * acc
                + jnp.dot(lhs, ps, preferred_element_type=F32))

    def own_block(u, raw):
        ms = []
        for hh in heads:
            s = jnp.where(causal, raw[hh] + bias_ref[hh, 0], NEG)
            m = jnp.max(s, axis=0, keepdims=True)
            p_sc[u, hh] = jnp.exp(s - m).astype(BF16)
            ms.append(m)
        return ms

    raw_own = {u: scores(cs[u], [u])[u] for u in tiles}
    raw_extra = scores(c0, [1])[1]
    raw_next = scores(0)
    for u in tiles:
        for hh in heads:
            s_sc[u, hh] = raw_next[u][hh]

    ones = jnp.ones((1, T), F32)
    zero_acc = jnp.zeros((acc_rows, T), F32)
    ms = {u: own_block(u, raw_own[u]) for u in tiles}
    alphas = {0: [ones, ones]}
    accs = {0: zero_acc}
    accs[1] = apply_pending(1, value_rows(cs[1]), [ones, ones], zero_acc)
    ms[1], alphas[1] = soften(1, c0, raw_extra, ms[1])

    def pack(ms, alphas, accs):
        return tuple(ms[0]) + tuple(ms[1]) + tuple(alphas[0]) + tuple(alphas[1]) + (accs[0], accs[1])

    def unpack(carry):
        return ({0: list(carry[0:2]), 1: list(carry[2:4])}, {0: list(carry[4:6]), 1: list(carry[6:8])},
                {0: carry[8], 1: carry[9]})

    def body(t, carry):
        ms, alphas, accs = unpack(carry)
        raw_t = {u: [s_sc[u, hh] for hh in heads] for u in tiles}
        lhs = value_rows(jnp.where(t == 0, c0, t - 1))
        for u in tiles:
            accs[u] = apply_pending(u, lhs, alphas[u], accs[u])
        raw_n = scores(jnp.minimum(t + 1, c0 - 1))
        for u in tiles:
            ms[u], alphas[u] = soften(u, t, raw_t[u], ms[u])
        for u in tiles:
            for hh in heads:
                s_sc[u, hh] = raw_n[u][hh]
        return pack(ms, alphas, accs)

    ms, alphas, accs = unpack(lax.fori_loop(
        0, step, lambda i, carry: body(2 * i + 1, body(2 * i, carry)), pack(ms, alphas, accs)))
    lhs = value_rows(jnp.maximum(c0 - 1, 0))
    for u in tiles:
        acc = apply_pending(u, lhs, alphas[u], accs[u])
        norm = _per_head_rows(head_row, [acc[LANES:LANES + 1], acc[LANES + 8:LANES + 9]])
        o_ref[0, u * T:(u + 1) * T, :] = (acc[:LANES] / norm).T.astype(o_ref.dtype)


def _moba_attention(qk, vt, bias_tiles):
    B, S, D2 = qk.shape
    D = D2 // 2
    T = ATT_BLOCK
    nb = S // T
    n_hp = D // LANES
    tq = MOBA_TILES_PER_STEP * T
    return pl.pallas_call(
        functools.partial(_moba_body, nb=nb),
        grid=(B, n_hp, S // tq),
        in_specs=[pl.BlockSpec((1, tq, LANES), lambda b, h, i: (b, i, h)),
                  pl.BlockSpec((1, S, LANES), lambda b, h, i: (b, 0, n_hp + h)),
                  pl.BlockSpec((1, nb, LANES, T), lambda b, h, i: (b, 0, h, 0)),
                  pl.BlockSpec((HEADS_PER_TILE, N_BIAS_TILES, T, T), lambda b, h, i: (h, 0, 0, 0))],
        out_specs=pl.BlockSpec((1, tq, LANES), lambda b, h, i: (b, i, h)),
        out_shape=jax.ShapeDtypeStruct((B, S, D), BF16),
        scratch_shapes=[pltpu.VMEM((3 * nb, LANES), BF16),
                        pltpu.VMEM((MOBA_TILES_PER_STEP, HEADS_PER_TILE, nb, T), F32),
                        pltpu.VMEM((MOBA_TILES_PER_STEP, HEADS_PER_TILE, T, T), F32),
                        pltpu.VMEM((MOBA_TILES_PER_STEP, HEADS_PER_TILE, T, T), BF16)],
        compiler_params=_params(3),
        name="moba_attention",
    )(qk, qk, vt, bias_tiles)


SB_TILES_PER_STEP = 4


def _sb_body(q_ref, k_ref, vt_ref, o_ref):
    step = pl.program_id(2)
    T = ATT_BLOCK
    lane = lax.broadcasted_iota(jnp.int32, (T, LANES), 1)
    key_i = lax.broadcasted_iota(jnp.int32, (T, T), 0)
    qry_i = lax.broadcasted_iota(jnp.int32, (T, T), 1)
    strict = key_i < qry_i
    later = (qry_i > key_i).astype(BF16)
    later2 = jnp.concatenate([later, later], axis=1)
    head_row = lax.broadcasted_iota(jnp.int32, (LANES, T), 0)
    scale = HEAD_DIM ** -0.5
    cs = [step * SB_TILES_PER_STEP + u for u in range(SB_TILES_PER_STEP)]
    qs = [[(qh.astype(F32) * scale).astype(BF16)
           for qh in _head_queries(q_ref[0, u * T:(u + 1) * T, :], lane)] for u in range(SB_TILES_PER_STEP)]

    def run(jobs, states):
        rs = {u: list(st[:HEADS_PER_TILE]) for u, st in states.items()}
        accs = {u: st[-1] for u, st in states.items()}
        items = [(ji, hh) for ji in range(len(jobs)) for hh in range(HEADS_PER_TILE)]
        kbs = [k_ref[0, pl.ds(pl.multiple_of(n * T, T), T), :] for _, n, _, _ in jobs]
        z = {(ji, hh): lax.dot_general(kbs[ji], qs[jobs[ji][0]][hh], _NT, preferred_element_type=F32)
             for ji, hh in items}
        lb, l1, hilo = {}, {}, {}
        for t in items:
            sp = jnp.log(1.0 + jnp.exp(-jnp.abs(z[t])))
            lb[t] = jnp.minimum(z[t], 0.0) - sp
            l1[t] = lb[t] - z[t]
            if jobs[t[0]][2]:
                l1[t] = jnp.where(strict, l1[t], 0.0)
            hi = l1[t].astype(BF16)
            lo = (l1[t] - hi.astype(F32)).astype(BF16)
            hilo[t] = jnp.concatenate([hi, lo], axis=0)
        tail = {t: jnp.dot(later2, hilo[t], preferred_element_type=F32) for t in items}
        ps = {}
        for ji, hh in items:
            u, _, diag, keep = jobs[ji]
            a = jnp.exp(lb[ji, hh] + tail[ji, hh] + rs[u][hh])
            if diag:
                a = jnp.where(strict, a, 0.0)
            ps[ji, hh] = a.astype(BF16)
            inc = tail[ji, hh][0:1, :] + l1[ji, hh][0:1, :]
            if keep is not None:
                inc = jnp.where(keep, inc, 0.0)
            rs[u][hh] = rs[u][hh] + inc
        for ji, (u, n, _, keep) in enumerate(jobs):
            vt = vt_ref[0, n]
            if keep is not None:
                vt = jnp.where(keep, vt, jnp.zeros_like(vt))
            accs[u] = accs[u] + _pv_t(vt, [ps[ji, hh] for hh in range(HEADS_PER_TILE)], head_row)
        return {u: tuple(rs[u]) + (accs[u],) for u in states}

    zero = (jnp.zeros((1, T), F32),) * HEADS_PER_TILE + (jnp.zeros((LANES, T), F32),)
    jobs = []
    for u, c in enumerate(cs):
        jobs += [(u, c, True, None), (u, jnp.maximum(c - 1, 0), False, (c > 0) if u == 0 else None)]
    states = run(jobs, {u: zero for u in range(SB_TILES_PER_STEP)})

    for u, c in enumerate(cs):
        def cond(carry, c=c):
            live = jnp.max(jnp.maximum(carry[1], carry[2])) > SB_DEAD_LOG
            return jnp.logical_and(carry[0] < c, live)

        def body(carry, u=u, c=c):
            i = carry[0]
            return (i + 1,) + run([(u, c - 1 - i, False, None)], {u: carry[1:]})[u]

        fin = lax.while_loop(cond, body, (jnp.int32(1),) + states[u])
        o_ref[0, u * T:(u + 1) * T, :] = fin[-1].T.astype(o_ref.dtype)


def _sb_attention(q, k, vt):
    B, S, D = q.shape
    T = ATT_BLOCK
    nb = S // T
    n_hp = D // LANES
    tq = SB_TILES_PER_STEP * T
    return pl.pallas_call(
        _sb_body,
        grid=(B, n_hp, S // tq),
        in_specs=[pl.BlockSpec((1, tq, LANES), lambda b, h, i: (b, i, h)),
                  pl.BlockSpec((1, S, LANES), lambda b, h, i: (b, 0, h)),
                  pl.BlockSpec((1, nb, LANES, T), lambda b, h, i: (b, 0, h, 0))],
        out_specs=pl.BlockSpec((1, tq, LANES), lambda b, h, i: (b, i, h)),
        out_shape=jax.ShapeDtypeStruct((B, S, D), BF16),
        compiler_params=_params(3),
        name="stick_breaking_attention",
    )(q, k, vt)


def _swiglu_chunk(xb, wg, wu, w2):
    g = jnp.dot(xb, wg, preferred_element_type=F32)
    u = jnp.dot(xb, wu, preferred_element_type=F32)
    a = (g * jax.nn.sigmoid(g) * u).astype(BF16)
    return jnp.dot(a, w2, preferred_element_type=F32)


def _dense_ffn_body(x_ref, g_ref, sh_ref, sc_ref, gate_ref, wg_ref, wu_ref, w2_ref, o_ref,
                    xb_sc, acc_sc, *, n_chunks):
    j = pl.program_id(1)

    @pl.when(j == 0)
    def _():
        xb_sc[...] = _norm_mod(x_ref[...], g_ref[...], sh_ref[0], sc_ref[0]).astype(BF16)

    y = _swiglu_chunk(xb_sc[...], wg_ref[0], wu_ref[0], w2_ref[0])

    @pl.when(j == 0)
    def _():
        acc_sc[...] = y

    @pl.when(j > 0)
    def _():
        acc_sc[...] += y

    @pl.when(j == n_chunks - 1)
    def _():
        o_ref[...] = x_ref[...] + gate_ref[0] * acc_sc[...]


def _dense_ffn(x, g, sh, sc, gate, w13, w2):
    B, S, D = x.shape
    Fh = w2.shape[1]
    tm, tf = 512, 1408
    n_chunks = Fh // tf
    per_b = S // tm
    xf = x.reshape(B * S, D)
    out = pl.pallas_call(
        functools.partial(_dense_ffn_body, n_chunks=n_chunks),
        grid=(B * S // tm, n_chunks),
        in_specs=[pl.BlockSpec((tm, D), lambda i, j: (i, 0)),
                  pl.BlockSpec((1, D), lambda i, j: (0, 0)),
                  pl.BlockSpec((1, 1, D), lambda i, j: (i // per_b, 0, 0)),
                  pl.BlockSpec((1, 1, D), lambda i, j: (i // per_b, 0, 0)),
                  pl.BlockSpec((1, 1, D), lambda i, j: (i // per_b, 0, 0)),
                  pl.BlockSpec((1, D, tf), lambda i, j: (0, 0, j)),
                  pl.BlockSpec((1, D, tf), lambda i, j: (0, 0, j + n_chunks)),
                  pl.BlockSpec((1, tf, D), lambda i, j: (0, j, 0))],
        out_specs=pl.BlockSpec((tm, D), lambda i, j: (i, 0)),
        out_shape=jax.ShapeDtypeStruct((B * S, D), F32),
        scratch_shapes=[pltpu.VMEM((tm, D), BF16), pltpu.VMEM((tm, D), F32)],
        compiler_params=_params(2),
        name="dense_swiglu",
    )(xf, g.reshape(1, D), sh, sc, gate, w13, w13, w2)
    return out.reshape(B, S, D)


def _router_body(x_ref, g_ref, sh_ref, sc_ref, rwt_ref, h_ref, ei_ref, w_ref):
    h = _norm_mod(x_ref[0], g_ref[...], sh_ref[0], sc_ref[0])
    h_ref[...] = h
    logits = lax.dot_general(rwt_ref[...], h, _NT, preferred_element_type=F32,
                             precision=HIGHEST)
    n_e = logits.shape[0]
    eio = lax.broadcasted_iota(jnp.int32, logits.shape, 0)
    m1 = jnp.max(logits, axis=0, keepdims=True)
    i1 = jnp.min(jnp.where(logits == m1, eio, n_e), axis=0, keepdims=True)
    rest = jnp.where(eio == i1, -jnp.inf, logits)
    m2 = jnp.max(rest, axis=0, keepdims=True)
    i2 = jnp.min(jnp.where(rest == m2, eio, n_e), axis=0, keepdims=True)
    t = jnp.exp(m2 - m1)
    den = 1.0 + t
    ei_ref[...] = jnp.concatenate([i1, i2], axis=0)
    w_ref[...] = jnp.concatenate([1.0 / den, t / den], axis=0)


def _router(x, g, sh, sc, router_w):
    B, S, D = x.shape
    E = router_w.shape[1]
    tm = 512
    per_b = S // tm
    return pl.pallas_call(
        _router_body,
        grid=(B, per_b),
        in_specs=[pl.BlockSpec((1, tm, D), lambda b, i: (b, i, 0)),
                  pl.BlockSpec((1, D), lambda b, i: (0, 0)),
                  pl.BlockSpec((1, 1, D), lambda b, i: (b, 0, 0)),
                  pl.BlockSpec((1, 1, D), lambda b, i: (b, 0, 0)),
                  pl.BlockSpec((E, D), lambda b, i: (0, 0))],
        out_specs=[pl.BlockSpec((tm, D), lambda b, i: (b * per_b + i, 0)),
                   pl.BlockSpec((TOP_K_EXPERTS, tm), lambda b, i: (0, b * per_b + i)),
                   pl.BlockSpec((TOP_K_EXPERTS, tm), lambda b, i: (0, b * per_b + i))],
        out_shape=[jax.ShapeDtypeStruct((B * S, D), F32),
                   jax.ShapeDtypeStruct((TOP_K_EXPERTS, B * S), jnp.int32),
                   jax.ShapeDtypeStruct((TOP_K_EXPERTS, B * S), F32)],
        compiler_params=_params(2),
        name="moe_router",
    )(x, g.reshape(1, D), sh, sc, router_w.T)


def _route_plan(ei, tm, n_tiles):
    n_tok = ei.shape[1]
    e_flat = ei.reshape(-1)
    onehot = (e_flat[:, None] == jnp.arange(N_EXPERTS, dtype=jnp.int32)[None, :]).astype(jnp.int32)
    csum = jnp.cumsum(onehot, axis=0)
    rank = jnp.take_along_axis(csum, e_flat[:, None], axis=1)[:, 0] - 1
    counts = csum[-1]
    padded = ((counts + tm - 1) // tm) * tm
    ends = jnp.cumsum(padded)
    offs = ends - padded
    pos = (offs[e_flat] + rank).astype(jnp.int32)
    spare = TOP_K_EXPERTS * n_tok + jnp.arange((n_tiles + 1) * tm, dtype=jnp.int32) % tm
    flat = spare.at[tm + pos].set(jnp.arange(TOP_K_EXPERTS * n_tok, dtype=jnp.int32))
    starts = jnp.arange(n_tiles, dtype=jnp.int32) * tm
    t_exp = jnp.minimum(jnp.searchsorted(ends, starts, side="right"), N_EXPERTS - 1).astype(jnp.int32)
    t_rows = jnp.clip(offs[t_exp] + counts[t_exp] - starts, 0, tm)
    t_rows = jnp.where(starts < ends[-1], t_rows, 0).astype(jnp.int32)
    return flat, t_exp, t_rows


def _moe_ffn_body(te_ref, tr_ref, flat_ref, h_hbm, wg_ref, wu_ref, w2_ref, ys_hbm,
                  xstage, xb_sc, acc_sc, ybuf, gsem, ssem, *, tm, n_chunks, n_tiles, n_tok):
    i = pl.program_id(0)
    j = pl.program_id(1)
    valid = tr_ref[i] > 0
    rows_per_step = tm // n_chunks

    def gather_row(tile, q, rr):
        tok = flat_ref[(tile + 1) * tm + q * rows_per_step + rr] & (n_tok - 1)
        return pltpu.make_async_copy(h_hbm.at[pl.ds(tok, 1)], xstage.at[q, pl.ds(rr, 1)], gsem)

    def scatter_row(tile, q, rr):
        dst = flat_ref[(tile + 1) * tm + q * rows_per_step + rr]
        return pltpu.make_async_copy(ybuf.at[q, pl.ds(rr, 1)], ys_hbm.at[pl.ds(dst, 1)], ssem)

    def gather_wait(steps):
        for q in range(steps):
            pltpu.make_async_copy(h_hbm.at[pl.ds(0, rows_per_step)], xstage.at[q], gsem).wait()

    def scatter_wait(steps):
        for q in range(steps):
            pltpu.make_async_copy(ybuf.at[q], ys_hbm.at[pl.ds(0, rows_per_step)], ssem).wait()

    def all_rows(start_row):
        def one(q, _):
            for rr in range(rows_per_step):
                start_row(q, rr).start()
            return 0
        lax.fori_loop(0, n_chunks, one, 0)

    @pl.when(valid)
    def _():
        @pl.when(j == 0)
        def _():
            @pl.when(i == 0)
            def _():
                all_rows(functools.partial(gather_row, 0))
                gather_wait(n_chunks)
                ybuf[...] = jnp.zeros_like(ybuf)

            xb_sc[...] = xstage[...].reshape(tm, -1).astype(BF16)

        for rr in range(rows_per_step):
            gather_row(i + 1, j, rr).start()
            scatter_row(i - 1, j, rr).start()

        xb = xb_sc[...]
        g = jnp.dot(xb, wg_ref[0].astype(BF16), preferred_element_type=F32)
        u = jnp.dot(xb, wu_ref[0].astype(BF16), preferred_element_type=F32)
        a = (g * jax.nn.sigmoid(g) * u).astype(BF16)
        gather_wait(1)
        scatter_wait(1)
        y = jnp.dot(a, w2_ref[0].astype(BF16), preferred_element_type=F32)

        @pl.when(j == 0)
        def _():
            acc_sc[...] = y

        @pl.when(jnp.logical_and(j > 0, j < n_chunks - 1))
        def _():
            acc_sc[...] += y

        @pl.when(j == n_chunks - 1)
        def _():
            ybuf[...] = (acc_sc[...] + y).reshape(ybuf.shape)

    prv = jnp.maximum(i - 1, 0)

    @pl.when(jnp.logical_and(jnp.logical_and(jnp.logical_not(valid), j == 0),
                             jnp.logical_and(i > 0, tr_ref[prv] > 0)))
    def _():
        all_rows(functools.partial(scatter_row, prv))
        scatter_wait(n_chunks)


def _moe_ffn(h, flat, t_exp, t_rows, w13, w2, tm):
    n_tok, D = h.shape
    Fh = w2.shape[1]
    n_tiles = t_exp.shape[0]
    n_chunks = 4
    tf = Fh // n_chunks
    assert tf % LANES == 0 and tm % n_chunks == 0 and n_tok & (n_tok - 1) == 0 and n_tok >= tm

    def wmap(col_off):
        return lambda i, j, te, tr, fl: (te[i], 0, j * jnp.minimum(tr[i], 1) + col_off)

    return pl.pallas_call(
        functools.partial(_moe_ffn_body, tm=tm, n_chunks=n_chunks, n_tiles=n_tiles, n_tok=n_tok),
        grid_spec=pltpu.PrefetchScalarGridSpec(
            num_scalar_prefetch=3,
            grid=(n_tiles, n_chunks),
            in_specs=[pl.BlockSpec(memory_space=pl.ANY),
                      pl.BlockSpec((1, D, tf), wmap(0)),
                      pl.BlockSpec((1, D, tf), wmap(n_chunks)),
                      pl.BlockSpec((1, tf, D), lambda i, j, te, tr, fl: (te[i], j * jnp.minimum(tr[i], 1), 0))],
            out_specs=pl.BlockSpec(memory_space=pl.ANY),
            scratch_shapes=[pltpu.VMEM((n_chunks, tm // n_chunks, D), F32), pltpu.VMEM((tm, D), BF16),
                            pltpu.VMEM((tm, D), F32), pltpu.VMEM((n_chunks, tm // n_chunks, D), F32),
                            pltpu.SemaphoreType.DMA, pltpu.SemaphoreType.DMA]),
        out_shape=jax.ShapeDtypeStruct((TOP_K_EXPERTS * n_tok + tm, D), F32),
        compiler_params=_params(2),
        name="moe_expert_swiglu",
    )(t_exp, t_rows, flat, h, w13, w13, w2)


def _combine_body(x_ref, y0_ref, y1_ref, w_ref, gate_ref, fg_ref, o_ref, *, final_norm):
    w = w_ref[...]
    y = w[:, 0:1] * y0_ref[...] + w[:, 1:2] * y1_ref[...]
    out = x_ref[...] + gate_ref[0] * y
    if final_norm:
        ms = jnp.mean(out * out, axis=-1, keepdims=True)
        out = (out * lax.rsqrt(ms + EPS)) * fg_ref[...]
    o_ref[...] = out


def _moe_combine(x, ys, wts, gate, final_g, final_norm):
    B, S, D = x.shape
    n_tok = B * S
    rows = 512
    per_b = S // rows
    out = pl.pallas_call(
        functools.partial(_combine_body, final_norm=final_norm),
        grid=(n_tok // rows,),
        in_specs=[pl.BlockSpec((rows, D), lambda i: (i, 0)),
                  pl.BlockSpec((rows, D), lambda i: (i, 0)),
                  pl.BlockSpec((rows, D), lambda i: (n_tok // rows + i, 0)),
                  pl.BlockSpec((rows, TOP_K_EXPERTS), lambda i: (i, 0)),
                  pl.BlockSpec((1, 1, D), lambda i: (i // per_b, 0, 0)),
                  pl.BlockSpec((1, D), lambda i: (0, 0))],
        out_specs=pl.BlockSpec((rows, D), lambda i: (i, 0)),
        out_shape=jax.ShapeDtypeStruct((n_tok, D), F32),
        compiler_params=_params(1),
        name="moe_combine",
    )(x.reshape(n_tok, D), ys, ys, wts, gate, final_g.reshape(1, D))
    return out.reshape(B, S, D)


def _moe_layer(x, g, sh, sc, gate, router_w, w13, w2, first_expert, final_g, final_norm):
    B, S, D = x.shape
    n_tok = B * S
    tm = 1024
    n_tiles = TOP_K_EXPERTS * n_tok // tm + N_EXPERTS
    h, ei, wts = _router(x, g, sh, sc, router_w)
    flat, t_exp, t_rows = _route_plan(ei, tm, n_tiles)
    ys = _moe_ffn(h, flat, t_exp + first_expert, t_rows, w13, w2, tm)
    return _moe_combine(x, ys, wts.T, gate, final_g, final_norm)


def kernel(x, c, rel_bias, mod_w, mod_b, norm_mix_g, norm_ffn_g, a_wqkv, a_wo, kv_norm_g, kv_mod_w,
           kv_mod_b, b_wkv, b_wq, b_wo, ffn_w13, ffn_w2, router_w, moe_w13, moe_w2, final_norm_g):
    D = D_MODEL
    mods = _modulation(c, mod_w, mod_b)
    kv_mod = _modulation(c, kv_mod_w[None], kv_mod_b[None])[0]
    bias_tiles = _bias_tiles(rel_bias)
    k_sh = vt_sh = None
    for l in range(DEPTH):
        sh_a, sc_a, g_a, sh_f, sc_f, g_f = [mods[l][:, None, i * D:(i + 1) * D] for i in range(6)]
        if l < N_A_LAYERS:
            w = a_wqkv[l].astype(BF16)
            qk = _norm_mod_matmul(x, norm_mix_g[l], sh_a, sc_a, w[:, :2 * D])
            vt = _norm_mod_matmul_t(x, norm_mix_g[l], sh_a, sc_a, w[:, 2 * D:].T)
            attn = _moba_attention(qk, vt, bias_tiles)
            x = _matmul_gated_residual(attn, a_wo[l].astype(BF16), x, g_a)
        else:
            j = l - N_A_LAYERS
            if j == 0:
                w = b_wkv.astype(BF16)
                kv_sh, kv_sc = kv_mod[:, None, :D], kv_mod[:, None, D:]
                k_sh = _norm_mod_matmul(x, kv_norm_g, kv_sh, kv_sc, w[:, :D])
                vt_sh = _norm_mod_matmul_t(x, kv_norm_g, kv_sh, kv_sc, w[:, D:].T)
            q = _norm_mod_matmul(x, norm_mix_g[l], sh_a, sc_a, b_wq[j].astype(BF16))
            attn = _sb_attention(q, k_sh, vt_sh)
            x = _matmul_gated_residual(attn, b_wo[j].astype(BF16), x, g_a)
        if l % 2 == 0:
            x = _dense_ffn(x, norm_ffn_g[l], sh_f, sc_f, g_f,
                           ffn_w13[l // 2][None].astype(BF16), ffn_w2[l // 2][None].astype(BF16))
        else:
            x = _moe_layer(x, norm_ffn_g[l], sh_f, sc_f, g_f, router_w[l // 2],
                           moe_w13.reshape((-1,) + moe_w13.shape[2:]),
                           moe_w2.reshape((-1,) + moe_w2.shape[2:]), (l // 2) * N_EXPERTS,
                           final_norm_g, l == DEPTH - 1)
    return x
```

```python
import functools
import math

import numpy as np
import jax
import jax.numpy as jnp
from jax import lax
from jax.experimental import pallas as pl
from jax.experimental.pallas import tpu as pltpu

D_MODEL = 1024
DEPTH = 4
N_HEADS = 16
HEAD_DIM = D_MODEL // N_HEADS
N_A_LAYERS = DEPTH // 2
MOBA_BLOCK = 256
MOBA_TOPK = 3
N_BUCKETS = 32
MAX_DISTANCE = 1024
N_EXPERTS = 8
TOP_K_EXPERTS = 2
EPS = 1e-6
NEG = -1e30

LANES = 128
HEADS_PER_TILE = LANES // HEAD_DIM
ATT_BLOCK = 256
N_BIAS_TILES = 6
SB_DEAD_LOG = -105.0
VMEM_LIMIT = 56 * 1024 * 1024

F32 = jnp.float32
BF16 = jnp.bfloat16
HIGHEST = lax.Precision.HIGHEST
_NT = (((1,), (1,)), ((), ()))


def _params(n_axes):
    return pltpu.CompilerParams(dimension_semantics=("arbitrary",) * n_axes,
                                vmem_limit_bytes=VMEM_LIMIT)


def _norm_mod(x, g, sh, sc):
    ms = jnp.mean(x * x, axis=-1, keepdims=True)
    y = (x * lax.rsqrt(ms + EPS)) * g
    return y * (1.0 + sc) + sh


def _mod_body(c_ref, w_ref, b_ref, o_ref):
    c = c_ref[...]
    s = c * jax.nn.sigmoid(c)
    o_ref[0] = jnp.dot(s, w_ref[0], preferred_element_type=F32, precision=HIGHEST) + b_ref[0]


def _modulation(c, w, b):
    L, D, M = w.shape
    B = c.shape[0]
    tn = 1024
    return pl.pallas_call(
        _mod_body,
        grid=(L, M // tn),
        in_specs=[pl.BlockSpec((B, D), lambda l, j: (0, 0)),
                  pl.BlockSpec((1, D, tn), lambda l, j: (l, 0, j)),
                  pl.BlockSpec((1, 1, tn), lambda l, j: (l, 0, j))],
        out_specs=pl.BlockSpec((1, B, tn), lambda l, j: (l, 0, j)),
        out_shape=jax.ShapeDtypeStruct((L, B, M), F32),
        compiler_params=_params(2),
        name="adaln_mod",
    )(c, w, b.reshape(L, 1, M))


def _nm_mm_body(x_ref, g_ref, sh_ref, sc_ref, w_ref, o_ref, h_sc):
    @pl.when(pl.program_id(2) == 0)
    def _():
        h_sc[...] = _norm_mod(x_ref[0], g_ref[...], sh_ref[0], sc_ref[0]).astype(BF16)

    o_ref[0] = jnp.dot(h_sc[...], w_ref[...], preferred_element_type=F32).astype(o_ref.dtype)


def _norm_mod_matmul(x, g, sh, sc, w):
    B, S, D = x.shape
    N = w.shape[1]
    tm, tn = 1024, 1024
    return pl.pallas_call(
        _nm_mm_body,
        grid=(B, S // tm, N // tn),
        in_specs=[pl.BlockSpec((1, tm, D), lambda b, i, j: (b, i, 0)),
                  pl.BlockSpec((1, D), lambda b, i, j: (0, 0)),
                  pl.BlockSpec((1, 1, D), lambda b, i, j: (b, 0, 0)),
                  pl.BlockSpec((1, 1, D), lambda b, i, j: (b, 0, 0)),
                  pl.BlockSpec((D, tn), lambda b, i, j: (0, j))],
        out_specs=pl.BlockSpec((1, tm, tn), lambda b, i, j: (b, i, j)),
        out_shape=jax.ShapeDtypeStruct((B, S, N), BF16),
        scratch_shapes=[pltpu.VMEM((tm, D), BF16)],
        compiler_params=_params(3),
        name="norm_mod_proj",
    )(x, g.reshape(1, D), sh, sc, w)


def _nm_mm_t_body(x_ref, g_ref, sh_ref, sc_ref, wt_ref, o_ref, h_sc):
    @pl.when(pl.program_id(2) == 0)
    def _():
        h_sc[...] = _norm_mod(x_ref[0], g_ref[...], sh_ref[0], sc_ref[0]).astype(BF16)

    res = lax.dot_general(wt_ref[...], h_sc[...], _NT, preferred_element_type=F32)
    for r in range(o_ref.shape[1]):
        o_ref[0, r] = res[:, r * ATT_BLOCK:(r + 1) * ATT_BLOCK].astype(o_ref.dtype)


def _norm_mod_matmul_t(x, g, sh, sc, wt):
    B, S, D = x.shape
    N = wt.shape[0]
    tm, tn = 1024, 1024
    T = ATT_BLOCK
    return pl.pallas_call(
        _nm_mm_t_body,
        grid=(B, S // tm, N // tn),
        in_specs=[pl.BlockSpec((1, tm, D), lambda b, i, j: (b, i, 0)),
                  pl.BlockSpec((1, D), lambda b, i, j: (0, 0)),
                  pl.BlockSpec((1, 1, D), lambda b, i, j: (b, 0, 0)),
                  pl.BlockSpec((1, 1, D), lambda b, i, j: (b, 0, 0)),
                  pl.BlockSpec((tn, D), lambda b, i, j: (j, 0))],
        out_specs=pl.BlockSpec((1, tm // T, tn, T), lambda b, i, j: (b, i, j, 0)),
        out_shape=jax.ShapeDtypeStruct((B, S // T, N, T), BF16),
        scratch_shapes=[pltpu.VMEM((tm, D), BF16)],
        compiler_params=_params(3),
        name="norm_mod_proj_t",
    )(x, g.reshape(1, D), sh, sc, wt)


def _mm_res_body(a_ref, w_ref, r_ref, gate_ref, o_ref):
    acc = jnp.dot(a_ref[0], w_ref[...], preferred_element_type=F32)
    o_ref[0] = r_ref[0] + gate_ref[0] * acc


def _matmul_gated_residual(a, w, res, gate):
    B, S, K = a.shape
    N = w.shape[1]
    tm, tn = 1024, 1024
    return pl.pallas_call(
        _mm_res_body,
        grid=(B, S // tm, N // tn),
        in_specs=[pl.BlockSpec((1, tm, K), lambda b, i, j: (b, i, 0)),
                  pl.BlockSpec((K, tn), lambda b, i, j: (0, j)),
                  pl.BlockSpec((1, tm, tn), lambda b, i, j: (b, i, j)),
                  pl.BlockSpec((1, 1, tn), lambda b, i, j: (b, 0, j))],
        out_specs=pl.BlockSpec((1, tm, tn), lambda b, i, j: (b, i, j)),
        out_shape=jax.ShapeDtypeStruct((B, S, N), F32),
        compiler_params=_params(3),
        name="out_proj_residual",
    )(a, w, res, gate)


def _pv_t(vt, ps, head_row):
    zeros = jnp.zeros_like(vt)
    vcat = jnp.concatenate(
        [jnp.where((head_row >= hh * HEAD_DIM) & (head_row < (hh + 1) * HEAD_DIM), vt, zeros)
         for hh in range(HEADS_PER_TILE)], axis=1)
    return jnp.dot(vcat, jnp.concatenate(ps, axis=0), preferred_element_type=F32)


def _per_head_rows(head_row, vals):
    assert HEADS_PER_TILE == 2
    return jnp.where(head_row < HEAD_DIM, vals[0], vals[1])


def _head_queries(q, lane):
    out = []
    for hh in range(HEADS_PER_TILE):
        in_head = (lane >= hh * HEAD_DIM) & (lane < (hh + 1) * HEAD_DIM)
        out.append(jnp.where(in_head, q, jnp.zeros_like(q)))
    return out


def _rel_bucket(dist):
    max_exact = N_BUCKETS // 2
    d = np.maximum(dist, 1).astype(np.float32)
    ratio = np.log(d / np.float32(max_exact)) / np.float32(math.log(MAX_DISTANCE / max_exact))
    large = max_exact + (ratio * np.float32(N_BUCKETS - max_exact)).astype(np.int32)
    large = np.minimum(large, N_BUCKETS - 1)
    return np.where(dist < max_exact, dist, large)


def _bias_tiles(rel_bias):
    T = ATT_BLOCK
    W = 2 * T
    H = rel_bias.shape[1]
    k = np.arange(W)
    lag = np.where(k <= T, k, k - W)
    dist = np.maximum(np.arange(N_BIAS_TILES)[:, None] * T + lag[None, :], 0)
    bucket = _rel_bucket(dist)
    assert (bucket[N_BIAS_TILES - 1] == N_BUCKETS - 1).all()
    onehot = np.eye(N_BUCKETS, dtype=np.float32)[bucket.reshape(-1)]
    vec = jnp.dot(jnp.asarray(onehot), rel_bias, precision=HIGHEST)
    a = vec.T.reshape(H, N_BIAS_TILES, W)
    tiled = jnp.tile(a, (1, 1, T))
    return tiled[:, :, :T * (W - 1)].reshape(H, N_BIAS_TILES, T, W - 1)[:, :, :, :T]


MOBA_TILES_PER_STEP = 2


def _moba_body(q_ref, k_ref, vt_ref, bias_ref, o_ref, kmean_sc, sel_sc, s_sc, p_sc, *, nb):
    step = pl.program_id(2)
    T = ATT_BLOCK
    tiles = range(MOBA_TILES_PER_STEP)
    heads = range(HEADS_PER_TILE)
    assert MOBA_TILES_PER_STEP == 2
    c0 = step * MOBA_TILES_PER_STEP
    cs = [c0 + u for u in tiles]

    @pl.when(step == 0)
    def _():
        rest = jnp.concatenate(
            [jnp.sum(k_ref[0, n * T:(n + 1) * T, :].astype(F32), axis=0, keepdims=True) * (1.0 / T)
             for n in range(nb)], axis=0)
        for part in range(3):
            term = rest.astype(BF16)
            kmean_sc[part * nb:(part + 1) * nb, :] = term
            rest = rest - term.astype(F32)

    lane = lax.broadcasted_iota(jnp.int32, (T, LANES), 1)
    blk = lax.broadcasted_iota(jnp.int32, (nb, T), 0)
    key_i = lax.broadcasted_iota(jnp.int32, (T, T), 0)
    qry_i = lax.broadcasted_iota(jnp.int32, (T, T), 1)
    causal = key_i <= qry_i
    head_row = lax.broadcasted_iota(jnp.int32, (LANES, T), 0)
    kmean = kmean_sc[...]
    scale = HEAD_DIM ** -0.5

    qs = []
    for u in tiles:
        qs.append([])
        for hh, qh in enumerate(_head_queries(q_ref[0, u * T:(u + 1) * T, :], lane)):
            terms = lax.dot_general(kmean, qh, _NT, preferred_element_type=F32)
            gate = terms[:nb] + terms[nb:2 * nb] + terms[2 * nb:]
            gate = jnp.where(blk < cs[u], gate, NEG)
            rank = jnp.zeros((nb, T), jnp.int32)
            for m in range(nb):
                gm = gate[m:m + 1, :]
                beats = (gm > gate) | ((gm == gate) & (blk > m))
                rank = rank + beats.astype(jnp.int32)
            sel_sc[u, hh] = ((rank < MOBA_TOPK) & (blk < cs[u])).astype(F32)
            qs[u].append((qh.astype(F32) * scale).astype(BF16))

    def scores(n, which=tiles):
        kb = k_ref[0, pl.ds(pl.multiple_of(n * T, T), T), :]
        return {u: [lax.dot_general(kb, qs[u][hh], _NT, preferred_element_type=F32) for hh in heads]
                for u in which}

    acc_rows = LANES + 16
    acc_row = lax.broadcasted_iota(jnp.int32, (acc_rows, T), 0)
    first_head = (acc_row < HEAD_DIM) | ((acc_row >= LANES) & (acc_row < LANES + 8))
    grp = lax.broadcasted_iota(jnp.int32, (16, T), 0) // 8
    ones_rows = [(grp == hh).astype(BF16) for hh in heads]

    def soften(u, n, raw, ms):
        dl = jnp.minimum(cs[u] - n, N_BIAS_TILES - 1)
        new_ms, alphas = [], []
        for hh in heads:
            m = ms[hh]
            picked = sel_sc[u, hh, pl.ds(n, 1), :] > 0.5
            s = raw[hh] + bias_ref[hh, pl.ds(dl, 1)][0]
            m_new = jnp.where(picked, jnp.maximum(m, jnp.max(s, axis=0, keepdims=True)), m)
            p_sc[u, hh] = jnp.exp(s - jnp.where(picked, m_new, -NEG)).astype(BF16)
            new_ms.append(m_new)
            alphas.append(jnp.exp(m - m_new))
        return new_ms, alphas

    def value_rows(blk_idx):
        vt = vt_ref[0, blk_idx]
        zeros = jnp.zeros_like(vt)
        return jnp.concatenate(
            [jnp.concatenate([jnp.where((head_row >= hh * HEAD_DIM) & (head_row < (hh + 1) * HEAD_DIM),
                                        vt, zeros), ones_rows[hh]], axis=0) for hh in heads],
            axis=1)

    def apply_pending(u, lhs, alphas, acc):
        ps = jnp.concatenate([p_sc[u, hh] for hh in heads], axis=0)
        return (jnp.where(first_head, alphas[0], alphas[1]) * acc
                + jnp.dot(lhs, ps, preferred_element_type=F32))

    def own_block(u, raw):
        ms = []
        for hh in heads:
            s = jnp.where(causal, raw[hh] + bias_ref[hh, 0], NEG)
            m = jnp.max(s, axis=0, keepdims=True)
            p_sc[u, hh] = jnp.exp(s - m).astype(BF16)
            ms.append(m)
        return ms

    raw_own = {u: scores(cs[u], [u])[u] for u in tiles}
    raw_extra = scores(c0, [1])[1]
    raw_next = scores(0)
    for u in tiles:
        for hh in heads:
            s_sc[u, hh] = raw_next[u][hh]

    ones = jnp.ones((1, T), F32)
    zero_acc = jnp.zeros((acc_rows, T), F32)
    ms = {u: own_block(u, raw_own[u]) for u in tiles}
    alphas = {0: [ones, ones]}
    accs = {0: zero_acc}
    accs[1] = apply_pending(1, value_rows(cs[1]), [ones, ones], zero_acc)
    ms[1], alphas[1] = soften(1, c0, raw_extra, ms[1])

    def pack(ms, alphas, accs):
        return tuple(ms[0]) + tuple(ms[1]) + tuple(alphas[0]) + tuple(alphas[1]) + (accs[0], accs[1])

    def unpack(carry):
        return ({0: list(carry[0:2]), 1: list(carry[2:4])}, {0: list(carry[4:6]), 1: list(carry[6:8])},
                {0: carry[8], 1: carry[9]})

    def body(t, carry):
        ms, alphas, accs = unpack(carry)
        raw_t = {u: [s_sc[u, hh] for hh in heads] for u in tiles}
        lhs = value_rows(jnp.where(t == 0, c0, t - 1))
        for u in tiles:
            accs[u] = apply_pending(u, lhs, alphas[u], accs[u])
        raw_n = scores(jnp.minimum(t + 1, c0 - 1))
        for u in tiles:
            ms[u], alphas[u] = soften(u, t, raw_t[u], ms[u])
        for u in tiles:
            for hh in heads:
                s_sc[u, hh] = raw_n[u][hh]
        return pack(ms, alphas, accs)

    ms, alphas, accs = unpack(lax.fori_loop(
        0, step, lambda i, carry: body(2 * i + 1, body(2 * i, carry)), pack(ms, alphas, accs)))
    lhs = value_rows(jnp.maximum(c0 - 1, 0))
    for u in tiles:
        acc = apply_pending(u, lhs, alphas[u], accs[u])
        norm = _per_head_rows(head_row, [acc[LANES:LANES + 1], acc[LANES + 8:LANES + 9]])
        o_ref[0, u * T:(u + 1) * T, :] = (acc[:LANES] / norm).T.astype(o_ref.dtype)


def _moba_attention(qk, vt, bias_tiles):
    B, S, D2 = qk.shape
    D = D2 // 2
    T = ATT_BLOCK
    nb = S // T
    n_hp = D // LANES
    tq = MOBA_TILES_PER_STEP * T
    return pl.pallas_call(
        functools.partial(_moba_body, nb=nb),
        grid=(B, n_hp, S // tq),
        in_specs=[pl.BlockSpec((1, tq, LANES), lambda b, h, i: (b, i, h)),
                  pl.BlockSpec((1, S, LANES), lambda b, h, i: (b, 0, n_hp + h)),
                  pl.BlockSpec((1, nb, LANES, T), lambda b, h, i: (b, 0, h, 0)),
                  pl.BlockSpec((HEADS_PER_TILE, N_BIAS_TILES, T, T), lambda b, h, i: (h, 0, 0, 0))],
        out_specs=pl.BlockSpec((1, tq, LANES), lambda b, h, i: (b, i, h)),
        out_shape=jax.ShapeDtypeStruct((B, S, D), BF16),
        scratch_shapes=[pltpu.VMEM((3 * nb, LANES), BF16),
                        pltpu.VMEM((MOBA_TILES_PER_STEP, HEADS_PER_TILE, nb, T), F32),
                        pltpu.VMEM((MOBA_TILES_PER_STEP, HEADS_PER_TILE, T, T), F32),
                        pltpu.VMEM((MOBA_TILES_PER_STEP, HEADS_PER_TILE, T, T), BF16)],
        compiler_params=_params(3),
        name="moba_attention",
    )(qk, qk, vt, bias_tiles)


SB_TILES_PER_STEP = 4


def _sb_body(q_ref, k_ref, vt_ref, o_ref):
    step = pl.program_id(2)
    T = ATT_BLOCK
    lane = lax.broadcasted_iota(jnp.int32, (T, LANES), 1)
    key_i = lax.broadcasted_iota(jnp.int32, (T, T), 0)
    qry_i = lax.broadcasted_iota(jnp.int32, (T, T), 1)
    strict = key_i < qry_i
    later = (qry_i > key_i).astype(BF16)
    later2 = jnp.concatenate([later, later], axis=1)
    head_row = lax.broadcasted_iota(jnp.int32, (LANES, T), 0)
    scale = HEAD_DIM ** -0.5
    cs = [step * SB_TILES_PER_STEP + u for u in range(SB_TILES_PER_STEP)]
    qs = [[(qh.astype(F32) * scale).astype(BF16)
           for qh in _head_queries(q_ref[0, u * T:(u + 1) * T, :], lane)] for u in range(SB_TILES_PER_STEP)]

    def run(jobs, states):
        rs = {u: list(st[:HEADS_PER_TILE]) for u, st in states.items()}
        accs = {u: st[-1] for u, st in states.items()}
        items = [(ji, hh) for ji in range(len(jobs)) for hh in range(HEADS_PER_TILE)]
        kbs = [k_ref[0, pl.ds(pl.multiple_of(n * T, T), T), :] for _, n, _, _ in jobs]
        z = {(ji, hh): lax.dot_general(kbs[ji], qs[jobs[ji][0]][hh], _NT, preferred_element_type=F32)
             for ji, hh in items}
        lb, l1, hilo = {}, {}, {}
        for t in items:
            sp = jnp.log(1.0 + jnp.exp(-jnp.abs(z[t])))
            lb[t] = jnp.minimum(z[t], 0.0) - sp
            l1[t] = lb[t] - z[t]
            if jobs[t[0]][2]:
                l1[t] = jnp.where(strict, l1[t], 0.0)
            hi = l1[t].astype(BF16)
            lo = (l1[t] - hi.astype(F32)).astype(BF16)
            hilo[t] = jnp.concatenate([hi, lo], axis=0)
        tail = {t: jnp.dot(later2, hilo[t], preferred_element_type=F32) for t in items}
        ps = {}
        for ji, hh in items:
            u, _, diag, keep = jobs[ji]
            a = jnp.exp(lb[ji, hh] + tail[ji, hh] + rs[u][hh])
            if diag:
                a = jnp.where(strict, a, 0.0)
            ps[ji, hh] = a.astype(BF16)
            inc = tail[ji, hh][0:1, :] + l1[ji, hh][0:1, :]
            if keep is not None:
                inc = jnp.where(keep, inc, 0.0)
            rs[u][hh] = rs[u][hh] + inc
        for ji, (u, n, _, keep) in enumerate(jobs):
            vt = vt_ref[0, n]
            if keep is not None:
                vt = jnp.where(keep, vt, jnp.zeros_like(vt))
            accs[u] = accs[u] + _pv_t(vt, [ps[ji, hh] for hh in range(HEADS_PER_TILE)], head_row)
        return {u: tuple(rs[u]) + (accs[u],) for u in states}

    zero = (jnp.zeros((1, T), F32),) * HEADS_PER_TILE + (jnp.zeros((LANES, T), F32),)
    jobs = []
    for u, c in enumerate(cs):
        jobs += [(u, c, True, None), (u, jnp.maximum(c - 1, 0), False, (c > 0) if u == 0 else None)]
    states = run(jobs, {u: zero for u in range(SB_TILES_PER_STEP)})

    for u, c in enumerate(cs):
        def cond(carry, c=c):
            live = jnp.max(jnp.maximum(carry[1], carry[2])) > SB_DEAD_LOG
            return jnp.logical_and(carry[0] < c, live)

        def body(carry, u=u, c=c):
            i = carry[0]
            return (i + 1,) + run([(u, c - 1 - i, False, None)], {u: carry[1:]})[u]

        fin = lax.while_loop(cond, body, (jnp.int32(1),) + states[u])
        o_ref[0, u * T:(u + 1) * T, :] = fin[-1].T.astype(o_ref.dtype)


def _sb_attention(q, k, vt):
    B, S, D = q.shape
    T = ATT_BLOCK
    nb = S // T
    n_hp = D // LANES
    tq = SB_TILES_PER_STEP * T
    return pl.pallas_call(
        _sb_body,
        grid=(B, n_hp, S // tq),
        in_specs=[pl.BlockSpec((1, tq, LANES), lambda b, h, i: (b, i, h)),
                  pl.BlockSpec((1, S, LANES), lambda b, h, i: (b, 0, h)),
                  pl.BlockSpec((1, nb, LANES, T), lambda b, h, i: (b, 0, h, 0))],
        out_specs=pl.BlockSpec((1, tq, LANES), lambda b, h, i: (b, i, h)),
        out_shape=jax.ShapeDtypeStruct((B, S, D), BF16),
        compiler_params=_params(3),
        name="stick_breaking_attention",
    )(q, k, vt)


def _swiglu_chunk(xb, wg, wu, w2):
    g = jnp.dot(xb, wg, preferred_element_type=F32)
    u = jnp.dot(xb, wu, preferred_element_type=F32)
    a = (g * jax.nn.sigmoid(g) * u).astype(BF16)
    return jnp.dot(a, w2, preferred_element_type=F32)


def _dense_ffn_body(x_ref, g_ref, sh_ref, sc_ref, gate_ref, wg_ref, wu_ref, w2_ref, o_ref,
                    xb_sc, acc_sc, *, n_chunks):
    j = pl.program_id(1)

    @pl.when(j == 0)
    def _():
        xb_sc[...] = _norm_mod(x_ref[...], g_ref[...], sh_ref[0], sc_ref[0]).astype(BF16)

    y = _swiglu_chunk(xb_sc[...], wg_ref[0], wu_ref[0], w2_ref[0])

    @pl.when(j == 0)
    def _():
        acc_sc[...] = y

    @pl.when(j > 0)
    def _():
        acc_sc[...] += y

    @pl.when(j == n_chunks - 1)
    def _():
        o_ref[...] = x_ref[...] + gate_ref[0] * acc_sc[...]


def _dense_ffn(x, g, sh, sc, gate, w13, w2):
    B, S, D = x.shape
    Fh = w2.shape[1]
    tm, tf = 512, 1408
    n_chunks = Fh // tf
    per_b = S // tm
    xf = x.reshape(B * S, D)
    out = pl.pallas_call(
        functools.partial(_dense_ffn_body, n_chunks=n_chunks),
        grid=(B * S // tm, n_chunks),
        in_specs=[pl.BlockSpec((tm, D), lambda i, j: (i, 0)),
                  pl.BlockSpec((1, D), lambda i, j: (0, 0)),
                  pl.BlockSpec((1, 1, D), lambda i, j: (i // per_b, 0, 0)),
                  pl.BlockSpec((1, 1, D), lambda i, j: (i // per_b, 0, 0)),
                  pl.BlockSpec((1, 1, D), lambda i, j: (i // per_b, 0, 0)),
                  pl.BlockSpec((1, D, tf), lambda i, j: (0, 0, j)),
                  pl.BlockSpec((1, D, tf), lambda i, j: (0, 0, j + n_chunks)),
                  pl.BlockSpec((1, tf, D), lambda i, j: (0, j, 0))],
        out_specs=pl.BlockSpec((tm, D), lambda i, j: (i, 0)),
        out_shape=jax.ShapeDtypeStruct((B * S, D), F32),
        scratch_shapes=[pltpu.VMEM((tm, D), BF16), pltpu.VMEM((tm, D), F32)],
        compiler_params=_params(2),
        name="dense_swiglu",
    )(xf, g.reshape(1, D), sh, sc, gate, w13, w13, w2)
    return out.reshape(B, S, D)


def _router_body(x_ref, g_ref, sh_ref, sc_ref, rwt_ref, h_ref, ei_ref, w_ref):
    h = _norm_mod(x_ref[0], g_ref[...], sh_ref[0], sc_ref[0])
    h_ref[...] = h
    logits = lax.dot_general(rwt_ref[...], h, _NT, preferred_element_type=F32,
                             precision=HIGHEST)
    n_e = logits.shape[0]
    eio = lax.broadcasted_iota(jnp.int32, logits.shape, 0)
    m1 = jnp.max(logits, axis=0, keepdims=True)
    i1 = jnp.min(jnp.where(logits == m1, eio, n_e), axis=0, keepdims=True)
    rest = jnp.where(eio == i1, -jnp.inf, logits)
    m2 = jnp.max(rest, axis=0, keepdims=True)
    i2 = jnp.min(jnp.where(rest == m2, eio, n_e), axis=0, keepdims=True)
    t = jnp.exp(m2 - m1)
    den = 1.0 + t
    ei_ref[...] = jnp.concatenate([i1, i2], axis=0)
    w_ref[...] = jnp.concatenate([1.0 / den, t / den], axis=0)


def _router(x, g, sh, sc, router_w):
    B, S, D = x.shape
    E = router_w.shape[1]
    tm = 512
    per_b = S // tm
    return pl.pallas_call(
        _router_body,
        grid=(B, per_b),
        in_specs=[pl.BlockSpec((1, tm, D), lambda b, i: (b, i, 0)),
                  pl.BlockSpec((1, D), lambda b, i: (0, 0)),
                  pl.BlockSpec((1, 1, D), lambda b, i: (b, 0, 0)),
                  pl.BlockSpec((1, 1, D), lambda b, i: (b, 0, 0)),
                  pl.BlockSpec((E, D), lambda b, i: (0, 0))],
        out_specs=[pl.BlockSpec((tm, D), lambda b, i: (b * per_b + i, 0)),
                   pl.BlockSpec((TOP_K_EXPERTS, tm), lambda b, i: (0, b * per_b + i)),
                   pl.BlockSpec((TOP_K_EXPERTS, tm), lambda b, i: (0, b * per_b + i))],
        out_shape=[jax.ShapeDtypeStruct((B * S, D), F32),
                   jax.ShapeDtypeStruct((TOP_K_EXPERTS, B * S), jnp.int32),
                   jax.ShapeDtypeStruct((TOP_K_EXPERTS, B * S), F32)],
        compiler_params=_params(2),
        name="moe_router",
    )(x, g.reshape(1, D), sh, sc, router_w.T)


def _route_plan(ei, tm, n_tiles):
    n_src = TOP_K_EXPERTS * ei.shape[1]
    assert n_src & (n_src - 1) == 0
    i32 = jnp.int32
    keys = jnp.sort(ei.reshape(-1) * n_src + jnp.arange(n_src, dtype=i32))
    runs = jnp.searchsorted(keys, jnp.arange(N_EXPERTS + 1, dtype=i32) * n_src).astype(i32)
    first, counts = runs[:-1], runs[1:] - runs[:-1]
    padded = ((counts + tm - 1) // tm) * tm
    ends = jnp.cumsum(padded)
    offs = ends - padded
    starts = jnp.arange(n_tiles, dtype=i32) * tm
    t_exp = jnp.minimum(jnp.searchsorted(ends, starts, side="right"), N_EXPERTS - 1).astype(i32)
    t_rows = jnp.clip(offs[t_exp] + counts[t_exp] - starts, 0, tm)
    t_rows = jnp.where(starts < ends[-1], t_rows, 0).astype(i32)
    r = jnp.arange(tm, dtype=i32)[None, :]
    nth = (first[t_exp] + starts - offs[t_exp])[:, None] + r
    real = r < t_rows[:, None]
    body = jnp.where(real, jnp.take(keys, jnp.clip(nth, 0, n_src - 1)) & (n_src - 1), n_src + r)
    flat = jnp.concatenate([n_src + r[0], body.reshape(-1)])
    return flat, t_exp, t_rows


def _moe_ffn_body(te_ref, tr_ref, flat_ref, h_hbm, wg_ref, wu_ref, w2_ref, ys_hbm,
                  xstage, xb_sc, acc_sc, ybuf, gsem, ssem, *, tm, n_chunks, n_tiles, n_tok):
    i = pl.program_id(0)
    j = pl.program_id(1)
    valid = tr_ref[i] > 0
    rows_per_step = tm // n_chunks

    def gather_row(tile, q, rr):
        tok = flat_ref[(tile + 1) * tm + q * rows_per_step + rr] & (n_tok - 1)
        return pltpu.make_async_copy(h_hbm.at[pl.ds(tok, 1)], xstage.at[q, pl.ds(rr, 1)], gsem)

    def scatter_row(tile, q, rr):
        dst = flat_ref[(tile + 1) * tm + q * rows_per_step + rr]
        return pltpu.make_async_copy(ybuf.at[q, pl.ds(rr, 1)], ys_hbm.at[pl.ds(dst, 1)], ssem)

    def gather_wait(steps):
        for q in range(steps):
            pltpu.make_async_copy(h_hbm.at[pl.ds(0, rows_per_step)], xstage.at[q], gsem).wait()

    def scatter_wait(steps):
        for q in range(steps):
            pltpu.make_async_copy(ybuf.at[q], ys_hbm.at[pl.ds(0, rows_per_step)], ssem).wait()

    def all_rows(start_row):
        def one(q, _):
            for rr in range(rows_per_step):
                start_row(q, rr).start()
            return 0
        lax.fori_loop(0, n_chunks, one, 0)

    @pl.when(valid)
    def _():
        @pl.when(j == 0)
        def _():
            @pl.when(i == 0)
            def _():
                all_rows(functools.partial(gather_row, 0))
                gather_wait(n_chunks)
                ybuf[...] = jnp.zeros_like(ybuf)

            xb_sc[...] = xstage[...].reshape(tm, -1).astype(BF16)

        for rr in range(rows_per_step):
            gather_row(i + 1, j, rr).start()
            scatter_row(i - 1, j, rr).start()

        xb = xb_sc[...]
        g = jnp.dot(xb, wg_ref[0].astype(BF16), preferred_element_type=F32)
        u = jnp.dot(xb, wu_ref[0].astype(BF16), preferred_element_type=F32)
        a = (g * jax.nn.sigmoid(g) * u).astype(BF16)
        gather_wait(1)
        scatter_wait(1)
        y = jnp.dot(a, w2_ref[0].astype(BF16), preferred_element_type=F32)

        @pl.when(j == 0)
        def _():
            acc_sc[...] = y

        @pl.when(jnp.logical_and(j > 0, j < n_chunks - 1))
        def _():
            acc_sc[...] += y

        @pl.when(j == n_chunks - 1)
        def _():
            ybuf[...] = (acc_sc[...] + y).reshape(ybuf.shape)

    prv = jnp.maximum(i - 1, 0)

    @pl.when(jnp.logical_and(jnp.logical_and(jnp.logical_not(valid), j == 0),
                             jnp.logical_and(i > 0, tr_ref[prv] > 0)))
    def _():
        all_rows(functools.partial(scatter_row, prv))
        scatter_wait(n_chunks)


def _moe_ffn(h, flat, t_exp, t_rows, w13, w2, tm):
    n_tok, D = h.shape
    Fh = w2.shape[1]
    n_tiles = t_exp.shape[0]
    n_chunks = 4
    tf = Fh // n_chunks
    assert tf % LANES == 0 and tm % n_chunks == 0 and n_tok & (n_tok - 1) == 0 and n_tok >= tm

    def wmap(col_off):
        return lambda i, j, te, tr, fl: (te[i], 0, j * jnp.minimum(tr[i], 1) + col_off)

    return pl.pallas_call(
        functools.partial(_moe_ffn_body, tm=tm, n_chunks=n_chunks, n_tiles=n_tiles, n_tok=n_tok),
        grid_spec=pltpu.PrefetchScalarGridSpec(
            num_scalar_prefetch=3,
            grid=(n_tiles, n_chunks),
            in_specs=[pl.BlockSpec(memory_space=pl.ANY),
                      pl.BlockSpec((1, D, tf), wmap(0)),
                      pl.BlockSpec((1, D, tf), wmap(n_chunks)),
                      pl.BlockSpec((1, tf, D), lambda i, j, te, tr, fl: (te[i], j * jnp.minimum(tr[i], 1), 0))],
            out_specs=pl.BlockSpec(memory_space=pl.ANY),
            scratch_shapes=[pltpu.VMEM((n_chunks, tm // n_chunks, D), F32), pltpu.VMEM((tm, D), BF16),
                            pltpu.VMEM((tm, D), F32), pltpu.VMEM((n_chunks, tm // n_chunks, D), F32),
                            pltpu.SemaphoreType.DMA, pltpu.SemaphoreType.DMA]),
        out_shape=jax.ShapeDtypeStruct((TOP_K_EXPERTS * n_tok + tm, D), F32),
        compiler_params=_params(2),
        name="moe_expert_swiglu",
    )(t_exp, t_rows, flat, h, w13, w13, w2)


def _combine_body(x_ref, y0_ref, y1_ref, w_ref, gate_ref, fg_ref, o_ref, *, final_norm):
    w = w_ref[...]
    y = w[:, 0:1] * y0_ref[...] + w[:, 1:2] * y1_ref[...]
    out = x_ref[...] + gate_ref[0] * y
    if final_norm:
        ms = jnp.mean(out * out, axis=-1, keepdims=True)
        out = (out * lax.rsqrt(ms + EPS)) * fg_ref[...]
    o_ref[...] = out


def _moe_combine(x, ys, wts, gate, final_g, final_norm):
    B, S, D = x.shape
    n_tok = B * S
    rows = 512
    per_b = S // rows
    out = pl.pallas_call(
        functools.partial(_combine_body, final_norm=final_norm),
        grid=(n_tok // rows,),
        in_specs=[pl.BlockSpec((rows, D), lambda i: (i, 0)),
                  pl.BlockSpec((rows, D), lambda i: (i, 0)),
                  pl.BlockSpec((rows, D), lambda i: (n_tok // rows + i, 0)),
                  pl.BlockSpec((rows, TOP_K_EXPERTS), lambda i: (i, 0)),
                  pl.BlockSpec((1, 1, D), lambda i: (i // per_b, 0, 0)),
                  pl.BlockSpec((1, D), lambda i: (0, 0))],
        out_specs=pl.BlockSpec((rows, D), lambda i: (i, 0)),
        out_shape=jax.ShapeDtypeStruct((n_tok, D), F32),
        compiler_params=_params(1),
        name="moe_combine",
    )(x.reshape(n_tok, D), ys, ys, wts, gate, final_g.reshape(1, D))
    return out.reshape(B, S, D)


def _moe_layer(x, g, sh, sc, gate, router_w, w13, w2, first_expert, final_g, final_norm):
    B, S, D = x.shape
    n_tok = B * S
    tm = 1024
    n_tiles = TOP_K_EXPERTS * n_tok // tm + N_EXPERTS
    h, ei, wts = _router(x, g, sh, sc, router_w)
    flat, t_exp, t_rows = _route_plan(ei, tm, n_tiles)
    ys = _moe_ffn(h, flat, t_exp + first_expert, t_rows, w13, w2, tm)
    return _moe_combine(x, ys, wts.T, gate, final_g, final_norm)


def kernel(x, c, rel_bias, mod_w, mod_b, norm_mix_g, norm_ffn_g, a_wqkv, a_wo, kv_norm_g, kv_mod_w,
           kv_mod_b, b_wkv, b_wq, b_wo, ffn_w13, ffn_w2, router_w, moe_w13, moe_w2, final_norm_g):
    D = D_MODEL
    mods = _modulation(c, mod_w, mod_b)
    kv_mod = _modulation(c, kv_mod_w[None], kv_mod_b[None])[0]
    bias_tiles = _bias_tiles(rel_bias)
    k_sh = vt_sh = None
    for l in range(DEPTH):
        sh_a, sc_a, g_a, sh_f, sc_f, g_f = [mods[l][:, None, i * D:(i + 1) * D] for i in range(6)]
        if l < N_A_LAYERS:
            w = a_wqkv[l].astype(BF16)
            qk = _norm_mod_matmul(x, norm_mix_g[l], sh_a, sc_a, w[:, :2 * D])
            vt = _norm_mod_matmul_t(x, norm_mix_g[l], sh_a, sc_a, w[:, 2 * D:].T)
            attn = _moba_attention(qk, vt, bias_tiles)
            x = _matmul_gated_residual(attn, a_wo[l].astype(BF16), x, g_a)
        else:
            j = l - N_A_LAYERS
            if j == 0:
                w = b_wkv.astype(BF16)
                kv_sh, kv_sc = kv_mod[:, None, :D], kv_mod[:, None, D:]
                k_sh = _norm_mod_matmul(x, kv_norm_g, kv_sh, kv_sc, w[:, :D])
                vt_sh = _norm_mod_matmul_t(x, kv_norm_g, kv_sh, kv_sc, w[:, D:].T)
            q = _norm_mod_matmul(x, norm_mix_g[l], sh_a, sc_a, b_wq[j].astype(BF16))
            attn = _sb_attention(q, k_sh, vt_sh)
            x = _matmul_gated_residual(attn, b_wo[j].astype(BF16), x, g_a)
        if l % 2 == 0:
            x = _dense_ffn(x, norm_ffn_g[l], sh_f, sc_f, g_f,
                           ffn_w13[l // 2][None].astype(BF16), ffn_w2[l // 2][None].astype(BF16))
        else:
            x = _moe_layer(x, norm_ffn_g[l], sh_f, sc_f, g_f, router_w[l // 2],
                           moe_w13.reshape((-1,) + moe_w13.shape[2:]),
                           moe_w2.reshape((-1,) + moe_w2.shape[2:]), (l // 2) * N_EXPERTS,
                           final_norm_g, l == DEPTH - 1)
    return x
```

```python
import functools
import math

import numpy as np
import jax
import jax.numpy as jnp
from jax import lax
from jax.experimental import pallas as pl
from jax.experimental.pallas import tpu as pltpu

D_MODEL = 1024
DEPTH = 4
N_HEADS = 16
HEAD_DIM = D_MODEL // N_HEADS
N_A_LAYERS = DEPTH // 2
MOBA_BLOCK = 256
MOBA_TOPK = 3
N_BUCKETS = 32
MAX_DISTANCE = 1024
N_EXPERTS = 8
TOP_K_EXPERTS = 2
EPS = 1e-6
NEG = -1e30

LANES = 128
HEADS_PER_TILE = LANES // HEAD_DIM
ATT_BLOCK = 256
N_BIAS_TILES = 6
SB_DEAD_LOG = -105.0
VMEM_LIMIT = 56 * 1024 * 1024

F32 = jnp.float32
BF16 = jnp.bfloat16
HIGHEST = lax.Precision.HIGHEST
_NT = (((1,), (1,)), ((), ()))


def _params(n_axes):
    return pltpu.CompilerParams(dimension_semantics=("arbitrary",) * n_axes,
                                vmem_limit_bytes=VMEM_LIMIT)


def _norm_mod(x, g, sh, sc):
    ms = jnp.mean(x * x, axis=-1, keepdims=True)
    y = (x * lax.rsqrt(ms + EPS)) * g
    return y * (1.0 + sc) + sh


def _mod_body(c_ref, w_ref, b_ref, o_ref):
    c = c_ref[...]
    s = c * jax.nn.sigmoid(c)
    o_ref[0] = jnp.dot(s, w_ref[0], preferred_element_type=F32, precision=HIGHEST) + b_ref[0]


def _modulation(c, w, b):
    L, D, M = w.shape
    B = c.shape[0]
    tn = 1024
    return pl.pallas_call(
        _mod_body,
        grid=(L, M // tn),
        in_specs=[pl.BlockSpec((B, D), lambda l, j: (0, 0)),
                  pl.BlockSpec((1, D, tn), lambda l, j: (l, 0, j)),
                  pl.BlockSpec((1, 1, tn), lambda l, j: (l, 0, j))],
        out_specs=pl.BlockSpec((1, B, tn), lambda l, j: (l, 0, j)),
        out_shape=jax.ShapeDtypeStruct((L, B, M), F32),
        compiler_params=_params(2),
        name="adaln_mod",
    )(c, w, b.reshape(L, 1, M))


def _nm_mm_body(x_ref, g_ref, sh_ref, sc_ref, w_ref, o_ref, h_sc):
    @pl.when(pl.program_id(2) == 0)
    def _():
        h_sc[...] = _norm_mod(x_ref[0], g_ref[...], sh_ref[0], sc_ref[0]).astype(BF16)

    o_ref[0] = jnp.dot(h_sc[...], w_ref[...], preferred_element_type=F32).astype(o_ref.dtype)


def _norm_mod_matmul(x, g, sh, sc, w):
    B, S, D = x.shape
    N = w.shape[1]
    tm, tn = 1024, 1024
    return pl.pallas_call(
        _nm_mm_body,
        grid=(B, S // tm, N // tn),
        in_specs=[pl.BlockSpec((1, tm, D), lambda b, i, j: (b, i, 0)),
                  pl.BlockSpec((1, D), lambda b, i, j: (0, 0)),
                  pl.BlockSpec((1, 1, D), lambda b, i, j: (b, 0, 0)),
                  pl.BlockSpec((1, 1, D), lambda b, i, j: (b, 0, 0)),
                  pl.BlockSpec((D, tn), lambda b, i, j: (0, j))],
        out_specs=pl.BlockSpec((1, tm, tn), lambda b, i, j: (b, i, j)),
        out_shape=jax.ShapeDtypeStruct((B, S, N), BF16),
        scratch_shapes=[pltpu.VMEM((tm, D), BF16)],
        compiler_params=_params(3),
        name="norm_mod_proj",
    )(x, g.reshape(1, D), sh, sc, w)


def _nm_pair_body(x_ref, g_ref, sh_ref, sc_ref, w_ref, wt_ref, o_ref, ot_ref, h_sc, *, n_plain):
    j = pl.program_id(2)

    @pl.when(j == 0)
    def _():
        h_sc[...] = _norm_mod(x_ref[0], g_ref[...], sh_ref[0], sc_ref[0]).astype(BF16)

    @pl.when(j < n_plain)
    def _():
        o_ref[0] = jnp.dot(h_sc[...], w_ref[...], preferred_element_type=F32).astype(o_ref.dtype)

    @pl.when(j >= n_plain)
    def _():
        res = lax.dot_general(wt_ref[...], h_sc[...], _NT, preferred_element_type=F32)
        for r in range(ot_ref.shape[1]):
            ot_ref[0, r] = res[:, r * ATT_BLOCK:(r + 1) * ATT_BLOCK].astype(ot_ref.dtype)


def _norm_mod_matmul_pair(x, g, sh, sc, w, wt):
    B, S, D = x.shape
    N1, N2 = w.shape[1], wt.shape[0]
    tm, tn = 1024, 1024
    T = ATT_BLOCK
    n_plain = N1 // tn
    return pl.pallas_call(
        functools.partial(_nm_pair_body, n_plain=n_plain),
        grid=(B, S // tm, n_plain + N2 // tn),
        in_specs=[pl.BlockSpec((1, tm, D), lambda b, i, j: (b, i, 0)),
                  pl.BlockSpec((1, D), lambda b, i, j: (0, 0)),
                  pl.BlockSpec((1, 1, D), lambda b, i, j: (b, 0, 0)),
                  pl.BlockSpec((1, 1, D), lambda b, i, j: (b, 0, 0)),
                  pl.BlockSpec((D, tn), lambda b, i, j: (0, jnp.minimum(j, n_plain - 1))),
                  pl.BlockSpec((tn, D), lambda b, i, j: (jnp.maximum(j - n_plain, 0), 0))],
        out_specs=[pl.BlockSpec((1, tm, tn), lambda b, i, j: (b, i, jnp.minimum(j, n_plain - 1))),
                   pl.BlockSpec((1, tm // T, tn, T), lambda b, i, j: (b, i, jnp.maximum(j - n_plain, 0), 0))],
        out_shape=[jax.ShapeDtypeStruct((B, S, N1), BF16),
                   jax.ShapeDtypeStruct((B, S // T, N2, T), BF16)],
        scratch_shapes=[pltpu.VMEM((tm, D), BF16)],
        compiler_params=_params(3),
        name="norm_mod_proj_pair",
    )(x, g.reshape(1, D), sh, sc, w, wt)


def _mm_res_body(a_ref, w_ref, r_ref, gate_ref, o_ref):
    acc = jnp.dot(a_ref[0], w_ref[...], preferred_element_type=F32)
    o_ref[0] = r_ref[0] + gate_ref[0] * acc


def _matmul_gated_residual(a, w, res, gate):
    B, S, K = a.shape
    N = w.shape[1]
    tm, tn = 1024, 1024
    return pl.pallas_call(
        _mm_res_body,
        grid=(B, S // tm, N // tn),
        in_specs=[pl.BlockSpec((1, tm, K), lambda b, i, j: (b, i, 0)),
                  pl.BlockSpec((K, tn), lambda b, i, j: (0, j)),
                  pl.BlockSpec((1, tm, tn), lambda b, i, j: (b, i, j)),
                  pl.BlockSpec((1, 1, tn), lambda b, i, j: (b, 0, j))],
        out_specs=pl.BlockSpec((1, tm, tn), lambda b, i, j: (b, i, j)),
        out_shape=jax.ShapeDtypeStruct((B, S, N), F32),
        compiler_params=_params(3),
        name="out_proj_residual",
    )(a, w, res, gate)


def _pv_t(vt, ps, head_row):
    zeros = jnp.zeros_like(vt)
    vcat = jnp.concatenate(
        [jnp.where((head_row >= hh * HEAD_DIM) & (head_row < (hh + 1) * HEAD_DIM), vt, zeros)
         for hh in range(HEADS_PER_TILE)], axis=1)
    return jnp.dot(vcat, jnp.concatenate(ps, axis=0), preferred_element_type=F32)


def _per_head_rows(head_row, vals):
    assert HEADS_PER_TILE == 2
    return jnp.where(head_row < HEAD_DIM, vals[0], vals[1])


def _head_queries(q, lane):
    out = []
    for hh in range(HEADS_PER_TILE):
        in_head = (lane >= hh * HEAD_DIM) & (lane < (hh + 1) * HEAD_DIM)
        out.append(jnp.where(in_head, q, jnp.zeros_like(q)))
    return out


def _rel_bucket(dist):
    max_exact = N_BUCKETS // 2
    d = np.maximum(dist, 1).astype(np.float32)
    ratio = np.log(d / np.float32(max_exact)) / np.float32(math.log(MAX_DISTANCE / max_exact))
    large = max_exact + (ratio * np.float32(N_BUCKETS - max_exact)).astype(np.int32)
    large = np.minimum(large, N_BUCKETS - 1)
    return np.where(dist < max_exact, dist, large)


def _bias_tiles(rel_bias):
    T = ATT_BLOCK
    W = 2 * T
    H = rel_bias.shape[1]
    k = np.arange(W)
    lag = np.where(k <= T, k, k - W)
    dist = np.maximum(np.arange(N_BIAS_TILES)[:, None] * T + lag[None, :], 0)
    bucket = _rel_bucket(dist)
    assert (bucket[N_BIAS_TILES - 1] == N_BUCKETS - 1).all()
    onehot = np.eye(N_BUCKETS, dtype=np.float32)[bucket.reshape(-1)]
    vec = jnp.dot(jnp.asarray(onehot), rel_bias, precision=HIGHEST)
    a = vec.T.reshape(H, N_BIAS_TILES, W)
    tiled = jnp.tile(a, (1, 1, T))
    return tiled[:, :, :T * (W - 1)].reshape(H, N_BIAS_TILES, T, W - 1)[:, :, :, :T]


MOBA_TILES_PER_STEP = 2


def _moba_body(q_ref, k_ref, vt_ref, bias_ref, o_ref, kmean_sc, sel_sc, s_sc, p_sc, *, nb):
    step = pl.program_id(2)
    T = ATT_BLOCK
    tiles = range(MOBA_TILES_PER_STEP)
    heads = range(HEADS_PER_TILE)
    assert MOBA_TILES_PER_STEP == 2
    c0 = step * MOBA_TILES_PER_STEP
    cs = [c0 + u for u in tiles]

    @pl.when(step == 0)
    def _():
        rest = jnp.concatenate(
            [jnp.sum(k_ref[0, n * T:(n + 1) * T, :].astype(F32), axis=0, keepdims=True) * (1.0 / T)
             for n in range(nb)], axis=0)
        for part in range(3):
            term = rest.astype(BF16)
            kmean_sc[part * nb:(part + 1) * nb, :] = term
            rest = rest - term.astype(F32)

    lane = lax.broadcasted_iota(jnp.int32, (T, LANES), 1)
    blk = lax.broadcasted_iota(jnp.int32, (nb, T), 0)
    key_i = lax.broadcasted_iota(jnp.int32, (T, T), 0)
    qry_i = lax.broadcasted_iota(jnp.int32, (T, T), 1)
    causal = key_i <= qry_i
    head_row = lax.broadcasted_iota(jnp.int32, (LANES, T), 0)
    kmean = kmean_sc[...]
    scale = HEAD_DIM ** -0.5

    qs = []
    for u in tiles:
        qs.append([])
        for hh, qh in enumerate(_head_queries(q_ref[0, u * T:(u + 1) * T, :], lane)):
            terms = lax.dot_general(kmean, qh, _NT, preferred_element_type=F32)
            gate = terms[:nb] + terms[nb:2 * nb] + terms[2 * nb:]
            gate = jnp.where(blk < cs[u], gate, NEG)
            rank = jnp.zeros((nb, T), jnp.int32)
            for m in range(nb):
                gm = gate[m:m + 1, :]
                beats = (gm > gate) | ((gm == gate) & (blk > m))
                rank = rank + beats.astype(jnp.int32)
            sel_sc[u, hh] = ((rank < MOBA_TOPK) & (blk < cs[u])).astype(F32)
            qs[u].append((qh.astype(F32) * scale).astype(BF16))

    def scores(n, which=tiles):
        kb = k_ref[0, pl.ds(pl.multiple_of(n * T, T), T), :]
        return {u: [lax.dot_general(kb, qs[u][hh], _NT, preferred_element_type=F32) for hh in heads]
                for u in which}

    acc_rows = LANES + 16
    acc_row = lax.broadcasted_iota(jnp.int32, (acc_rows, T), 0)
    first_head = (acc_row < HEAD_DIM) | ((acc_row >= LANES) & (acc_row < LANES + 8))
    grp = lax.broadcasted_iota(jnp.int32, (16, T), 0) // 8
    ones_rows = [(grp == hh).astype(BF16) for hh in heads]

    def soften(u, n, raw, ms):
        dl = jnp.minimum(cs[u] - n, N_BIAS_TILES - 1)
        new_ms, alphas = [], []
        for hh in heads:
            m = ms[hh]
            picked = sel_sc[u, hh, pl.ds(n, 1), :] > 0.5
            s = raw[hh] + bias_ref[hh, pl.ds(dl, 1)][0]
            m_new = jnp.where(picked, jnp.maximum(m, jnp.max(s, axis=0, keepdims=True)), m)
            p_sc[u, hh] = jnp.exp(s - jnp.where(picked, m_new, -NEG)).astype(BF16)
            new_ms.append(m_new)
            alphas.append(jnp.exp(m - m_new))
        return new_ms, alphas

    def value_rows(blk_idx):
        vt = vt_ref[0, blk_idx]
        zeros = jnp.zeros_like(vt)
        return jnp.concatenate(
            [jnp.concatenate([jnp.where((head_row >= hh * HEAD_DIM) & (head_row < (hh + 1) * HEAD_DIM),
                                        vt, zeros), ones_rows[hh]], axis=0) for hh in heads],
            axis=1)

    def apply_pending(u, lhs, alphas, acc):
        ps = jnp.concatenate([p_sc[u, hh] for hh in heads], axis=0)
        return (jnp.where(first_head, alphas[0], alphas[1]) * acc
                + jnp.dot(lhs, ps, preferred_element_type=F32))

    def own_block(u, raw):
        ms = []
        for hh in heads:
            s = jnp.where(causal, raw[hh] + bias_ref[hh, 0], NEG)
            m = jnp.max(s, axis=0, keepdims=True)
            p_sc[u, hh] = jnp.exp(s - m).astype(BF16)
            ms.append(m)
        return ms

    raw_own = {u: scores(cs[u], [u])[u] for u in tiles}
    raw_extra = scores(c0, [1])[1]
    raw_next = scores(0)
    for u in tiles:
        for hh in heads:
            s_sc[u, hh] = raw_next[u][hh]

    ones = jnp.ones((1, T), F32)
    zero_acc = jnp.zeros((acc_rows, T), F32)
    ms = {u: own_block(u, raw_own[u]) for u in tiles}
    alphas = {0: [ones, ones]}
    accs = {0: zero_acc}
    accs[1] = apply_pending(1, value_rows(cs[1]), [ones, ones], zero_acc)
    ms[1], alphas[1] = soften(1, c0, raw_extra, ms[1])

    def pack(ms, alphas, accs):
        return tuple(ms[0]) + tuple(ms[1]) + tuple(alphas[0]) + tuple(alphas[1]) + (accs[0], accs[1])

    def unpack(carry):
        return ({0: list(carry[0:2]), 1: list(carry[2:4])}, {0: list(carry[4:6]), 1: list(carry[6:8])},
                {0: carry[8], 1: carry[9]})

    def body(t, carry):
        ms, alphas, accs = unpack(carry)
        raw_t = {u: [s_sc[u, hh] for hh in heads] for u in tiles}
        lhs = value_rows(jnp.where(t == 0, c0, t - 1))
        for u in tiles:
            accs[u] = apply_pending(u, lhs, alphas[u], accs[u])
        raw_n = scores(jnp.minimum(t + 1, c0 - 1))
        for u in tiles:
            ms[u], alphas[u] = soften(u, t, raw_t[u], ms[u])
        for u in tiles:
            for hh in heads:
                s_sc[u, hh] = raw_n[u][hh]
        return pack(ms, alphas, accs)

    ms, alphas, accs = unpack(lax.fori_loop(
        0, step, lambda i, carry: body(2 * i + 1, body(2 * i, carry)), pack(ms, alphas, accs)))
    lhs = value_rows(jnp.maximum(c0 - 1, 0))
    for u in tiles:
        acc = apply_pending(u, lhs, alphas[u], accs[u])
        norm = _per_head_rows(head_row, [acc[LANES:LANES + 1], acc[LANES + 8:LANES + 9]])
        o_ref[0, u * T:(u + 1) * T, :] = (acc[:LANES] / norm).T.astype(o_ref.dtype)


def _moba_attention(qk, vt, bias_tiles):
    B, S, D2 = qk.shape
    D = D2 // 2
    T = ATT_BLOCK
    nb = S // T
    n_hp = D // LANES
    tq = MOBA_TILES_PER_STEP * T
    return pl.pallas_call(
        functools.partial(_moba_body, nb=nb),
        grid=(B, n_hp, S // tq),
        in_specs=[pl.BlockSpec((1, tq, LANES), lambda b, h, i: (b, i, h)),
                  pl.BlockSpec((1, S, LANES), lambda b, h, i: (b, 0, n_hp + h)),
                  pl.BlockSpec((1, nb, LANES, T), lambda b, h, i: (b, 0, h, 0)),
                  pl.BlockSpec((HEADS_PER_TILE, N_BIAS_TILES, T, T), lambda b, h, i: (h, 0, 0, 0))],
        out_specs=pl.BlockSpec((1, tq, LANES), lambda b, h, i: (b, i, h)),
        out_shape=jax.ShapeDtypeStruct((B, S, D), BF16),
        scratch_shapes=[pltpu.VMEM((3 * nb, LANES), BF16),
                        pltpu.VMEM((MOBA_TILES_PER_STEP, HEADS_PER_TILE, nb, T), F32),
                        pltpu.VMEM((MOBA_TILES_PER_STEP, HEADS_PER_TILE, T, T), F32),
                        pltpu.VMEM((MOBA_TILES_PER_STEP, HEADS_PER_TILE, T, T), BF16)],
        compiler_params=_params(3),
        name="moba_attention",
    )(qk, qk, vt, bias_tiles)


SB_TILES_PER_STEP = 4


def _sb_body(q_ref, k_ref, vt_ref, o_ref):
    step = pl.program_id(2)
    T = ATT_BLOCK
    lane = lax.broadcasted_iota(jnp.int32, (T, LANES), 1)
    key_i = lax.broadcasted_iota(jnp.int32, (T, T), 0)
    qry_i = lax.broadcasted_iota(jnp.int32, (T, T), 1)
    strict = key_i < qry_i
    later = (qry_i > key_i).astype(BF16)
    later2 = jnp.concatenate([later, later], axis=1)
    head_row = lax.broadcasted_iota(jnp.int32, (LANES, T), 0)
    scale = HEAD_DIM ** -0.5
    cs = [step * SB_TILES_PER_STEP + u for u in range(SB_TILES_PER_STEP)]
    qs = [[(qh.astype(F32) * scale).astype(BF16)
           for qh in _head_queries(q_ref[0, u * T:(u + 1) * T, :], lane)] for u in range(SB_TILES_PER_STEP)]

    def run(jobs, states):
        rs = {u: list(st[:HEADS_PER_TILE]) for u, st in states.items()}
        accs = {u: st[-1] for u, st in states.items()}
        items = [(ji, hh) for ji in range(len(jobs)) for hh in range(HEADS_PER_TILE)]
        kbs = [k_ref[0, pl.ds(pl.multiple_of(n * T, T), T), :] for _, n, _, _ in jobs]
        z = {(ji, hh): lax.dot_general(kbs[ji], qs[jobs[ji][0]][hh], _NT, preferred_element_type=F32)
             for ji, hh in items}
        lb, l1, hilo = {}, {}, {}
        for t in items:
            sp = jnp.log(1.0 + jnp.exp(-jnp.abs(z[t])))
            lb[t] = jnp.minimum(z[t], 0.0) - sp
            l1[t] = lb[t] - z[t]
            if jobs[t[0]][2]:
                l1[t] = jnp.where(strict, l1[t], 0.0)
            hi = l1[t].astype(BF16)
            lo = (l1[t] - hi.astype(F32)).astype(BF16)
            hilo[t] = jnp.concatenate([hi, lo], axis=0)
        tail = {t: jnp.dot(later2, hilo[t], preferred_element_type=F32) for t in items}
        ps = {}
        for ji, hh in items:
            u, _, diag, keep = jobs[ji]
            a = jnp.exp(lb[ji, hh] + tail[ji, hh] + rs[u][hh])
            if diag:
                a = jnp.where(strict, a, 0.0)
            ps[ji, hh] = a.astype(BF16)
            inc = tail[ji, hh][0:1, :] + l1[ji, hh][0:1, :]
            if keep is not None:
                inc = jnp.where(keep, inc, 0.0)
            rs[u][hh] = rs[u][hh] + inc
        for ji, (u, n, _, keep) in enumerate(jobs):
            vt = vt_ref[0, n]
            if keep is not None:
                vt = jnp.where(keep, vt, jnp.zeros_like(vt))
            accs[u] = accs[u] + _pv_t(vt, [ps[ji, hh] for hh in range(HEADS_PER_TILE)], head_row)
        return {u: tuple(rs[u]) + (accs[u],) for u in states}

    zero = (jnp.zeros((1, T), F32),) * HEADS_PER_TILE + (jnp.zeros((LANES, T), F32),)
    jobs = []
    for u, c in enumerate(cs):
        jobs += [(u, c, True, None), (u, jnp.maximum(c - 1, 0), False, (c > 0) if u == 0 else None)]
    states = run(jobs, {u: zero for u in range(SB_TILES_PER_STEP)})

    for u, c in enumerate(cs):
        def cond(carry, c=c):
            live = jnp.max(jnp.maximum(carry[1], carry[2])) > SB_DEAD_LOG
            return jnp.logical_and(carry[0] < c, live)

        def body(carry, u=u, c=c):
            i = carry[0]
            return (i + 1,) + run([(u, c - 1 - i, False, None)], {u: carry[1:]})[u]

        fin = lax.while_loop(cond, body, (jnp.int32(1),) + states[u])
        o_ref[0, u * T:(u + 1) * T, :] = fin[-1].T.astype(o_ref.dtype)


def _sb_attention(q, k, vt):
    B, S, D = q.shape
    T = ATT_BLOCK
    nb = S // T
    n_hp = D // LANES
    tq = SB_TILES_PER_STEP * T
    return pl.pallas_call(
        _sb_body,
        grid=(B, n_hp, S // tq),
        in_specs=[pl.BlockSpec((1, tq, LANES), lambda b, h, i: (b, i, h)),
                  pl.BlockSpec((1, S, LANES), lambda b, h, i: (b, 0, h)),
                  pl.BlockSpec((1, nb, LANES, T), lambda b, h, i: (b, 0, h, 0))],
        out_specs=pl.BlockSpec((1, tq, LANES), lambda b, h, i: (b, i, h)),
        out_shape=jax.ShapeDtypeStruct((B, S, D), BF16),
        compiler_params=_params(3),
        name="stick_breaking_attention",
    )(q, k, vt)


def _swiglu_chunk(xb, wg, wu, w2):
    g = jnp.dot(xb, wg, preferred_element_type=F32)
    u = jnp.dot(xb, wu, preferred_element_type=F32)
    a = (g * jax.nn.sigmoid(g) * u).astype(BF16)
    return jnp.dot(a, w2, preferred_element_type=F32)


def _dense_ffn_body(x_ref, g_ref, sh_ref, sc_ref, gate_ref, wg_ref, wu_ref, w2_ref, o_ref,
                    xb_sc, acc_sc, *, n_chunks):
    j = pl.program_id(1)

    @pl.when(j == 0)
    def _():
        xb_sc[...] = _norm_mod(x_ref[...], g_ref[...], sh_ref[0], sc_ref[0]).astype(BF16)

    y = _swiglu_chunk(xb_sc[...], wg_ref[0], wu_ref[0], w2_ref[0])

    @pl.when(j == 0)
    def _():
        acc_sc[...] = y

    @pl.when(j > 0)
    def _():
        acc_sc[...] += y

    @pl.when(j == n_chunks - 1)
    def _():
        o_ref[...] = x_ref[...] + gate_ref[0] * acc_sc[...]


def _dense_ffn(x, g, sh, sc, gate, w13, w2):
    B, S, D = x.shape
    Fh = w2.shape[1]
    tm, tf = 512, 1408
    n_chunks = Fh // tf
    per_b = S // tm
    xf = x.reshape(B * S, D)
    out = pl.pallas_call(
        functools.partial(_dense_ffn_body, n_chunks=n_chunks),
        grid=(B * S // tm, n_chunks),
        in_specs=[pl.BlockSpec((tm, D), lambda i, j: (i, 0)),
                  pl.BlockSpec((1, D), lambda i, j: (0, 0)),
                  pl.BlockSpec((1, 1, D), lambda i, j: (i // per_b, 0, 0)),
                  pl.BlockSpec((1, 1, D), lambda i, j: (i // per_b, 0, 0)),
                  pl.BlockSpec((1, 1, D), lambda i, j: (i // per_b, 0, 0)),
                  pl.BlockSpec((1, D, tf), lambda i, j: (0, 0, j)),
                  pl.BlockSpec((1, D, tf), lambda i, j: (0, 0, j + n_chunks)),
                  pl.BlockSpec((1, tf, D), lambda i, j: (0, j, 0))],
        out_specs=pl.BlockSpec((tm, D), lambda i, j: (i, 0)),
        out_shape=jax.ShapeDtypeStruct((B * S, D), F32),
        scratch_shapes=[pltpu.VMEM((tm, D), BF16), pltpu.VMEM((tm, D), F32)],
        compiler_params=_params(2),
        name="dense_swiglu",
    )(xf, g.reshape(1, D), sh, sc, gate, w13, w13, w2)
    return out.reshape(B, S, D)


def _router_body(x_ref, g_ref, sh_ref, sc_ref, rwt_ref, h_ref, ei_ref, w_ref):
    h = _norm_mod(x_ref[0], g_ref[...], sh_ref[0], sc_ref[0])
    h_ref[...] = h
    logits = lax.dot_general(rwt_ref[...], h, _NT, preferred_element_type=F32,
                             precision=HIGHEST)
    n_e = logits.shape[0]
    eio = lax.broadcasted_iota(jnp.int32, logits.shape, 0)
    m1 = jnp.max(logits, axis=0, keepdims=True)
    i1 = jnp.min(jnp.where(logits == m1, eio, n_e), axis=0, keepdims=True)
    rest = jnp.where(eio == i1, -jnp.inf, logits)
    m2 = jnp.max(rest, axis=0, keepdims=True)
    i2 = jnp.min(jnp.where(rest == m2, eio, n_e), axis=0, keepdims=True)
    t = jnp.exp(m2 - m1)
    den = 1.0 + t
    ei_ref[...] = jnp.concatenate([i1, i2], axis=0)
    w_ref[...] = jnp.concatenate([1.0 / den, t / den], axis=0)


def _router(x, g, sh, sc, router_w):
    B, S, D = x.shape
    E = router_w.shape[1]
    tm = 512
    per_b = S // tm
    return pl.pallas_call(
        _router_body,
        grid=(B, per_b),
        in_specs=[pl.BlockSpec((1, tm, D), lambda b, i: (b, i, 0)),
                  pl.BlockSpec((1, D), lambda b, i: (0, 0)),
                  pl.BlockSpec((1, 1, D), lambda b, i: (b, 0, 0)),
                  pl.BlockSpec((1, 1, D), lambda b, i: (b, 0, 0)),
                  pl.BlockSpec((E, D), lambda b, i: (0, 0))],
        out_specs=[pl.BlockSpec((tm, D), lambda b, i: (b * per_b + i, 0)),
                   pl.BlockSpec((TOP_K_EXPERTS, tm), lambda b, i: (0, b * per_b + i)),
                   pl.BlockSpec((TOP_K_EXPERTS, tm), lambda b, i: (0, b * per_b + i))],
        out_shape=[jax.ShapeDtypeStruct((B * S, D), F32),
                   jax.ShapeDtypeStruct((TOP_K_EXPERTS, B * S), jnp.int32),
                   jax.ShapeDtypeStruct((TOP_K_EXPERTS, B * S), F32)],
        compiler_params=_params(2),
        name="moe_router",
    )(x, g.reshape(1, D), sh, sc, router_w.T)


def _route_plan(ei, tm, n_tiles):
    n_src = TOP_K_EXPERTS * ei.shape[1]
    assert n_src & (n_src - 1) == 0
    i32 = jnp.int32
    keys = jnp.sort(ei.reshape(-1) * n_src + jnp.arange(n_src, dtype=i32))
    runs = jnp.sum(keys[None, :] < (jnp.arange(N_EXPERTS + 1, dtype=i32) * n_src)[:, None], axis=1, dtype=i32)
    first, counts = runs[:-1], runs[1:] - runs[:-1]
    padded = ((counts + tm - 1) // tm) * tm
    ends = jnp.cumsum(padded)
    offs = ends - padded
    starts = jnp.arange(n_tiles, dtype=i32) * tm
    t_exp = jnp.minimum(jnp.sum(ends[None, :] <= starts[:, None], axis=1, dtype=i32), N_EXPERTS - 1)
    t_rows = jnp.clip(offs[t_exp] + counts[t_exp] - starts, 0, tm)
    t_rows = jnp.where(starts < ends[-1], t_rows, 0).astype(i32)
    r = jnp.arange(tm, dtype=i32)[None, :]
    nth = (first[t_exp] + starts - offs[t_exp])[:, None] + r
    real = r < t_rows[:, None]
    body = jnp.where(real, jnp.take(keys, jnp.clip(nth, 0, n_src - 1)) & (n_src - 1), n_src + r)
    flat = jnp.concatenate([n_src + r[0], body.reshape(-1)])
    return flat, t_exp, t_rows


def _moe_ffn_body(te_ref, tr_ref, flat_ref, h_hbm, wg_ref, wu_ref, w2_ref, ys_hbm,
                  xstage, xb_sc, acc_sc, ybuf, gsem, ssem, *, tm, n_chunks, n_tiles, n_tok):
    i = pl.program_id(0)
    j = pl.program_id(1)
    valid = tr_ref[i] > 0
    rows_per_step = tm // n_chunks

    def gather_row(tile, q, rr):
        tok = flat_ref[(tile + 1) * tm + q * rows_per_step + rr] & (n_tok - 1)
        return pltpu.make_async_copy(h_hbm.at[pl.ds(tok, 1)], xstage.at[q, pl.ds(rr, 1)], gsem)

    def scatter_row(tile, q, rr):
        dst = flat_ref[(tile + 1) * tm + q * rows_per_step + rr]
        return pltpu.make_async_copy(ybuf.at[q, pl.ds(rr, 1)], ys_hbm.at[pl.ds(dst, 1)], ssem)

    def gather_wait(steps):
        for q in range(steps):
            pltpu.make_async_copy(h_hbm.at[pl.ds(0, rows_per_step)], xstage.at[q], gsem).wait()

    def scatter_wait(steps):
        for q in range(steps):
            pltpu.make_async_copy(ybuf.at[q], ys_hbm.at[pl.ds(0, rows_per_step)], ssem).wait()

    def all_rows(start_row):
        def one(q, _):
            for rr in range(rows_per_step):
                start_row(q, rr).start()
            return 0
        lax.fori_loop(0, n_chunks, one, 0)

    @pl.when(valid)
    def _():
        @pl.when(j == 0)
        def _():
            @pl.when(i == 0)
            def _():
                all_rows(functools.partial(gather_row, 0))
                gather_wait(n_chunks)
                ybuf[...] = jnp.zeros_like(ybuf)

            xb_sc[...] = xstage[...].reshape(tm, -1).astype(BF16)

        for rr in range(rows_per_step):
            gather_row(i + 1, j, rr).start()
            scatter_row(i - 1, j, rr).start()

        xb = xb_sc[...]
        g = jnp.dot(xb, wg_ref[0].astype(BF16), preferred_element_type=F32)
        u = jnp.dot(xb, wu_ref[0].astype(BF16), preferred_element_type=F32)
        a = (g * jax.nn.sigmoid(g) * u).astype(BF16)
        gather_wait(1)
        scatter_wait(1)
        y = jnp.dot(a, w2_ref[0].astype(BF16), preferred_element_type=F32)

        @pl.when(j == 0)
        def _():
            acc_sc[...] = y

        @pl.when(jnp.logical_and(j > 0, j < n_chunks - 1))
        def _():
            acc_sc[...] += y

        @pl.when(j == n_chunks - 1)
        def _():
            ybuf[...] = (acc_sc[...] + y).reshape(ybuf.shape)

    prv = jnp.maximum(i - 1, 0)

    @pl.when(jnp.logical_and(jnp.logical_and(jnp.logical_not(valid), j == 0),
                             jnp.logical_and(i > 0, tr_ref[prv] > 0)))
    def _():
        all_rows(functools.partial(scatter_row, prv))
        scatter_wait(n_chunks)


def _moe_ffn(h, flat, t_exp, t_rows, w13, w2, tm):
    n_tok, D = h.shape
    Fh = w2.shape[1]
    n_tiles = t_exp.shape[0]
    n_chunks = 4
    tf = Fh // n_chunks
    assert tf % LANES == 0 and tm % n_chunks == 0 and n_tok & (n_tok - 1) == 0 and n_tok >= tm

    def wmap(col_off):
        return lambda i, j, te, tr, fl: (te[i], 0, j * jnp.minimum(tr[i], 1) + col_off)

    return pl.pallas_call(
        functools.partial(_moe_ffn_body, tm=tm, n_chunks=n_chunks, n_tiles=n_tiles, n_tok=n_tok),
        grid_spec=pltpu.PrefetchScalarGridSpec(
            num_scalar_prefetch=3,
            grid=(n_tiles, n_chunks),
            in_specs=[pl.BlockSpec(memory_space=pl.ANY),
                      pl.BlockSpec((1, D, tf), wmap(0)),
                      pl.BlockSpec((1, D, tf), wmap(n_chunks)),
                      pl.BlockSpec((1, tf, D), lambda i, j, te, tr, fl: (te[i], j * jnp.minimum(tr[i], 1), 0))],
            out_specs=pl.BlockSpec(memory_space=pl.ANY),
            scratch_shapes=[pltpu.VMEM((n_chunks, tm // n_chunks, D), F32), pltpu.VMEM((tm, D), BF16),
                            pltpu.VMEM((tm, D), F32), pltpu.VMEM((n_chunks, tm // n_chunks, D), F32),
                            pltpu.SemaphoreType.DMA, pltpu.SemaphoreType.DMA]),
        out_shape=jax.ShapeDtypeStruct((TOP_K_EXPERTS * n_tok + tm, D), F32),
        compiler_params=_params(2),
        name="moe_expert_swiglu",
    )(t_exp, t_rows, flat, h, w13, w13, w2)


def _combine_body(x_ref, y0_ref, y1_ref, w_ref, gate_ref, fg_ref, o_ref, *, final_norm):
    w = w_ref[...]
    y = w[:, 0:1] * y0_ref[...] + w[:, 1:2] * y1_ref[...]
    out = x_ref[...] + gate_ref[0] * y
    if final_norm:
        ms = jnp.mean(out * out, axis=-1, keepdims=True)
        out = (out * lax.rsqrt(ms + EPS)) * fg_ref[...]
    o_ref[...] = out


def _moe_combine(x, ys, wts, gate, final_g, final_norm):
    B, S, D = x.shape
    n_tok = B * S
    rows = 512
    per_b = S // rows
    out = pl.pallas_call(
        functools.partial(_combine_body, final_norm=final_norm),
        grid=(n_tok // rows,),
        in_specs=[pl.BlockSpec((rows, D), lambda i: (i, 0)),
                  pl.BlockSpec((rows, D), lambda i: (i, 0)),
                  pl.BlockSpec((rows, D), lambda i: (n_tok // rows + i, 0)),
                  pl.BlockSpec((rows, TOP_K_EXPERTS), lambda i: (i, 0)),
                  pl.BlockSpec((1, 1, D), lambda i: (i // per_b, 0, 0)),
                  pl.BlockSpec((1, D), lambda i: (0, 0))],
        out_specs=pl.BlockSpec((rows, D), lambda i: (i, 0)),
        out_shape=jax.ShapeDtypeStruct((n_tok, D), F32),
        compiler_params=_params(1),
        name="moe_combine",
    )(x.reshape(n_tok, D), ys, ys, wts, gate, final_g.reshape(1, D))
    return out.reshape(B, S, D)


def _moe_layer(x, g, sh, sc, gate, router_w, w13, w2, first_expert, final_g, final_norm):
    B, S, D = x.shape
    n_tok = B * S
    tm = 1024
    n_tiles = TOP_K_EXPERTS * n_tok // tm + N_EXPERTS
    h, ei, wts = _router(x, g, sh, sc, router_w)
    flat, t_exp, t_rows = _route_plan(ei, tm, n_tiles)
    ys = _moe_ffn(h, flat, t_exp + first_expert, t_rows, w13, w2, tm)
    return _moe_combine(x, ys, wts.T, gate, final_g, final_norm)


def kernel(x, c, rel_bias, mod_w, mod_b, norm_mix_g, norm_ffn_g, a_wqkv, a_wo, kv_norm_g, kv_mod_w,
           kv_mod_b, b_wkv, b_wq, b_wo, ffn_w13, ffn_w2, router_w, moe_w13, moe_w2, final_norm_g):
    D = D_MODEL
    mods = _modulation(c, mod_w, mod_b)
    kv_mod = _modulation(c, kv_mod_w[None], kv_mod_b[None])[0]
    bias_tiles = _bias_tiles(rel_bias)
    k_sh = vt_sh = None
    for l in range(DEPTH):
        sh_a, sc_a, g_a, sh_f, sc_f, g_f = [mods[l][:, None, i * D:(i + 1) * D] for i in range(6)]
        if l < N_A_LAYERS:
            w = a_wqkv[l].astype(BF16)
            qk, vt = _norm_mod_matmul_pair(x, norm_mix_g[l], sh_a, sc_a, w[:, :2 * D], w[:, 2 * D:].T)
            attn = _moba_attention(qk, vt, bias_tiles)
            x = _matmul_gated_residual(attn, a_wo[l].astype(BF16), x, g_a)
        else:
            j = l - N_A_LAYERS
            if j == 0:
                w = b_wkv.astype(BF16)
                kv_sh, kv_sc = kv_mod[:, None, :D], kv_mod[:, None, D:]
                k_sh, vt_sh = _norm_mod_matmul_pair(x, kv_norm_g, kv_sh, kv_sc, w[:, :D], w[:, D:].T)
            q = _norm_mod_matmul(x, norm_mix_g[l], sh_a, sc_a, b_wq[j].astype(BF16))
            attn = _sb_attention(q, k_sh, vt_sh)
            x = _matmul_gated_residual(attn, b_wo[j].astype(BF16), x, g_a)
        if l % 2 == 0:
            x = _dense_ffn(x, norm_ffn_g[l], sh_f, sc_f, g_f,
                           ffn_w13[l // 2][None].astype(BF16), ffn_w2[l // 2][None].astype(BF16))
        else:
            x = _moe_layer(x, norm_ffn_g[l], sh_f, sc_f, g_f, router_w[l // 2],
                           moe_w13.reshape((-1,) + moe_w13.shape[2:]),
                           moe_w2.reshape((-1,) + moe_w2.shape[2:]), (l // 2) * N_EXPERTS,
                           final_norm_g, l == DEPTH - 1)
    return x
```

```python
import functools
import math

import numpy as np
import jax
import jax.numpy as jnp
from jax import lax
from jax.experimental import pallas as pl
from jax.experimental.pallas import tpu as pltpu

D_MODEL = 1024
DEPTH = 4
N_HEADS = 16
HEAD_DIM = D_MODEL // N_HEADS
N_A_LAYERS = DEPTH // 2
MOBA_BLOCK = 256
MOBA_TOPK = 3
N_BUCKETS = 32
MAX_DISTANCE = 1024
N_EXPERTS = 8
TOP_K_EXPERTS = 2
EPS = 1e-6
NEG = -1e30

LANES = 128
HEADS_PER_TILE = LANES // HEAD_DIM
ATT_BLOCK = 256
N_BIAS_TILES = 6
SB_DEAD_LOG = -105.0
VMEM_LIMIT = 56 * 1024 * 1024

F32 = jnp.float32
BF16 = jnp.bfloat16
HIGHEST = lax.Precision.HIGHEST
_NT = (((1,), (1,)), ((), ()))


def _params(n_axes):
    return pltpu.CompilerParams(dimension_semantics=("arbitrary",) * n_axes,
                                vmem_limit_bytes=VMEM_LIMIT)


def _norm_mod(x, g, sh, sc):
    ms = jnp.mean(x * x, axis=-1, keepdims=True)
    y = (x * lax.rsqrt(ms + EPS)) * g
    return y * (1.0 + sc) + sh


def _mod_body(c_ref, w_ref, b_ref, o_ref):
    c = c_ref[...]
    s = c * jax.nn.sigmoid(c)
    o_ref[0] = jnp.dot(s, w_ref[0], preferred_element_type=F32, precision=HIGHEST) + b_ref[0]


def _modulation(c, w, b):
    L, D, M = w.shape
    B = c.shape[0]
    tn = 1024
    return pl.pallas_call(
        _mod_body,
        grid=(L, M // tn),
        in_specs=[pl.BlockSpec((B, D), lambda l, j: (0, 0)),
                  pl.BlockSpec((1, D, tn), lambda l, j: (l, 0, j)),
                  pl.BlockSpec((1, 1, tn), lambda l, j: (l, 0, j))],
        out_specs=pl.BlockSpec((1, B, tn), lambda l, j: (l, 0, j)),
        out_shape=jax.ShapeDtypeStruct((L, B, M), F32),
        compiler_params=_params(2),
        name="adaln_mod",
    )(c, w, b.reshape(L, 1, M))


def _nm_mm_body(x_ref, g_ref, sh_ref, sc_ref, w_ref, o_ref, h_sc):
    @pl.when(pl.program_id(2) == 0)
    def _():
        h_sc[...] = _norm_mod(x_ref[0], g_ref[...], sh_ref[0], sc_ref[0]).astype(BF16)

    o_ref[0] = jnp.dot(h_sc[...], w_ref[...], preferred_element_type=F32).astype(o_ref.dtype)


def _norm_mod_matmul(x, g, sh, sc, w):
    B, S, D = x.shape
    N = w.shape[1]
    tm, tn = 1024, 1024
    return pl.pallas_call(
        _nm_mm_body,
        grid=(B, S // tm, N // tn),
        in_specs=[pl.BlockSpec((1, tm, D), lambda b, i, j: (b, i, 0)),
                  pl.BlockSpec((1, D), lambda b, i, j: (0, 0)),
                  pl.BlockSpec((1, 1, D), lambda b, i, j: (b, 0, 0)),
                  pl.BlockSpec((1, 1, D), lambda b, i, j: (b, 0, 0)),
                  pl.BlockSpec((D, tn), lambda b, i, j: (0, j))],
        out_specs=pl.BlockSpec((1, tm, tn), lambda b, i, j: (b, i, j)),
        out_shape=jax.ShapeDtypeStruct((B, S, N), BF16),
        scratch_shapes=[pltpu.VMEM((tm, D), BF16)],
        compiler_params=_params(3),
        name="norm_mod_proj",
    )(x, g.reshape(1, D), sh, sc, w)


def _nm_pair_body(x_ref, g_ref, sh_ref, sc_ref, w_ref, wt_ref, o_ref, ot_ref, h_sc, *, n_plain):
    j = pl.program_id(2)

    @pl.when(j == 0)
    def _():
        h_sc[...] = _norm_mod(x_ref[0], g_ref[...], sh_ref[0], sc_ref[0]).astype(BF16)

    @pl.when(j < n_plain)
    def _():
        o_ref[0] = jnp.dot(h_sc[...], w_ref[...], preferred_element_type=F32).astype(o_ref.dtype)

    @pl.when(j >= n_plain)
    def _():
        res = lax.dot_general(wt_ref[...], h_sc[...], _NT, preferred_element_type=F32)
        for r in range(ot_ref.shape[1]):
            ot_ref[0, r] = res[:, r * ATT_BLOCK:(r + 1) * ATT_BLOCK].astype(ot_ref.dtype)


def _norm_mod_matmul_pair(x, g, sh, sc, w, wt):
    B, S, D = x.shape
    N1, N2 = w.shape[1], wt.shape[0]
    tm, tn = 1024, 1024
    T = ATT_BLOCK
    n_plain = N1 // tn
    return pl.pallas_call(
        functools.partial(_nm_pair_body, n_plain=n_plain),
        grid=(B, S // tm, n_plain + N2 // tn),
        in_specs=[pl.BlockSpec((1, tm, D), lambda b, i, j: (b, i, 0)),
                  pl.BlockSpec((1, D), lambda b, i, j: (0, 0)),
                  pl.BlockSpec((1, 1, D), lambda b, i, j: (b, 0, 0)),
                  pl.BlockSpec((1, 1, D), lambda b, i, j: (b, 0, 0)),
                  pl.BlockSpec((D, tn), lambda b, i, j: (0, jnp.minimum(j, n_plain - 1))),
                  pl.BlockSpec((tn, D), lambda b, i, j: (jnp.maximum(j - n_plain, 0), 0))],
        out_specs=[pl.BlockSpec((1, tm, tn), lambda b, i, j: (b, i, jnp.minimum(j, n_plain - 1))),
                   pl.BlockSpec((1, tm // T, tn, T), lambda b, i, j: (b, i, jnp.maximum(j - n_plain, 0), 0))],
        out_shape=[jax.ShapeDtypeStruct((B, S, N1), BF16),
                   jax.ShapeDtypeStruct((B, S // T, N2, T), BF16)],
        scratch_shapes=[pltpu.VMEM((tm, D), BF16)],
        compiler_params=_params(3),
        name="norm_mod_proj_pair",
    )(x, g.reshape(1, D), sh, sc, w, wt)


def _mm_res_body(a_ref, w_ref, r_ref, gate_ref, o_ref):
    acc = jnp.dot(a_ref[0], w_ref[...], preferred_element_type=F32)
    o_ref[0] = r_ref[0] + gate_ref[0] * acc


def _matmul_gated_residual(a, w, res, gate):
    B, S, K = a.shape
    N = w.shape[1]
    tm, tn = 1024, 1024
    return pl.pallas_call(
        _mm_res_body,
        grid=(B, S // tm, N // tn),
        in_specs=[pl.BlockSpec((1, tm, K), lambda b, i, j: (b, i, 0)),
                  pl.BlockSpec((K, tn), lambda b, i, j: (0, j)),
                  pl.BlockSpec((1, tm, tn), lambda b, i, j: (b, i, j)),
                  pl.BlockSpec((1, 1, tn), lambda b, i, j: (b, 0, j))],
        out_specs=pl.BlockSpec((1, tm, tn), lambda b, i, j: (b, i, j)),
        out_shape=jax.ShapeDtypeStruct((B, S, N), F32),
        compiler_params=_params(3),
        name="out_proj_residual",
    )(a, w, res, gate)


def _pv_t(vt, ps, head_row):
    zeros = jnp.zeros_like(vt)
    vcat = jnp.concatenate(
        [jnp.where((head_row >= hh * HEAD_DIM) & (head_row < (hh + 1) * HEAD_DIM), vt, zeros)
         for hh in range(HEADS_PER_TILE)], axis=1)
    return jnp.dot(vcat, jnp.concatenate(ps, axis=0), preferred_element_type=F32)


def _per_head_rows(head_row, vals):
    assert HEADS_PER_TILE == 2
    return jnp.where(head_row < HEAD_DIM, vals[0], vals[1])


def _head_queries(q, lane):
    out = []
    for hh in range(HEADS_PER_TILE):
        in_head = (lane >= hh * HEAD_DIM) & (lane < (hh + 1) * HEAD_DIM)
        out.append(jnp.where(in_head, q, jnp.zeros_like(q)))
    return out


def _rel_bucket(dist):
    max_exact = N_BUCKETS // 2
    d = np.maximum(dist, 1).astype(np.float32)
    ratio = np.log(d / np.float32(max_exact)) / np.float32(math.log(MAX_DISTANCE / max_exact))
    large = max_exact + (ratio * np.float32(N_BUCKETS - max_exact)).astype(np.int32)
    large = np.minimum(large, N_BUCKETS - 1)
    return np.where(dist < max_exact, dist, large)


def _bias_tiles(rel_bias):
    T = ATT_BLOCK
    W = 2 * T
    H = rel_bias.shape[1]
    k = np.arange(W)
    lag = np.where(k <= T, k, k - W)
    dist = np.maximum(np.arange(N_BIAS_TILES)[:, None] * T + lag[None, :], 0)
    bucket = _rel_bucket(dist)
    assert (bucket[N_BIAS_TILES - 1] == N_BUCKETS - 1).all()
    onehot = np.eye(N_BUCKETS, dtype=np.float32)[bucket.reshape(-1)]
    vec = jnp.dot(jnp.asarray(onehot), rel_bias, precision=HIGHEST)
    a = vec.T.reshape(H, N_BIAS_TILES, W)
    tiled = jnp.tile(a, (1, 1, T))
    return tiled[:, :, :T * (W - 1)].reshape(H, N_BIAS_TILES, T, W - 1)[:, :, :, :T]


MOBA_TILES_PER_STEP = 2


def _moba_body(q_ref, k_ref, vt_ref, bias_ref, o_ref, kmean_sc, sel_sc, s_sc, p_sc, *, nb):
    step = pl.program_id(2)
    T = ATT_BLOCK
    tiles = range(MOBA_TILES_PER_STEP)
    heads = range(HEADS_PER_TILE)
    assert MOBA_TILES_PER_STEP == 2
    c0 = step * MOBA_TILES_PER_STEP
    cs = [c0 + u for u in tiles]

    @pl.when(step == 0)
    def _():
        rest = jnp.concatenate(
            [jnp.sum(k_ref[0, n * T:(n + 1) * T, :].astype(F32), axis=0, keepdims=True) * (1.0 / T)
             for n in range(nb)], axis=0)
        for part in range(3):
            term = rest.astype(BF16)
            kmean_sc[part * nb:(part + 1) * nb, :] = term
            rest = rest - term.astype(F32)

    lane = lax.broadcasted_iota(jnp.int32, (T, LANES), 1)
    blk = lax.broadcasted_iota(jnp.int32, (nb, T), 0)
    key_i = lax.broadcasted_iota(jnp.int32, (T, T), 0)
    qry_i = lax.broadcasted_iota(jnp.int32, (T, T), 1)
    causal = key_i <= qry_i
    head_row = lax.broadcasted_iota(jnp.int32, (LANES, T), 0)
    kmean = kmean_sc[...]
    scale = HEAD_DIM ** -0.5

    qs = []
    for u in tiles:
        qs.append([])
        for hh, qh in enumerate(_head_queries(q_ref[0, u * T:(u + 1) * T, :], lane)):
            terms = lax.dot_general(kmean, qh, _NT, preferred_element_type=F32)
            gate = terms[:nb] + terms[nb:2 * nb] + terms[2 * nb:]
            gate = jnp.where(blk < cs[u], gate, NEG)
            rank = jnp.zeros((nb, T), jnp.int32)
            for m in range(nb):
                gm = gate[m:m + 1, :]
                beats = (gm > gate) | ((gm == gate) & (blk > m))
                rank = rank + beats.astype(jnp.int32)
            sel_sc[u, hh] = ((rank < MOBA_TOPK) & (blk < cs[u])).astype(F32)
            qs[u].append((qh.astype(F32) * scale).astype(BF16))

    def scores(n, which=tiles):
        kb = k_ref[0, pl.ds(pl.multiple_of(n * T, T), T), :]
        return {u: [lax.dot_general(kb, qs[u][hh], _NT, preferred_element_type=F32) for hh in heads]
                for u in which}

    acc_rows = LANES + 16
    acc_row = lax.broadcasted_iota(jnp.int32, (acc_rows, T), 0)
    first_head = (acc_row < HEAD_DIM) | ((acc_row >= LANES) & (acc_row < LANES + 8))
    grp = lax.broadcasted_iota(jnp.int32, (16, T), 0) // 8
    ones_rows = [(grp == hh).astype(BF16) for hh in heads]

    def soften(u, n, raw, ms):
        dl = jnp.minimum(cs[u] - n, N_BIAS_TILES - 1)
        new_ms, alphas = [], []
        for hh in heads:
            m = ms[hh]
            picked = sel_sc[u, hh, pl.ds(n, 1), :] > 0.5
            s = raw[hh] + bias_ref[hh, pl.ds(dl, 1)][0]
            m_new = jnp.where(picked, jnp.maximum(m, jnp.max(s, axis=0, keepdims=True)), m)
            p_sc[u, hh] = jnp.exp(s - jnp.where(picked, m_new, -NEG)).astype(BF16)
            new_ms.append(m_new)
            alphas.append(jnp.exp(m - m_new))
        return new_ms, alphas

    def value_rows(blk_idx):
        vt = vt_ref[0, blk_idx]
        zeros = jnp.zeros_like(vt)
        return jnp.concatenate(
            [jnp.concatenate([jnp.where((head_row >= hh * HEAD_DIM) & (head_row < (hh + 1) * HEAD_DIM),
                                        vt, zeros), ones_rows[hh]], axis=0) for hh in heads],
            axis=1)

    def apply_pending(u, lhs, alphas, acc):
        ps = jnp.concatenate([p_sc[u, hh] for hh in heads], axis=0)
        return (jnp.where(first_head, alphas[0], alphas[1]) * acc
                + jnp.dot(lhs, ps, preferred_element_type=F32))

    def own_block(u, raw):
        ms = []
        for hh in heads:
            s = jnp.where(causal, raw[hh] + bias_ref[hh, 0], NEG)
            m = jnp.max(s, axis=0, keepdims=True)
            p_sc[u, hh] = jnp.exp(s - m).astype(BF16)
            ms.append(m)
        return ms

    raw_own = {u: scores(cs[u], [u])[u] for u in tiles}
    raw_extra = scores(c0, [1])[1]
    raw_next = scores(0)
    for u in tiles:
        for hh in heads:
            s_sc[u, hh] = raw_next[u][hh]

    ones = jnp.ones((1, T), F32)
    zero_acc = jnp.zeros((acc_rows, T), F32)
    ms = {u: own_block(u, raw_own[u]) for u in tiles}
    alphas = {0: [ones, ones]}
    accs = {0: zero_acc}
    accs[1] = apply_pending(1, value_rows(cs[1]), [ones, ones], zero_acc)
    ms[1], alphas[1] = soften(1, c0, raw_extra, ms[1])

    def pack(ms, alphas, accs):
        return tuple(ms[0]) + tuple(ms[1]) + tuple(alphas[0]) + tuple(alphas[1]) + (accs[0], accs[1])

    def unpack(carry):
        return ({0: list(carry[0:2]), 1: list(carry[2:4])}, {0: list(carry[4:6]), 1: list(carry[6:8])},
                {0: carry[8], 1: carry[9]})

    def body(t, carry):
        ms, alphas, accs = unpack(carry)
        raw_t = {u: [s_sc[u, hh] for hh in heads] for u in tiles}
        lhs = value_rows(jnp.where(t == 0, c0, t - 1))
        for u in tiles:
            accs[u] = apply_pending(u, lhs, alphas[u], accs[u])
        raw_n = scores(jnp.minimum(t + 1, c0 - 1))
        for u in tiles:
            ms[u], alphas[u] = soften(u, t, raw_t[u], ms[u])
        for u in tiles:
            for hh in heads:
                s_sc[u, hh] = raw_n[u][hh]
        return pack(ms, alphas, accs)

    ms, alphas, accs = unpack(lax.fori_loop(
        0, step, lambda i, carry: body(2 * i + 1, body(2 * i, carry)), pack(ms, alphas, accs)))
    lhs = value_rows(jnp.maximum(c0 - 1, 0))
    for u in tiles:
        acc = apply_pending(u, lhs, alphas[u], accs[u])
        norm = _per_head_rows(head_row, [acc[LANES:LANES + 1], acc[LANES + 8:LANES + 9]])
        o_ref[0, u * T:(u + 1) * T, :] = (acc[:LANES] / norm).T.astype(o_ref.dtype)


def _moba_attention(qk, vt, bias_tiles):
    B, S, D2 = qk.shape
    D = D2 // 2
    T = ATT_BLOCK
    nb = S // T
    n_hp = D // LANES
    tq = MOBA_TILES_PER_STEP * T
    return pl.pallas_call(
        functools.partial(_moba_body, nb=nb),
        grid=(B, n_hp, S // tq),
        in_specs=[pl.BlockSpec((1, tq, LANES), lambda b, h, i: (b, i, h)),
                  pl.BlockSpec((1, S, LANES), lambda b, h, i: (b, 0, n_hp + h)),
                  pl.BlockSpec((1, nb, LANES, T), lambda b, h, i: (b, 0, h, 0)),
                  pl.BlockSpec((HEADS_PER_TILE, N_BIAS_TILES, T, T), lambda b, h, i: (h, 0, 0, 0))],
        out_specs=pl.BlockSpec((1, tq, LANES), lambda b, h, i: (b, i, h)),
        out_shape=jax.ShapeDtypeStruct((B, S, D), BF16),
        scratch_shapes=[pltpu.VMEM((3 * nb, LANES), BF16),
                        pltpu.VMEM((MOBA_TILES_PER_STEP, HEADS_PER_TILE, nb, T), F32),
                        pltpu.VMEM((MOBA_TILES_PER_STEP, HEADS_PER_TILE, T, T), F32),
                        pltpu.VMEM((MOBA_TILES_PER_STEP, HEADS_PER_TILE, T, T), BF16)],
        compiler_params=_params(3),
        name="moba_attention",
    )(qk, qk, vt, bias_tiles)


SB_TILES_PER_STEP = 4


def _sb_body(q_ref, k_ref, vt_ref, o_ref):
    step = pl.program_id(2)
    T = ATT_BLOCK
    lane = lax.broadcasted_iota(jnp.int32, (T, LANES), 1)
    key_i = lax.broadcasted_iota(jnp.int32, (T, T), 0)
    qry_i = lax.broadcasted_iota(jnp.int32, (T, T), 1)
    strict = key_i < qry_i
    later = (qry_i > key_i).astype(BF16)
    later2 = jnp.concatenate([later, later], axis=1)
    head_row = lax.broadcasted_iota(jnp.int32, (LANES, T), 0)
    scale = HEAD_DIM ** -0.5
    cs = [step * SB_TILES_PER_STEP + u for u in range(SB_TILES_PER_STEP)]
    qs = [[(qh.astype(F32) * scale).astype(BF16)
           for qh in _head_queries(q_ref[0, u * T:(u + 1) * T, :], lane)] for u in range(SB_TILES_PER_STEP)]

    def run(jobs, states):
        rs = {u: list(st[:HEADS_PER_TILE]) for u, st in states.items()}
        accs = {u: st[-1] for u, st in states.items()}
        items = [(ji, hh) for ji in range(len(jobs)) for hh in range(HEADS_PER_TILE)]
        kbs = [k_ref[0, pl.ds(pl.multiple_of(n * T, T), T), :] for _, n, _, _ in jobs]
        z = {(ji, hh): lax.dot_general(kbs[ji], qs[jobs[ji][0]][hh], _NT, preferred_element_type=F32)
             for ji, hh in items}
        lb, l1, hilo = {}, {}, {}
        for t in items:
            sp = jnp.log(1.0 + jnp.exp(-jnp.abs(z[t])))
            lb[t] = jnp.minimum(z[t], 0.0) - sp
            l1[t] = lb[t] - z[t]
            if jobs[t[0]][2]:
                l1[t] = jnp.where(strict, l1[t], 0.0)
            hi = l1[t].astype(BF16)
            lo = (l1[t] - hi.astype(F32)).astype(BF16)
            hilo[t] = jnp.concatenate([hi, lo], axis=0)
        tail = {t: jnp.dot(later2, hilo[t], preferred_element_type=F32) for t in items}
        ps = {}
        for ji, hh in items:
            u, _, diag, keep = jobs[ji]
            a = jnp.exp(lb[ji, hh] + tail[ji, hh] + rs[u][hh])
            if diag:
                a = jnp.where(strict, a, 0.0)
            ps[ji, hh] = a.astype(BF16)
            inc = tail[ji, hh][0:1, :] + l1[ji, hh][0:1, :]
            if keep is not None:
                inc = jnp.where(keep, inc, 0.0)
            rs[u][hh] = rs[u][hh] + inc
        for ji, (u, n, _, keep) in enumerate(jobs):
            vt = vt_ref[0, n]
            if keep is not None:
                vt = jnp.where(keep, vt, jnp.zeros_like(vt))
            accs[u] = accs[u] + _pv_t(vt, [ps[ji, hh] for hh in range(HEADS_PER_TILE)], head_row)
        return {u: tuple(rs[u]) + (accs[u],) for u in states}

    zero = (jnp.zeros((1, T), F32),) * HEADS_PER_TILE + (jnp.zeros((LANES, T), F32),)
    jobs = []
    for u, c in enumerate(cs):
        jobs += [(u, c, True, None), (u, jnp.maximum(c - 1, 0), False, (c > 0) if u == 0 else None)]
    states = run(jobs, {u: zero for u in range(SB_TILES_PER_STEP)})

    for u, c in enumerate(cs):
        def cond(carry, c=c):
            live = jnp.max(jnp.maximum(carry[1], carry[2])) > SB_DEAD_LOG
            return jnp.logical_and(carry[0] < c, live)

        def body(carry, u=u, c=c):
            i = carry[0]
            return (i + 1,) + run([(u, c - 1 - i, False, None)], {u: carry[1:]})[u]

        fin = lax.while_loop(cond, body, (jnp.int32(1),) + states[u])
        o_ref[0, u * T:(u + 1) * T, :] = fin[-1].T.astype(o_ref.dtype)


def _sb_attention(q, k, vt):
    B, S, D = q.shape
    T = ATT_BLOCK
    nb = S // T
    n_hp = D // LANES
    tq = SB_TILES_PER_STEP * T
    return pl.pallas_call(
        _sb_body,
        grid=(B, n_hp, S // tq),
        in_specs=[pl.BlockSpec((1, tq, LANES), lambda b, h, i: (b, i, h)),
                  pl.BlockSpec((1, S, LANES), lambda b, h, i: (b, 0, h)),
                  pl.BlockSpec((1, nb, LANES, T), lambda b, h, i: (b, 0, h, 0))],
        out_specs=pl.BlockSpec((1, tq, LANES), lambda b, h, i: (b, i, h)),
        out_shape=jax.ShapeDtypeStruct((B, S, D), BF16),
        compiler_params=_params(3),
        name="stick_breaking_attention",
    )(q, k, vt)


def _swiglu_chunk(xb, wg, wu, w2):
    g = jnp.dot(xb, wg, preferred_element_type=F32)
    u = jnp.dot(xb, wu, preferred_element_type=F32)
    a = (g * jax.nn.sigmoid(g) * u).astype(BF16)
    return jnp.dot(a, w2, preferred_element_type=F32)


def _dense_ffn_body(x_ref, g_ref, sh_ref, sc_ref, gate_ref, wg_ref, wu_ref, w2_ref, o_ref,
                    xb_sc, acc_sc, *, n_chunks):
    j = pl.program_id(1)

    @pl.when(j == 0)
    def _():
        xb_sc[...] = _norm_mod(x_ref[...], g_ref[...], sh_ref[0], sc_ref[0]).astype(BF16)

    y = _swiglu_chunk(xb_sc[...], wg_ref[0], wu_ref[0], w2_ref[0])

    @pl.when(j == 0)
    def _():
        acc_sc[...] = y

    @pl.when(j > 0)
    def _():
        acc_sc[...] += y

    @pl.when(j == n_chunks - 1)
    def _():
        o_ref[...] = x_ref[...] + gate_ref[0] * acc_sc[...]


def _dense_ffn(x, g, sh, sc, gate, w13, w2):
    B, S, D = x.shape
    Fh = w2.shape[1]
    tm, tf = 512, 1408
    n_chunks = Fh // tf
    per_b = S // tm
    xf = x.reshape(B * S, D)
    out = pl.pallas_call(
        functools.partial(_dense_ffn_body, n_chunks=n_chunks),
        grid=(B * S // tm, n_chunks),
        in_specs=[pl.BlockSpec((tm, D), lambda i, j: (i, 0)),
                  pl.BlockSpec((1, D), lambda i, j: (0, 0)),
                  pl.BlockSpec((1, 1, D), lambda i, j: (i // per_b, 0, 0)),
                  pl.BlockSpec((1, 1, D), lambda i, j: (i // per_b, 0, 0)),
                  pl.BlockSpec((1, 1, D), lambda i, j: (i // per_b, 0, 0)),
                  pl.BlockSpec((1, D, tf), lambda i, j: (0, 0, j)),
                  pl.BlockSpec((1, D, tf), lambda i, j: (0, 0, j + n_chunks)),
                  pl.BlockSpec((1, tf, D), lambda i, j: (0, j, 0))],
        out_specs=pl.BlockSpec((tm, D), lambda i, j: (i, 0)),
        out_shape=jax.ShapeDtypeStruct((B * S, D), F32),
        scratch_shapes=[pltpu.VMEM((tm, D), BF16), pltpu.VMEM((tm, D), F32)],
        compiler_params=_params(2),
        name="dense_swiglu",
    )(xf, g.reshape(1, D), sh, sc, gate, w13, w13, w2)
    return out.reshape(B, S, D)


def _router_body(x_ref, g_ref, sh_ref, sc_ref, rwt_ref, h_ref, ei_ref, w_ref):
    h = _norm_mod(x_ref[0], g_ref[...], sh_ref[0], sc_ref[0])
    h_ref[...] = h
    logits = lax.dot_general(rwt_ref[...], h, _NT, preferred_element_type=F32,
                             precision=HIGHEST)
    n_e = logits.shape[0]
    eio = lax.broadcasted_iota(jnp.int32, logits.shape, 0)
    m1 = jnp.max(logits, axis=0, keepdims=True)
    i1 = jnp.min(jnp.where(logits == m1, eio, n_e), axis=0, keepdims=True)
    rest = jnp.where(eio == i1, -jnp.inf, logits)
    m2 = jnp.max(rest, axis=0, keepdims=True)
    i2 = jnp.min(jnp.where(rest == m2, eio, n_e), axis=0, keepdims=True)
    t = jnp.exp(m2 - m1)
    den = 1.0 + t
    ei_ref[...] = jnp.concatenate([i1, i2], axis=0)
    w_ref[...] = jnp.concatenate([1.0 / den, t / den], axis=0)


def _router(x, g, sh, sc, router_w):
    B, S, D = x.shape
    E = router_w.shape[1]
    tm = 512
    per_b = S // tm
    return pl.pallas_call(
        _router_body,
        grid=(B, per_b),
        in_specs=[pl.BlockSpec((1, tm, D), lambda b, i: (b, i, 0)),
                  pl.BlockSpec((1, D), lambda b, i: (0, 0)),
                  pl.BlockSpec((1, 1, D), lambda b, i: (b, 0, 0)),
                  pl.BlockSpec((1, 1, D), lambda b, i: (b, 0, 0)),
                  pl.BlockSpec((E, D), lambda b, i: (0, 0))],
        out_specs=[pl.BlockSpec((tm, D), lambda b, i: (b * per_b + i, 0)),
                   pl.BlockSpec((TOP_K_EXPERTS, tm), lambda b, i: (0, b * per_b + i)),
                   pl.BlockSpec((TOP_K_EXPERTS, tm), lambda b, i: (0, b * per_b + i))],
        out_shape=[jax.ShapeDtypeStruct((B * S, D), F32),
                   jax.ShapeDtypeStruct((TOP_K_EXPERTS, B * S), jnp.int32),
                   jax.ShapeDtypeStruct((TOP_K_EXPERTS, B * S), F32)],
        compiler_params=_params(2),
        name="moe_router",
    )(x, g.reshape(1, D), sh, sc, router_w.T)


def _route_plan(ei, tm, n_tiles):
    n_src = TOP_K_EXPERTS * ei.shape[1]
    assert n_src & (n_src - 1) == 0
    i32 = jnp.int32
    keys = jnp.sort(ei.reshape(-1) * n_src + jnp.arange(n_src, dtype=i32))
    runs = jnp.sum(keys[None, :] < (jnp.arange(N_EXPERTS + 1, dtype=i32) * n_src)[:, None], axis=1, dtype=i32)
    first, counts = runs[:-1], runs[1:] - runs[:-1]
    padded = ((counts + tm - 1) // tm) * tm
    ends = jnp.cumsum(padded)
    offs = ends - padded
    starts = jnp.arange(n_tiles, dtype=i32) * tm
    t_exp = jnp.minimum(jnp.sum(ends[None, :] <= starts[:, None], axis=1, dtype=i32), N_EXPERTS - 1)
    t_rows = jnp.clip(offs[t_exp] + counts[t_exp] - starts, 0, tm)
    t_rows = jnp.where(starts < ends[-1], t_rows, 0).astype(i32)
    r = jnp.arange(tm, dtype=i32)[None, :]
    nth = (first[t_exp] + starts - offs[t_exp])[:, None] + r
    real = r < t_rows[:, None]
    body = jnp.where(real, jnp.take(keys, jnp.clip(nth, 0, n_src - 1)) & (n_src - 1), n_src + r)
    flat = jnp.concatenate([n_src + r[0], body.reshape(-1)])
    return flat, t_exp, t_rows


def _moe_ffn_body(te_ref, tr_ref, flat_ref, h_hbm, wg_ref, wu_ref, w2_ref, ys_hbm,
                  xstage, xb_sc, acc_sc, ybuf, gsem, ssem, *, tm, n_chunks, n_tiles, n_tok):
    i = pl.program_id(0)
    j = pl.program_id(1)
    valid = tr_ref[i] > 0
    rows_per_step = tm // n_chunks

    def gather_row(tile, q, rr):
        tok = flat_ref[(tile + 1) * tm + q * rows_per_step + rr] & (n_tok - 1)
        return pltpu.make_async_copy(h_hbm.at[pl.ds(tok, 1)], xstage.at[q, pl.ds(rr, 1)], gsem)

    def scatter_row(tile, q, rr):
        dst = flat_ref[(tile + 1) * tm + q * rows_per_step + rr]
        return pltpu.make_async_copy(ybuf.at[q, pl.ds(rr, 1)], ys_hbm.at[pl.ds(dst, 1)], ssem)

    def gather_wait(steps):
        for q in range(steps):
            pltpu.make_async_copy(h_hbm.at[pl.ds(0, rows_per_step)], xstage.at[q], gsem).wait()

    def scatter_wait(steps):
        for q in range(steps):
            pltpu.make_async_copy(ybuf.at[q], ys_hbm.at[pl.ds(0, rows_per_step)], ssem).wait()

    def all_rows(start_row):
        def one(q, _):
            for rr in range(rows_per_step):
                start_row(q, rr).start()
            return 0
        lax.fori_loop(0, n_chunks, one, 0)

    @pl.when(valid)
    def _():
        @pl.when(j == 0)
        def _():
            @pl.when(i == 0)
            def _():
                all_rows(functools.partial(gather_row, 0))
                gather_wait(n_chunks)
                ybuf[...] = jnp.zeros_like(ybuf)

            xb_sc[...] = xstage[...].reshape(tm, -1).astype(BF16)

        for rr in range(rows_per_step):
            gather_row(i + 1, j, rr).start(priority=0)
            scatter_row(i - 1, j, rr).start(priority=1)

        xb = xb_sc[...]
        g = jnp.dot(xb, wg_ref[0].astype(BF16), preferred_element_type=F32)
        u = jnp.dot(xb, wu_ref[0].astype(BF16), preferred_element_type=F32)
        a = (g * jax.nn.sigmoid(g) * u).astype(BF16)
        gather_wait(1)
        scatter_wait(1)
        y = jnp.dot(a, w2_ref[0].astype(BF16), preferred_element_type=F32)

        @pl.when(j == 0)
        def _():
            acc_sc[...] = y

        @pl.when(jnp.logical_and(j > 0, j < n_chunks - 1))
        def _():
            acc_sc[...] += y

        @pl.when(j == n_chunks - 1)
        def _():
            ybuf[...] = (acc_sc[...] + y).reshape(ybuf.shape)

    prv = jnp.maximum(i - 1, 0)

    @pl.when(jnp.logical_and(jnp.logical_and(jnp.logical_not(valid), j == 0),
                             jnp.logical_and(i > 0, tr_ref[prv] > 0)))
    def _():
        all_rows(functools.partial(scatter_row, prv))
        scatter_wait(n_chunks)


def _moe_ffn(h, flat, t_exp, t_rows, w13, w2, tm):
    n_tok, D = h.shape
    Fh = w2.shape[1]
    n_tiles = t_exp.shape[0]
    n_chunks = 4
    tf = Fh // n_chunks
    assert tf % LANES == 0 and tm % n_chunks == 0 and n_tok & (n_tok - 1) == 0 and n_tok >= tm

    def wmap(col_off):
        return lambda i, j, te, tr, fl: (te[i], 0, j * jnp.minimum(tr[i], 1) + col_off)

    return pl.pallas_call(
        functools.partial(_moe_ffn_body, tm=tm, n_chunks=n_chunks, n_tiles=n_tiles, n_tok=n_tok),
        grid_spec=pltpu.PrefetchScalarGridSpec(
            num_scalar_prefetch=3,
            grid=(n_tiles, n_chunks),
            in_specs=[pl.BlockSpec(memory_space=pl.ANY),
                      pl.BlockSpec((1, D, tf), wmap(0)),
                      pl.BlockSpec((1, D, tf), wmap(n_chunks)),
                      pl.BlockSpec((1, tf, D), lambda i, j, te, tr, fl: (te[i], j * jnp.minimum(tr[i], 1), 0))],
            out_specs=pl.BlockSpec(memory_space=pl.ANY),
            scratch_shapes=[pltpu.VMEM((n_chunks, tm // n_chunks, D), F32), pltpu.VMEM((tm, D), BF16),
                            pltpu.VMEM((tm, D), F32), pltpu.VMEM((n_chunks, tm // n_chunks, D), F32),
                            pltpu.SemaphoreType.DMA, pltpu.SemaphoreType.DMA]),
        out_shape=jax.ShapeDtypeStruct((TOP_K_EXPERTS * n_tok + tm, D), F32),
        compiler_params=_params(2),
        name="moe_expert_swiglu",
    )(t_exp, t_rows, flat, h, w13, w13, w2)


def _combine_body(x_ref, y0_ref, y1_ref, w_ref, gate_ref, fg_ref, o_ref, *, final_norm):
    w = w_ref[...]
    y = w[:, 0:1] * y0_ref[...] + w[:, 1:2] * y1_ref[...]
    out = x_ref[...] + gate_ref[0] * y
    if final_norm:
        ms = jnp.mean(out * out, axis=-1, keepdims=True)
        out = (out * lax.rsqrt(ms + EPS)) * fg_ref[...]
    o_ref[...] = out


def _moe_combine(x, ys, wts, gate, final_g, final_norm):
    B, S, D = x.shape
    n_tok = B * S
    rows = 512
    per_b = S // rows
    out = pl.pallas_call(
        functools.partial(_combine_body, final_norm=final_norm),
        grid=(n_tok // rows,),
        in_specs=[pl.BlockSpec((rows, D), lambda i: (i, 0)),
                  pl.BlockSpec((rows, D), lambda i: (i, 0)),
                  pl.BlockSpec((rows, D), lambda i: (n_tok // rows + i, 0)),
                  pl.BlockSpec((rows, TOP_K_EXPERTS), lambda i: (i, 0)),
                  pl.BlockSpec((1, 1, D), lambda i: (i // per_b, 0, 0)),
                  pl.BlockSpec((1, D), lambda i: (0, 0))],
        out_specs=pl.BlockSpec((rows, D), lambda i: (i, 0)),
        out_shape=jax.ShapeDtypeStruct((n_tok, D), F32),
        compiler_params=_params(1),
        name="moe_combine",
    )(x.reshape(n_tok, D), ys, ys, wts, gate, final_g.reshape(1, D))
    return out.reshape(B, S, D)


def _moe_layer(x, g, sh, sc, gate, router_w, w13, w2, first_expert, final_g, final_norm):
    B, S, D = x.shape
    n_tok = B * S
    tm = 1024
    n_tiles = TOP_K_EXPERTS * n_tok // tm + N_EXPERTS
    h, ei, wts = _router(x, g, sh, sc, router_w)
    flat, t_exp, t_rows = _route_plan(ei, tm, n_tiles)
    ys = _moe_ffn(h, flat, t_exp + first_expert, t_rows, w13, w2, tm)
    return _moe_combine(x, ys, wts.T, gate, final_g, final_norm)


def kernel(x, c, rel_bias, mod_w, mod_b, norm_mix_g, norm_ffn_g, a_wqkv, a_wo, kv_norm_g, kv_mod_w,
           kv_mod_b, b_wkv, b_wq, b_wo, ffn_w13, ffn_w2, router_w, moe_w13, moe_w2, final_norm_g):
    D = D_MODEL
    mods = _modulation(c, mod_w, mod_b)
    kv_mod = _modulation(c, kv_mod_w[None], kv_mod_b[None])[0]
    bias_tiles = _bias_tiles(rel_bias)
    k_sh = vt_sh = None
    for l in range(DEPTH):
        sh_a, sc_a, g_a, sh_f, sc_f, g_f = [mods[l][:, None, i * D:(i + 1) * D] for i in range(6)]
        if l < N_A_LAYERS:
            w = a_wqkv[l].astype(BF16)
            qk, vt = _norm_mod_matmul_pair(x, norm_mix_g[l], sh_a, sc_a, w[:, :2 * D], w[:, 2 * D:].T)
            attn = _moba_attention(qk, vt, bias_tiles)
            x = _matmul_gated_residual(attn, a_wo[l].astype(BF16), x, g_a)
        else:
            j = l - N_A_LAYERS
            if j == 0:
                w = b_wkv.astype(BF16)
                kv_sh, kv_sc = kv_mod[:, None, :D], kv_mod[:, None, D:]
                k_sh, vt_sh = _norm_mod_matmul_pair(x, kv_norm_g, kv_sh, kv_sc, w[:, :D], w[:, D:].T)
            q = _norm_mod_matmul(x, norm_mix_g[l], sh_a, sc_a, b_wq[j].astype(BF16))
            attn = _sb_attention(q, k_sh, vt_sh)
            x = _matmul_gated_residual(attn, b_wo[j].astype(BF16), x, g_a)
        if l % 2 == 0:
            x = _dense_ffn(x, norm_ffn_g[l], sh_f, sc_f, g_f,
                           ffn_w13[l // 2][None].astype(BF16), ffn_w2[l // 2][None].astype(BF16))
        else:
            x = _moe_layer(x, norm_ffn_g[l], sh_f, sc_f, g_f, router_w[l // 2],
                           moe_w13.reshape((-1,) + moe_w13.shape[2:]),
                           moe_w2.reshape((-1,) + moe_w2.shape[2:]), (l // 2) * N_EXPERTS,
                           final_norm_g, l == DEPTH - 1)
    return x
```

```python
import functools
import math

import numpy as np
import jax
import jax.numpy as jnp
from jax import lax
from jax.experimental import pallas as pl
from jax.experimental.pallas import tpu as pltpu

D_MODEL = 1024
DEPTH = 4
N_HEADS = 16
HEAD_DIM = D_MODEL // N_HEADS
N_A_LAYERS = DEPTH // 2
MOBA_BLOCK = 256
MOBA_TOPK = 3
N_BUCKETS = 32
MAX_DISTANCE = 1024
N_EXPERTS = 8
TOP_K_EXPERTS = 2
EPS = 1e-6
NEG = -1e30

LANES = 128
HEADS_PER_TILE = LANES // HEAD_DIM
ATT_BLOCK = 256
N_BIAS_TILES = 6
SB_DEAD_LOG = -105.0
VMEM_LIMIT = 56 * 1024 * 1024

F32 = jnp.float32
BF16 = jnp.bfloat16
HIGHEST = lax.Precision.HIGHEST
_NT = (((1,), (1,)), ((), ()))


def _params(n_axes):
    return pltpu.CompilerParams(dimension_semantics=("arbitrary",) * n_axes,
                                vmem_limit_bytes=VMEM_LIMIT)


def _norm_mod(x, g, sh, sc):
    ms = jnp.mean(x * x, axis=-1, keepdims=True)
    y = (x * lax.rsqrt(ms + EPS)) * g
    return y * (1.0 + sc) + sh


def _mod_body(c_ref, w_ref, b_ref, o_ref):
    c = c_ref[...]
    s = c * jax.nn.sigmoid(c)
    o_ref[0] = jnp.dot(s, w_ref[0], preferred_element_type=F32, precision=HIGHEST) + b_ref[0]


def _modulation(c, w, b):
    L, D, M = w.shape
    B = c.shape[0]
    tn = 1024
    return pl.pallas_call(
        _mod_body,
        grid=(L, M // tn),
        in_specs=[pl.BlockSpec((B, D), lambda l, j: (0, 0)),
                  pl.BlockSpec((1, D, tn), lambda l, j: (l, 0, j)),
                  pl.BlockSpec((1, 1, tn), lambda l, j: (l, 0, j))],
        out_specs=pl.BlockSpec((1, B, tn), lambda l, j: (l, 0, j)),
        out_shape=jax.ShapeDtypeStruct((L, B, M), F32),
        compiler_params=_params(2),
        name="adaln_mod",
    )(c, w, b.reshape(L, 1, M))


def _nm_mm_body(x_ref, g_ref, sh_ref, sc_ref, w_ref, o_ref, h_sc):
    @pl.when(pl.program_id(2) == 0)
    def _():
        h_sc[...] = _norm_mod(x_ref[0], g_ref[...], sh_ref[0], sc_ref[0]).astype(BF16)

    o_ref[0] = jnp.dot(h_sc[...], w_ref[...], preferred_element_type=F32).astype(o_ref.dtype)


def _norm_mod_matmul(x, g, sh, sc, w):
    B, S, D = x.shape
    N = w.shape[1]
    tm, tn = 1024, 1024
    return pl.pallas_call(
        _nm_mm_body,
        grid=(B, S // tm, N // tn),
        in_specs=[pl.BlockSpec((1, tm, D), lambda b, i, j: (b, i, 0)),
                  pl.BlockSpec((1, D), lambda b, i, j: (0, 0)),
                  pl.BlockSpec((1, 1, D), lambda b, i, j: (b, 0, 0)),
                  pl.BlockSpec((1, 1, D), lambda b, i, j: (b, 0, 0)),
                  pl.BlockSpec((D, tn), lambda b, i, j: (0, j))],
        out_specs=pl.BlockSpec((1, tm, tn), lambda b, i, j: (b, i, j)),
        out_shape=jax.ShapeDtypeStruct((B, S, N), BF16),
        scratch_shapes=[pltpu.VMEM((tm, D), BF16)],
        compiler_params=_params(3),
        name="norm_mod_proj",
    )(x, g.reshape(1, D), sh, sc, w)


def _nm_pair_body(x_ref, g_ref, sh_ref, sc_ref, w_ref, wt_ref, o_ref, ot_ref, h_sc, *, n_plain):
    j = pl.program_id(2)

    @pl.when(j == 0)
    def _():
        h_sc[...] = _norm_mod(x_ref[0], g_ref[...], sh_ref[0], sc_ref[0]).astype(BF16)

    @pl.when(j < n_plain)
    def _():
        o_ref[0] = jnp.dot(h_sc[...], w_ref[...], preferred_element_type=F32).astype(o_ref.dtype)

    @pl.when(j >= n_plain)
    def _():
        res = lax.dot_general(wt_ref[...], h_sc[...], _NT, preferred_element_type=F32)
        for r in range(ot_ref.shape[1]):
            ot_ref[0, r] = res[:, r * ATT_BLOCK:(r + 1) * ATT_BLOCK].astype(ot_ref.dtype)


def _norm_mod_matmul_pair(x, g, sh, sc, w, wt):
    B, S, D = x.shape
    N1, N2 = w.shape[1], wt.shape[0]
    tm, tn = 1024, 1024
    T = ATT_BLOCK
    n_plain = N1 // tn
    return pl.pallas_call(
        functools.partial(_nm_pair_body, n_plain=n_plain),
        grid=(B, S // tm, n_plain + N2 // tn),
        in_specs=[pl.BlockSpec((1, tm, D), lambda b, i, j: (b, i, 0)),
                  pl.BlockSpec((1, D), lambda b, i, j: (0, 0)),
                  pl.BlockSpec((1, 1, D), lambda b, i, j: (b, 0, 0)),
                  pl.BlockSpec((1, 1, D), lambda b, i, j: (b, 0, 0)),
                  pl.BlockSpec((D, tn), lambda b, i, j: (0, jnp.minimum(j, n_plain - 1))),
                  pl.BlockSpec((tn, D), lambda b, i, j: (jnp.maximum(j - n_plain, 0), 0))],
        out_specs=[pl.BlockSpec((1, tm, tn), lambda b, i, j: (b, i, jnp.minimum(j, n_plain - 1))),
                   pl.BlockSpec((1, tm // T, tn, T), lambda b, i, j: (b, i, jnp.maximum(j - n_plain, 0), 0))],
        out_shape=[jax.ShapeDtypeStruct((B, S, N1), BF16),
                   jax.ShapeDtypeStruct((B, S // T, N2, T), BF16)],
        scratch_shapes=[pltpu.VMEM((tm, D), BF16)],
        compiler_params=_params(3),
        name="norm_mod_proj_pair",
    )(x, g.reshape(1, D), sh, sc, w, wt)


def _mm_res_body(a_ref, w_ref, r_ref, gate_ref, o_ref):
    acc = jnp.dot(a_ref[0], w_ref[...], preferred_element_type=F32)
    o_ref[0] = r_ref[0] + gate_ref[0] * acc


def _matmul_gated_residual(a, w, res, gate):
    B, S, K = a.shape
    N = w.shape[1]
    tm, tn = 1024, 1024
    return pl.pallas_call(
        _mm_res_body,
        grid=(B, S // tm, N // tn),
        in_specs=[pl.BlockSpec((1, tm, K), lambda b, i, j: (b, i, 0)),
                  pl.BlockSpec((K, tn), lambda b, i, j: (0, j)),
                  pl.BlockSpec((1, tm, tn), lambda b, i, j: (b, i, j)),
                  pl.BlockSpec((1, 1, tn), lambda b, i, j: (b, 0, j))],
        out_specs=pl.BlockSpec((1, tm, tn), lambda b, i, j: (b, i, j)),
        out_shape=jax.ShapeDtypeStruct((B, S, N), F32),
        compiler_params=_params(3),
        name="out_proj_residual",
    )(a, w, res, gate)


def _pv_t(vt, ps, head_row):
    zeros = jnp.zeros_like(vt)
    vcat = jnp.concatenate(
        [jnp.where((head_row >= hh * HEAD_DIM) & (head_row < (hh + 1) * HEAD_DIM), vt, zeros)
         for hh in range(HEADS_PER_TILE)], axis=1)
    return jnp.dot(vcat, jnp.concatenate(ps, axis=0), preferred_element_type=F32)


def _per_head_rows(head_row, vals):
    assert HEADS_PER_TILE == 2
    return jnp.where(head_row < HEAD_DIM, vals[0], vals[1])


def _head_queries(q, lane):
    out = []
    for hh in range(HEADS_PER_TILE):
        in_head = (lane >= hh * HEAD_DIM) & (lane < (hh + 1) * HEAD_DIM)
        out.append(jnp.where(in_head, q, jnp.zeros_like(q)))
    return out


def _rel_bucket(dist):
    max_exact = N_BUCKETS // 2
    d = np.maximum(dist, 1).astype(np.float32)
    ratio = np.log(d / np.float32(max_exact)) / np.float32(math.log(MAX_DISTANCE / max_exact))
    large = max_exact + (ratio * np.float32(N_BUCKETS - max_exact)).astype(np.int32)
    large = np.minimum(large, N_BUCKETS - 1)
    return np.where(dist < max_exact, dist, large)


def _bias_tiles(rel_bias):
    T = ATT_BLOCK
    W = 2 * T
    H = rel_bias.shape[1]
    k = np.arange(W)
    lag = np.where(k <= T, k, k - W)
    dist = np.maximum(np.arange(N_BIAS_TILES)[:, None] * T + lag[None, :], 0)
    bucket = _rel_bucket(dist)
    assert (bucket[N_BIAS_TILES - 1] == N_BUCKETS - 1).all()
    onehot = np.eye(N_BUCKETS, dtype=np.float32)[bucket.reshape(-1)]
    vec = jnp.dot(jnp.asarray(onehot), rel_bias, precision=HIGHEST)
    a = vec.T.reshape(H, N_BIAS_TILES, W)
    tiled = jnp.tile(a, (1, 1, T))
    return tiled[:, :, :T * (W - 1)].reshape(H, N_BIAS_TILES, T, W - 1)[:, :, :, :T]


MOBA_TILES_PER_STEP = 2


def _moba_body(q_ref, k_ref, vt_ref, bias_ref, o_ref, kmean_sc, sel_sc, s_sc, p_sc, *, nb):
    step = pl.program_id(2)
    T = ATT_BLOCK
    tiles = range(MOBA_TILES_PER_STEP)
    heads = range(HEADS_PER_TILE)
    assert MOBA_TILES_PER_STEP == 2
    c0 = step * MOBA_TILES_PER_STEP
    cs = [c0 + u for u in tiles]

    @pl.when(step == 0)
    def _():
        rest = jnp.concatenate(
            [jnp.sum(k_ref[0, n * T:(n + 1) * T, :].astype(F32), axis=0, keepdims=True) * (1.0 / T)
             for n in range(nb)], axis=0)
        for part in range(3):
            term = rest.astype(BF16)
            kmean_sc[part * nb:(part + 1) * nb, :] = term
            rest = rest - term.astype(F32)

    lane = lax.broadcasted_iota(jnp.int32, (T, LANES), 1)
    blk = lax.broadcasted_iota(jnp.int32, (nb, T), 0)
    key_i = lax.broadcasted_iota(jnp.int32, (T, T), 0)
    qry_i = lax.broadcasted_iota(jnp.int32, (T, T), 1)
    causal = key_i <= qry_i
    head_row = lax.broadcasted_iota(jnp.int32, (LANES, T), 0)
    kmean = kmean_sc[...]
    scale = HEAD_DIM ** -0.5

    qs = []
    for u in tiles:
        qs.append([])
        for hh, qh in enumerate(_head_queries(q_ref[0, u * T:(u + 1) * T, :], lane)):
            terms = lax.dot_general(kmean, qh, _NT, preferred_element_type=F32)
            gate = terms[:nb] + terms[nb:2 * nb] + terms[2 * nb:]
            gate = jnp.where(blk < cs[u], gate, NEG)
            rank = jnp.zeros((nb, T), jnp.int32)
            for m in range(nb):
                gm = gate[m:m + 1, :]
                beats = (gm > gate) | ((gm == gate) & (blk > m))
                rank = rank + beats.astype(jnp.int32)
            sel_sc[u, hh] = ((rank < MOBA_TOPK) & (blk < cs[u])).astype(F32)
            qs[u].append((qh.astype(F32) * scale).astype(BF16))

    def scores(n, which=tiles):
        kb = k_ref[0, pl.ds(pl.multiple_of(n * T, T), T), :]
        return {u: [lax.dot_general(kb, qs[u][hh], _NT, preferred_element_type=F32) for hh in heads]
                for u in which}

    acc_rows = LANES + 16
    acc_row = lax.broadcasted_iota(jnp.int32, (acc_rows, T), 0)
    first_head = (acc_row < HEAD_DIM) | ((acc_row >= LANES) & (acc_row < LANES + 8))
    grp = lax.broadcasted_iota(jnp.int32, (16, T), 0) // 8
    ones_rows = [(grp == hh).astype(BF16) for hh in heads]

    def soften(u, n, raw, ms):
        dl = jnp.minimum(cs[u] - n, N_BIAS_TILES - 1)
        new_ms, alphas = [], []
        for hh in heads:
            m = ms[hh]
            picked = sel_sc[u, hh, pl.ds(n, 1), :] > 0.5
            s = raw[hh] + bias_ref[hh, pl.ds(dl, 1)][0]
            m_new = jnp.where(picked, jnp.maximum(m, jnp.max(s, axis=0, keepdims=True)), m)
            p_sc[u, hh] = jnp.exp(s - jnp.where(picked, m_new, -NEG)).astype(BF16)
            new_ms.append(m_new)
            alphas.append(jnp.exp(m - m_new))
        return new_ms, alphas

    def value_rows(blk_idx):
        vt = vt_ref[0, blk_idx]
        zeros = jnp.zeros_like(vt)
        return jnp.concatenate(
            [jnp.concatenate([jnp.where((head_row >= hh * HEAD_DIM) & (head_row < (hh + 1) * HEAD_DIM),
                                        vt, zeros), ones_rows[hh]], axis=0) for hh in heads],
            axis=1)

    def apply_pending(u, lhs, alphas, acc):
        ps = jnp.concatenate([p_sc[u, hh] for hh in heads], axis=0)
        return (jnp.where(first_head, alphas[0], alphas[1]) * acc
                + jnp.dot(lhs, ps, preferred_element_type=F32))

    def own_block(u, raw):
        ms = []
        for hh in heads:
            s = jnp.where(causal, raw[hh] + bias_ref[hh, 0], NEG)
            m = jnp.max(s, axis=0, keepdims=True)
            p_sc[u, hh] = jnp.exp(s - m).astype(BF16)
            ms.append(m)
        return ms

    raw_own = {u: scores(cs[u], [u])[u] for u in tiles}
    raw_extra = scores(c0, [1])[1]
    raw_next = scores(0)
    for u in tiles:
        for hh in heads:
            s_sc[u, hh] = raw_next[u][hh]

    ones = jnp.ones((1, T), F32)
    zero_acc = jnp.zeros((acc_rows, T), F32)
    ms = {u: own_block(u, raw_own[u]) for u in tiles}
    alphas = {0: [ones, ones]}
    accs = {0: zero_acc}
    accs[1] = apply_pending(1, value_rows(cs[1]), [ones, ones], zero_acc)
    ms[1], alphas[1] = soften(1, c0, raw_extra, ms[1])

    def pack(ms, alphas, accs):
        return tuple(ms[0]) + tuple(ms[1]) + tuple(alphas[0]) + tuple(alphas[1]) + (accs[0], accs[1])

    def unpack(carry):
        return ({0: list(carry[0:2]), 1: list(carry[2:4])}, {0: list(carry[4:6]), 1: list(carry[6:8])},
                {0: carry[8], 1: carry[9]})

    def body(t, carry):
        ms, alphas, accs = unpack(carry)
        raw_t = {u: [s_sc[u, hh] for hh in heads] for u in tiles}
        lhs = value_rows(jnp.where(t == 0, c0, t - 1))
        for u in tiles:
            accs[u] = apply_pending(u, lhs, alphas[u], accs[u])
        raw_n = scores(jnp.minimum(t + 1, c0 - 1))
        for u in tiles:
            ms[u], alphas[u] = soften(u, t, raw_t[u], ms[u])
        for u in tiles:
            for hh in heads:
                s_sc[u, hh] = raw_n[u][hh]
        return pack(ms, alphas, accs)

    ms, alphas, accs = unpack(lax.fori_loop(
        0, step, lambda i, carry: body(2 * i + 1, body(2 * i, carry)), pack(ms, alphas, accs)))
    lhs = value_rows(jnp.maximum(c0 - 1, 0))
    for u in tiles:
        acc = apply_pending(u, lhs, alphas[u], accs[u])
        norm = _per_head_rows(head_row, [acc[LANES:LANES + 1], acc[LANES + 8:LANES + 9]])
        o_ref[0, u * T:(u + 1) * T, :] = (acc[:LANES] / norm).T.astype(o_ref.dtype)


def _moba_attention(qk, vt, bias_tiles):
    B, S, D2 = qk.shape
    D = D2 // 2
    T = ATT_BLOCK
    nb = S // T
    n_hp = D // LANES
    tq = MOBA_TILES_PER_STEP * T
    return pl.pallas_call(
        functools.partial(_moba_body, nb=nb),
        grid=(B, n_hp, S // tq),
        in_specs=[pl.BlockSpec((1, tq, LANES), lambda b, h, i: (b, i, h)),
                  pl.BlockSpec((1, S, LANES), lambda b, h, i: (b, 0, n_hp + h)),
                  pl.BlockSpec((1, nb, LANES, T), lambda b, h, i: (b, 0, h, 0)),
                  pl.BlockSpec((HEADS_PER_TILE, N_BIAS_TILES, T, T), lambda b, h, i: (h, 0, 0, 0))],
        out_specs=pl.BlockSpec((1, tq, LANES), lambda b, h, i: (b, i, h)),
        out_shape=jax.ShapeDtypeStruct((B, S, D), BF16),
        scratch_shapes=[pltpu.VMEM((3 * nb, LANES), BF16),
                        pltpu.VMEM((MOBA_TILES_PER_STEP, HEADS_PER_TILE, nb, T), F32),
                        pltpu.VMEM((MOBA_TILES_PER_STEP, HEADS_PER_TILE, T, T), F32),
                        pltpu.VMEM((MOBA_TILES_PER_STEP, HEADS_PER_TILE, T, T), BF16)],
        compiler_params=_params(3),
        name="moba_attention",
    )(qk, qk, vt, bias_tiles)


SB_TILES_PER_STEP = 4


def _sb_body(q_ref, k_ref, vt_ref, o_ref):
    step = pl.program_id(2)
    T = ATT_BLOCK
    lane = lax.broadcasted_iota(jnp.int32, (T, LANES), 1)
    key_i = lax.broadcasted_iota(jnp.int32, (T, T), 0)
    qry_i = lax.broadcasted_iota(jnp.int32, (T, T), 1)
    strict = key_i < qry_i
    later = (qry_i > key_i).astype(BF16)
    later2 = jnp.concatenate([later, later], axis=1)
    head_row = lax.broadcasted_iota(jnp.int32, (LANES, T), 0)
    scale = HEAD_DIM ** -0.5
    cs = [step * SB_TILES_PER_STEP + u for u in range(SB_TILES_PER_STEP)]
    qs = [[(qh.astype(F32) * scale).astype(BF16)
           for qh in _head_queries(q_ref[0, u * T:(u + 1) * T, :], lane)] for u in range(SB_TILES_PER_STEP)]

    def run(jobs, states):
        rs = {u: list(st[:HEADS_PER_TILE]) for u, st in states.items()}
        accs = {u: st[-1] for u, st in states.items()}
        items = [(ji, hh) for ji in range(len(jobs)) for hh in range(HEADS_PER_TILE)]
        kbs = [k_ref[0, pl.ds(pl.multiple_of(n * T, T), T), :] for _, n, _, _ in jobs]
        z = {(ji, hh): lax.dot_general(kbs[ji], qs[jobs[ji][0]][hh], _NT, preferred_element_type=F32)
             for ji, hh in items}
        lb, l1, hilo = {}, {}, {}
        for t in items:
            sp = jnp.log(1.0 + jnp.exp(-jnp.abs(z[t])))
            lb[t] = jnp.minimum(z[t], 0.0) - sp
            l1[t] = lb[t] - z[t]
            if jobs[t[0]][2]:
                l1[t] = jnp.where(strict, l1[t], 0.0)
            hi = l1[t].astype(BF16)
            lo = (l1[t] - hi.astype(F32)).astype(BF16)
            hilo[t] = jnp.concatenate([hi, lo], axis=0)
        tail = {t: jnp.dot(later2, hilo[t], preferred_element_type=F32) for t in items}
        ps = {}
        for ji, hh in items:
            u, _, diag, keep = jobs[ji]
            a = jnp.exp(lb[ji, hh] + tail[ji, hh] + rs[u][hh])
            if diag:
                a = jnp.where(strict, a, 0.0)
            ps[ji, hh] = a.astype(BF16)
            inc = tail[ji, hh][0:1, :] + l1[ji, hh][0:1, :]
            if keep is not None:
                inc = jnp.where(keep, inc, 0.0)
            rs[u][hh] = rs[u][hh] + inc
        for ji, (u, n, _, keep) in enumerate(jobs):
            vt = vt_ref[0, n]
            if keep is not None:
                vt = jnp.where(keep, vt, jnp.zeros_like(vt))
            accs[u] = accs[u] + _pv_t(vt, [ps[ji, hh] for hh in range(HEADS_PER_TILE)], head_row)
        return {u: tuple(rs[u]) + (accs[u],) for u in states}

    zero = (jnp.zeros((1, T), F32),) * HEADS_PER_TILE + (jnp.zeros((LANES, T), F32),)
    jobs = []
    for u, c in enumerate(cs):
        jobs += [(u, c, True, None), (u, jnp.maximum(c - 1, 0), False, (c > 0) if u == 0 else None)]
    states = run(jobs, {u: zero for u in range(SB_TILES_PER_STEP)})

    for u, c in enumerate(cs):
        def cond(carry, c=c):
            live = jnp.max(jnp.maximum(carry[1], carry[2])) > SB_DEAD_LOG
            return jnp.logical_and(carry[0] < c, live)

        def body(carry, u=u, c=c):
            i = carry[0]
            return (i + 1,) + run([(u, c - 1 - i, False, None)], {u: carry[1:]})[u]

        fin = lax.while_loop(cond, body, (jnp.int32(1),) + states[u])
        o_ref[0, u * T:(u + 1) * T, :] = fin[-1].T.astype(o_ref.dtype)


def _sb_attention(q, k, vt):
    B, S, D = q.shape
    T = ATT_BLOCK
    nb = S // T
    n_hp = D // LANES
    tq = SB_TILES_PER_STEP * T
    return pl.pallas_call(
        _sb_body,
        grid=(B, n_hp, S // tq),
        in_specs=[pl.BlockSpec((1, tq, LANES), lambda b, h, i: (b, i, h)),
                  pl.BlockSpec((1, S, LANES), lambda b, h, i: (b, 0, h)),
                  pl.BlockSpec((1, nb, LANES, T), lambda b, h, i: (b, 0, h, 0))],
        out_specs=pl.BlockSpec((1, tq, LANES), lambda b, h, i: (b, i, h)),
        out_shape=jax.ShapeDtypeStruct((B, S, D), BF16),
        compiler_params=_params(3),
        name="stick_breaking_attention",
    )(q, k, vt)


def _swiglu_chunk(xb, wg, wu, w2):
    g = jnp.dot(xb, wg, preferred_element_type=F32)
    u = jnp.dot(xb, wu, preferred_element_type=F32)
    a = (g * jax.nn.sigmoid(g) * u).astype(BF16)
    return jnp.dot(a, w2, preferred_element_type=F32)


def _dense_ffn_body(x_ref, g_ref, sh_ref, sc_ref, gate_ref, wg_ref, wu_ref, w2_ref, o_ref,
                    xb_sc, acc_sc, *, n_chunks):
    j = pl.program_id(1)

    @pl.when(j == 0)
    def _():
        xb_sc[...] = _norm_mod(x_ref[...], g_ref[...], sh_ref[0], sc_ref[0]).astype(BF16)

        @pl.when(pl.program_id(0) == 0)
        def _():
            acc_sc[...] = jnp.zeros_like(acc_sc)

    y = _swiglu_chunk(xb_sc[...], wg_ref[0], wu_ref[0], w2_ref[0])
    acc_sc[...] = y + jnp.where(j > 0, acc_sc[...], 0.0)

    @pl.when(j == n_chunks - 1)
    def _():
        o_ref[...] = x_ref[...] + gate_ref[0] * acc_sc[...]


def _dense_ffn(x, g, sh, sc, gate, w13, w2):
    B, S, D = x.shape
    Fh = w2.shape[1]
    tm, tf = 512, 1408
    n_chunks = Fh // tf
    per_b = S // tm
    xf = x.reshape(B * S, D)
    out = pl.pallas_call(
        functools.partial(_dense_ffn_body, n_chunks=n_chunks),
        grid=(B * S // tm, n_chunks),
        in_specs=[pl.BlockSpec((tm, D), lambda i, j: (i, 0)),
                  pl.BlockSpec((1, D), lambda i, j: (0, 0)),
                  pl.BlockSpec((1, 1, D), lambda i, j: (i // per_b, 0, 0)),
                  pl.BlockSpec((1, 1, D), lambda i, j: (i // per_b, 0, 0)),
                  pl.BlockSpec((1, 1, D), lambda i, j: (i // per_b, 0, 0)),
                  pl.BlockSpec((1, D, tf), lambda i, j: (0, 0, j)),
                  pl.BlockSpec((1, D, tf), lambda i, j: (0, 0, j + n_chunks)),
                  pl.BlockSpec((1, tf, D), lambda i, j: (0, j, 0))],
        out_specs=pl.BlockSpec((tm, D), lambda i, j: (i, 0)),
        out_shape=jax.ShapeDtypeStruct((B * S, D), F32),
        scratch_shapes=[pltpu.VMEM((tm, D), BF16), pltpu.VMEM((tm, D), F32)],
        compiler_params=_params(2),
        name="dense_swiglu",
    )(xf, g.reshape(1, D), sh, sc, gate, w13, w13, w2)
    return out.reshape(B, S, D)


def _router_body(x_ref, g_ref, sh_ref, sc_ref, rwt_ref, h_ref, ei_ref, w_ref):
    h = _norm_mod(x_ref[0], g_ref[...], sh_ref[0], sc_ref[0])
    h_ref[...] = h
    logits = lax.dot_general(rwt_ref[...], h, _NT, preferred_element_type=F32,
                             precision=HIGHEST)
    n_e = logits.shape[0]
    eio = lax.broadcasted_iota(jnp.int32, logits.shape, 0)
    m1 = jnp.max(logits, axis=0, keepdims=True)
    i1 = jnp.min(jnp.where(logits == m1, eio, n_e), axis=0, keepdims=True)
    rest = jnp.where(eio == i1, -jnp.inf, logits)
    m2 = jnp.max(rest, axis=0, keepdims=True)
    i2 = jnp.min(jnp.where(rest == m2, eio, n_e), axis=0, keepdims=True)
    t = jnp.exp(m2 - m1)
    den = 1.0 + t
    ei_ref[...] = jnp.concatenate([i1, i2], axis=0)
    w_ref[...] = jnp.concatenate([1.0 / den, t / den], axis=0)


def _router(x, g, sh, sc, router_w):
    B, S, D = x.shape
    E = router_w.shape[1]
    tm = 512
    per_b = S // tm
    return pl.pallas_call(
        _router_body,
        grid=(B, per_b),
        in_specs=[pl.BlockSpec((1, tm, D), lambda b, i: (b, i, 0)),
                  pl.BlockSpec((1, D), lambda b, i: (0, 0)),
                  pl.BlockSpec((1, 1, D), lambda b, i: (b, 0, 0)),
                  pl.BlockSpec((1, 1, D), lambda b, i: (b, 0, 0)),
                  pl.BlockSpec((E, D), lambda b, i: (0, 0))],
        out_specs=[pl.BlockSpec((tm, D), lambda b, i: (b * per_b + i, 0)),
                   pl.BlockSpec((TOP_K_EXPERTS, tm), lambda b, i: (0, b * per_b + i)),
                   pl.BlockSpec((TOP_K_EXPERTS, tm), lambda b, i: (0, b * per_b + i))],
        out_shape=[jax.ShapeDtypeStruct((B * S, D), F32),
                   jax.ShapeDtypeStruct((TOP_K_EXPERTS, B * S), jnp.int32),
                   jax.ShapeDtypeStruct((TOP_K_EXPERTS, B * S), F32)],
        compiler_params=_params(2),
        name="moe_router",
    )(x, g.reshape(1, D), sh, sc, router_w.T)


def _route_plan(ei, tm, n_tiles):
    n_src = TOP_K_EXPERTS * ei.shape[1]
    assert n_src & (n_src - 1) == 0
    i32 = jnp.int32
    keys = jnp.sort(ei.reshape(-1) * n_src + jnp.arange(n_src, dtype=i32))
    runs = jnp.sum(keys[None, :] < (jnp.arange(N_EXPERTS + 1, dtype=i32) * n_src)[:, None], axis=1, dtype=i32)
    first, counts = runs[:-1], runs[1:] - runs[:-1]
    padded = ((counts + tm - 1) // tm) * tm
    ends = jnp.cumsum(padded)
    offs = ends - padded
    starts = jnp.arange(n_tiles, dtype=i32) * tm
    t_exp = jnp.minimum(jnp.sum(ends[None, :] <= starts[:, None], axis=1, dtype=i32), N_EXPERTS - 1)
    t_rows = jnp.clip(offs[t_exp] + counts[t_exp] - starts, 0, tm)
    t_rows = jnp.where(starts < ends[-1], t_rows, 0).astype(i32)
    r = jnp.arange(tm, dtype=i32)[None, :]
    nth = (first[t_exp] + starts - offs[t_exp])[:, None] + r
    real = r < t_rows[:, None]
    body = jnp.where(real, jnp.take(keys, jnp.clip(nth, 0, n_src - 1)) & (n_src - 1), n_src + r)
    flat = jnp.concatenate([n_src + r[0], body.reshape(-1)])
    return flat, t_exp, t_rows


def _moe_ffn_body(te_ref, tr_ref, flat_ref, h_hbm, wg_ref, wu_ref, w2_ref, ys_hbm,
                  xstage, xb_sc, acc_sc, ybuf, gsem, ssem, *, tm, n_chunks, n_tiles, n_tok):
    i = pl.program_id(0)
    j = pl.program_id(1)
    valid = tr_ref[i] > 0
    rows_per_step = tm // n_chunks

    def gather_row(tile, q, rr):
        tok = flat_ref[(tile + 1) * tm + q * rows_per_step + rr] & (n_tok - 1)
        return pltpu.make_async_copy(h_hbm.at[pl.ds(tok, 1)], xstage.at[q, pl.ds(rr, 1)], gsem)

    def scatter_row(tile, q, rr):
        dst = flat_ref[(tile + 1) * tm + q * rows_per_step + rr]
        return pltpu.make_async_copy(ybuf.at[q, pl.ds(rr, 1)], ys_hbm.at[pl.ds(dst, 1)], ssem)

    def gather_wait(steps):
        for q in range(steps):
            pltpu.make_async_copy(h_hbm.at[pl.ds(0, rows_per_step)], xstage.at[q], gsem).wait()

    def scatter_wait(steps):
        for q in range(steps):
            pltpu.make_async_copy(ybuf.at[q], ys_hbm.at[pl.ds(0, rows_per_step)], ssem).wait()

    def all_rows(start_row):
        def one(q, _):
            for rr in range(rows_per_step):
                start_row(q, rr).start()
            return 0
        lax.fori_loop(0, n_chunks, one, 0)

    @pl.when(valid)
    def _():
        @pl.when(j == 0)
        def _():
            @pl.when(i == 0)
            def _():
                all_rows(functools.partial(gather_row, 0))
                gather_wait(n_chunks)
                ybuf[...] = jnp.zeros_like(ybuf)
                acc_sc[...] = jnp.zeros_like(acc_sc)

            xb_sc[...] = xstage[...].reshape(tm, -1).astype(BF16)

        for rr in range(rows_per_step):
            gather_row(i + 1, j, rr).start()
            scatter_row(i - 1, j, rr).start()

        xb = xb_sc[...]
        g = jnp.dot(xb, wg_ref[0].astype(BF16), preferred_element_type=F32)
        u = jnp.dot(xb, wu_ref[0].astype(BF16), preferred_element_type=F32)
        a = (g * jax.nn.sigmoid(g) * u).astype(BF16)
        gather_wait(1)
        scatter_wait(1)
        y = jnp.dot(a, w2_ref[0].astype(BF16), preferred_element_type=F32)
        acc_sc[...] = y + jnp.where(j > 0, acc_sc[...], 0.0)

        @pl.when(j == n_chunks - 1)
        def _():
            ybuf[...] = acc_sc[...].reshape(ybuf.shape)

    prv = jnp.maximum(i - 1, 0)

    @pl.when(jnp.logical_and(jnp.logical_and(jnp.logical_not(valid), j == 0),
                             jnp.logical_and(i > 0, tr_ref[prv] > 0)))
    def _():
        all_rows(functools.partial(scatter_row, prv))
        scatter_wait(n_chunks)


def _moe_ffn(h, flat, t_exp, t_rows, w13, w2, tm):
    n_tok, D = h.shape
    Fh = w2.shape[1]
    n_tiles = t_exp.shape[0]
    n_chunks = 4
    tf = Fh // n_chunks
    assert tf % LANES == 0 and tm % n_chunks == 0 and n_tok & (n_tok - 1) == 0 and n_tok >= tm

    def wmap(col_off):
        return lambda i, j, te, tr, fl: (te[i], 0, j * jnp.minimum(tr[i], 1) + col_off)

    return pl.pallas_call(
        functools.partial(_moe_ffn_body, tm=tm, n_chunks=n_chunks, n_tiles=n_tiles, n_tok=n_tok),
        grid_spec=pltpu.PrefetchScalarGridSpec(
            num_scalar_prefetch=3,
            grid=(n_tiles, n_chunks),
            in_specs=[pl.BlockSpec(memory_space=pl.ANY),
                      pl.BlockSpec((1, D, tf), wmap(0)),
                      pl.BlockSpec((1, D, tf), wmap(n_chunks)),
                      pl.BlockSpec((1, tf, D), lambda i, j, te, tr, fl: (te[i], j * jnp.minimum(tr[i], 1), 0))],
            out_specs=pl.BlockSpec(memory_space=pl.ANY),
            scratch_shapes=[pltpu.VMEM((n_chunks, tm // n_chunks, D), F32), pltpu.VMEM((tm, D), BF16),
                            pltpu.VMEM((tm, D), F32), pltpu.VMEM((n_chunks, tm // n_chunks, D), F32),
                            pltpu.SemaphoreType.DMA, pltpu.SemaphoreType.DMA]),
        out_shape=jax.ShapeDtypeStruct((TOP_K_EXPERTS * n_tok + tm, D), F32),
        compiler_params=_params(2),
        name="moe_expert_swiglu",
    )(t_exp, t_rows, flat, h, w13, w13, w2)


def _combine_body(x_ref, y0_ref, y1_ref, w_ref, gate_ref, fg_ref, o_ref, *, final_norm):
    w = w_ref[...]
    y = w[:, 0:1] * y0_ref[...] + w[:, 1:2] * y1_ref[...]
    out = x_ref[...] + gate_ref[0] * y
    if final_norm:
        ms = jnp.mean(out * out, axis=-1, keepdims=True)
        out = (out * lax.rsqrt(ms + EPS)) * fg_ref[...]
    o_ref[...] = out


def _moe_combine(x, ys, wts, gate, final_g, final_norm):
    B, S, D = x.shape
    n_tok = B * S
    rows = 512
    per_b = S // rows
    out = pl.pallas_call(
        functools.partial(_combine_body, final_norm=final_norm),
        grid=(n_tok // rows,),
        in_specs=[pl.BlockSpec((rows, D), lambda i: (i, 0)),
                  pl.BlockSpec((rows, D), lambda i: (i, 0)),
                  pl.BlockSpec((rows, D), lambda i: (n_tok // rows + i, 0)),
                  pl.BlockSpec((rows, TOP_K_EXPERTS), lambda i: (i, 0)),
                  pl.BlockSpec((1, 1, D), lambda i: (i // per_b, 0, 0)),
                  pl.BlockSpec((1, D), lambda i: (0, 0))],
        out_specs=pl.BlockSpec((rows, D), lambda i: (i, 0)),
        out_shape=jax.ShapeDtypeStruct((n_tok, D), F32),
        compiler_params=_params(1),
        name="moe_combine",
    )(x.reshape(n_tok, D), ys, ys, wts, gate, final_g.reshape(1, D))
    return out.reshape(B, S, D)


def _moe_layer(x, g, sh, sc, gate, router_w, w13, w2, first_expert, final_g, final_norm):
    B, S, D = x.shape
    n_tok = B * S
    tm = 1024
    n_tiles = TOP_K_EXPERTS * n_tok // tm + N_EXPERTS
    h, ei, wts = _router(x, g, sh, sc, router_w)
    flat, t_exp, t_rows = _route_plan(ei, tm, n_tiles)
    ys = _moe_ffn(h, flat, t_exp + first_expert, t_rows, w13, w2, tm)
    return _moe_combine(x, ys, wts.T, gate, final_g, final_norm)


def kernel(x, c, rel_bias, mod_w, mod_b, norm_mix_g, norm_ffn_g, a_wqkv, a_wo, kv_norm_g, kv_mod_w,
           kv_mod_b, b_wkv, b_wq, b_wo, ffn_w13, ffn_w2, router_w, moe_w13, moe_w2, final_norm_g):
    D = D_MODEL
    mods = _modulation(c, mod_w, mod_b)
    kv_mod = _modulation(c, kv_mod_w[None], kv_mod_b[None])[0]
    bias_tiles = _bias_tiles(rel_bias)
    k_sh = vt_sh = None
    for l in range(DEPTH):
        sh_a, sc_a, g_a, sh_f, sc_f, g_f = [mods[l][:, None, i * D:(i + 1) * D] for i in range(6)]
        if l < N_A_LAYERS:
            w = a_wqkv[l].astype(BF16)
            qk, vt = _norm_mod_matmul_pair(x, norm_mix_g[l], sh_a, sc_a, w[:, :2 * D], w[:, 2 * D:].T)
            attn = _moba_attention(qk, vt, bias_tiles)
            x = _matmul_gated_residual(attn, a_wo[l].astype(BF16), x, g_a)
        else:
            j = l - N_A_LAYERS
            if j == 0:
                w = b_wkv.astype(BF16)
                kv_sh, kv_sc = kv_mod[:, None, :D], kv_mod[:, None, D:]
                k_sh, vt_sh = _norm_mod_matmul_pair(x, kv_norm_g, kv_sh, kv_sc, w[:, :D], w[:, D:].T)
            q = _norm_mod_matmul(x, norm_mix_g[l], sh_a, sc_a, b_wq[j].astype(BF16))
            attn = _sb_attention(q, k_sh, vt_sh)
            x = _matmul_gated_residual(attn, b_wo[j].astype(BF16), x, g_a)
        if l % 2 == 0:
            x = _dense_ffn(x, norm_ffn_g[l], sh_f, sc_f, g_f,
                           ffn_w13[l // 2][None].astype(BF16), ffn_w2[l // 2][None].astype(BF16))
        else:
            x = _moe_layer(x, norm_ffn_g[l], sh_f, sc_f, g_f, router_w[l // 2],
                           moe_w13.reshape((-1,) + moe_w13.shape[2:]),
                           moe_w2.reshape((-1,) + moe_w2.shape[2:]), (l // 2) * N_EXPERTS,
                           final_norm_g, l == DEPTH - 1)
    return x
```
